```python
import jax
import jax.numpy as jnp
from jax import lax
import numpy as np

D_MODEL = 2048
BATCH = 1
SEQ = 8192
DEPTH = 1
DEC_BATCH = 128
DEC_SEQ = 8
PAST_LEN = 2048
PAGE_SIZE = 128

HEAD_DIM = 128
N_RET_HEADS = 8
N_SB_HEADS = 8
RET_WIDTH = N_RET_HEADS * HEAD_DIM
SB_WIDTH = N_SB_HEADS * HEAD_DIM
MIX_WIDTH = RET_WIDTH + SB_WIDTH
IN_WIDTH = 4 * RET_WIDTH + 3 * SB_WIDTH
RET_CHUNK = 128
SB_BLOCK = 128
SB_BIAS_INIT = -7.0
ROPE_BASE = 10000.0
N_MEM = 256
N_MEM_HEADS = 4
MEM_WIDTH = N_MEM_HEADS * HEAD_DIM
N_EXPERTS = 64
TOP_K = 8
D_EXPERT = 512
D_SHARED = 512
ROUTED_SCALE = 2.5
EXPERT_BLOCK = 128
RMS_EPS = 1e-6
NORM_EPS = 1e-6

kernel_name = 'hymba_retention_stickbreaking_moe_step'


def rmsnorm(x, g):
    xf = x.astype(jnp.float32)
    y = xf * lax.rsqrt(jnp.mean(xf * xf, axis=-1, keepdims=True) + RMS_EPS)
    return (y * g.astype(jnp.float32)).astype(x.dtype)


def head_layernorm(o, g):
    c = o - jnp.mean(o, axis=-1, keepdims=True)
    y = c * lax.rsqrt(jnp.mean(c * c, axis=-1, keepdims=True) + NORM_EPS)
    return y * g.astype(jnp.float32).reshape(o.shape[-2], o.shape[-1])


def head_rmsnorm(o, g):
    y = o * lax.rsqrt(jnp.mean(o * o, axis=-1, keepdims=True) + RMS_EPS)
    return y * g.astype(jnp.float32).reshape(o.shape[-2], o.shape[-1])


def rotary(x, pos):
    half = HEAD_DIM // 2
    inv_freq = ROPE_BASE ** (-jnp.arange(half, dtype=jnp.float32) / half)
    ang = pos.astype(jnp.float32)[:, None] * inv_freq[None, :]
    cos = jnp.cos(ang)[None, :, None, :]
    sin = jnp.sin(ang)[None, :, None, :]
    x1, x2 = x[..., :half], x[..., half:]
    return jnp.concatenate([x1 * cos - x2 * sin, x1 * sin + x2 * cos], axis=-1)


def retention_log_decay():
    return jnp.log1p(-jnp.exp2(-5.0 - jnp.arange(N_RET_HEADS, dtype=jnp.float32)))


def retention_chunk(q, k, v, s_prev, log_g):
    L = q.shape[1]
    i = jnp.arange(L, dtype=jnp.float32)
    diff = i[:, None] - i[None, :]
    decay = jnp.where(diff >= 0, jnp.exp(jnp.maximum(diff, 0.0)[None] * log_g[:, None, None]), 0.0)
    scores = jnp.einsum('blhd,bmhd->bhlm', q, k) * decay[None]
    o = jnp.einsum('bhlm,bmhe->blhe', scores, v)
    q_dec = q * jnp.exp((i[:, None] + 1.0) * log_g[None, :])[None, :, :, None]
    o = o + jnp.einsum('blhd,bhde->blhe', q_dec, s_prev)
    k_dec = k * jnp.exp((L - 1.0 - i)[:, None] * log_g[None, :])[None, :, :, None]
    s_new = jnp.exp(L * log_g)[None, :, None, None] * s_prev + jnp.einsum('blhd,blhe->bhde', k_dec, v)
    return o, s_new


def retention_prompt(q, k, v, log_g):
    b, t, h, d = q.shape
    nc = t // RET_CHUNK

    def to_chunks(a):
        return a.reshape(b, nc, RET_CHUNK, h, d).swapaxes(0, 1)

    def step(s, xs):
        qc, kc, vc = xs
        o, s = retention_chunk(qc, kc, vc, s, log_g)
        return s, o

    s0 = jnp.zeros((b, h, d, d), jnp.float32)
    s_fin, o = lax.scan(step, s0, (to_chunks(q), to_chunks(k), to_chunks(v)))
    return o.swapaxes(0, 1).reshape(b, t, h, d), s_fin


def stick_breaking_weights(z, mask):
    log_beta = jax.nn.log_sigmoid(z)
    log_keep = jnp.where(mask, jax.nn.log_sigmoid(-z), 0.0)
    between = lax.cumsum(log_keep, axis=z.ndim - 1, reverse=True) - log_keep
    return jnp.where(mask, jnp.exp(log_beta + between), 0.0)


def sb_prompt(q, k, v, bias):
    b, t, h, d = q.shape
    nb = t // SB_BLOCK
    qb = q.reshape(b, nb, SB_BLOCK, h, d).swapaxes(0, 1)
    k_pos = jnp.arange(t)
    bias = bias.astype(jnp.float32)[None, :, None, None]

    def one_block(args):
        q_blk, blk = args
        q_pos = blk * SB_BLOCK + jnp.arange(SB_BLOCK)
        z = jnp.einsum('bqhd,bkhd->bhqk', q_blk, k) * HEAD_DIM ** -0.5 + bias
        a = stick_breaking_weights(z, k_pos[None, :] < q_pos[:, None])
        return jnp.einsum('bhqk,bkhd->bqhd', a, v)

    o = lax.map(one_block, (qb, jnp.arange(nb)))
    return o.swapaxes(0, 1).reshape(b, t, h, d)


def sb_sample(q, k_new, v_new, k_past, v_past, bias):
    p = k_past.shape[1]
    l = q.shape[1]
    z = jnp.concatenate([jnp.einsum('bqhd,bkhd->bhqk', q, k_past),
                         jnp.einsum('bqhd,bkhd->bhqk', q, k_new)], axis=-1) * HEAD_DIM ** -0.5
    z = z + bias.astype(jnp.float32)[None, :, None, None]
    k_pos = jnp.arange(p + l)
    q_pos = p + jnp.arange(l)
    a = stick_breaking_weights(z, k_pos[None, :] < q_pos[:, None])
    return (jnp.einsum('bhqk,bkhd->bqhd', a[..., :p], v_past)
            + jnp.einsum('bhqk,bkhd->bqhd', a[..., p:], v_new))


def mem_kv(mem, norm_mem, w_ck, w_cv):
    b, m, _ = mem.shape
    mn = rmsnorm(mem, norm_mem)
    return ((mn @ w_ck).reshape(b, m, N_MEM_HEADS, HEAD_DIM),
            (mn @ w_cv).reshape(b, m, N_MEM_HEADS, HEAD_DIM))


def cross_attend(h, mem_k, mem_v, w_cq, w_co):
    b, t, _ = h.shape
    q = (h @ w_cq).reshape(b, t, N_MEM_HEADS, HEAD_DIM).astype(jnp.float32)
    s = jnp.einsum('bthd,bmhd->bhtm', q, mem_k.astype(jnp.float32)) * HEAD_DIM ** -0.5
    p = jax.nn.softmax(s, axis=-1)
    o = jnp.einsum('bhtm,bmhd->bthd', p, mem_v.astype(jnp.float32))
    return o.reshape(b, t, MEM_WIDTH).astype(h.dtype) @ w_co


def moe_ffn(h, router_w, router_bias, w_gate, w_up, w_down, ws_gate, ws_up, ws_down):
    t, d = h.shape
    scores = jax.nn.sigmoid((h @ router_w).astype(jnp.float32))
    _, idx = lax.top_k(scores + router_bias.astype(jnp.float32), TOP_K)
    sel = jnp.take_along_axis(scores, idx, axis=-1)
    gates = sel / jnp.sum(sel, axis=-1, keepdims=True) * ROUTED_SCALE
    n_assign = t * TOP_K
    flat_e = idx.reshape(-1)
    flat_tok = jnp.arange(n_assign) // TOP_K
    order = jnp.argsort(flat_e)
    e_sorted = flat_e[order]
    tok_sorted = flat_tok[order]
    g_sorted = gates.reshape(-1)[order]
    counts = jnp.bincount(flat_e, length=N_EXPERTS)
    padded = (counts + EXPERT_BLOCK - 1) // EXPERT_BLOCK * EXPERT_BLOCK
    pad_end = jnp.cumsum(padded)
    pad_start = pad_end - padded
    start = jnp.cumsum(counts) - counts
    slot = pad_start[e_sorted] + jnp.arange(n_assign) - start[e_sorted]
    n_blocks = -(-n_assign // EXPERT_BLOCK) + N_EXPERTS
    slot_tok = jnp.full((n_blocks * EXPERT_BLOCK,), t, jnp.int32).at[slot].set(tok_sorted.astype(jnp.int32))
    h_pad = jnp.concatenate([h, jnp.zeros((1, d), h.dtype)], axis=0)
    xb = h_pad[slot_tok].reshape(n_blocks, EXPERT_BLOCK, d)
    block_e = jnp.minimum(jnp.searchsorted(pad_end, jnp.arange(n_blocks) * EXPERT_BLOCK, side='right'), N_EXPERTS - 1)

    def expert_block(args):
        x_blk, e = args
        return (jax.nn.silu(x_blk @ w_gate[e]) * (x_blk @ w_up[e])) @ w_down[e]

    yb = lax.map(expert_block, (xb, block_e)).reshape(-1, d)
    routed = jnp.zeros((t, d), jnp.float32).at[tok_sorted].add(yb[slot].astype(jnp.float32) * g_sorted[:, None])
    shared = (jax.nn.silu(h @ ws_gate) * (h @ ws_up)) @ ws_down
    return routed.astype(h.dtype) + shared


def decoder_layer(x, pos, mixer_core, mem_k, mem_v, norm_mix, w_in, ret_gn, sb_gn, w_out, norm_cross,
                  w_cq, w_co, norm_ffn, router_w, router_bias, w_gate, w_up, w_down, ws_gate, ws_up, ws_down):
    b, t, d = x.shape
    h = rmsnorm(x, norm_mix)
    proj = (h @ w_in).astype(jnp.float32)
    cuts = [RET_WIDTH, 2 * RET_WIDTH, 3 * RET_WIDTH, 4 * RET_WIDTH,
            4 * RET_WIDTH + SB_WIDTH, 4 * RET_WIDTH + 2 * SB_WIDTH]
    rq, rk, rv, rg, sq, sk, sv = jnp.split(proj, cuts, axis=-1)

    def heads(a, n):
        return a.reshape(b, t, n, HEAD_DIM)

    rq = rotary(heads(rq, N_RET_HEADS), pos)
    rk = rotary(heads(rk, N_RET_HEADS), pos) * HEAD_DIM ** -0.5
    o_ret, o_sb, state = mixer_core(rq, rk, heads(rv, N_RET_HEADS),
                                    heads(sq, N_SB_HEADS), heads(sk, N_SB_HEADS), heads(sv, N_SB_HEADS))
    o_ret = head_layernorm(o_ret, ret_gn).reshape(b, t, RET_WIDTH) * jax.nn.silu(rg)
    o_sb = head_rmsnorm(o_sb, sb_gn).reshape(b, t, SB_WIDTH)
    x = x + jnp.concatenate([o_ret, o_sb], axis=-1).astype(x.dtype) @ w_out
    x = x + cross_attend(rmsnorm(x, norm_cross), mem_k, mem_v, w_cq, w_co)
    ffn = moe_ffn(rmsnorm(x, norm_ffn).reshape(b * t, d), router_w, router_bias,
                  w_gate, w_up, w_down, ws_gate, ws_up, ws_down)
    x = x + ffn.reshape(b, t, d)
    return x, state


def setup_inputs(seed: int = 0) -> dict:
    key = jax.random.key(seed)
    ks = iter(jax.random.split(key, 32))
    f32 = jnp.float32
    n_pages = PAST_LEN // PAGE_SIZE
    n_used = DEC_BATCH * n_pages
    n_phys = n_used + n_used // 4

    def normal(shape, scale=1.0):
        return jax.random.normal(next(ks), shape, f32) * scale

    def gain(shape):
        return 1.0 + 0.02 * jax.random.normal(next(ks), shape, f32)

    gam = 1.0 - jnp.exp2(-5.0 - jnp.arange(N_RET_HEADS, dtype=f32))
    ret_std = lax.rsqrt(HEAD_DIM * (1.0 - gam * gam))[None, None, :, None, None]
    x_prompt = normal((BATCH, SEQ, D_MODEL))
    x_sample = normal((DEC_BATCH, DEC_SEQ, D_MODEL))
    mem_prompt = normal((BATCH, N_MEM, D_MODEL))
    cache_sb_k = normal((DEPTH, n_phys, PAGE_SIZE, N_SB_HEADS, HEAD_DIM))
    cache_sb_v = normal((DEPTH, n_phys, PAGE_SIZE, N_SB_HEADS, HEAD_DIM))
    page_table = jax.random.permutation(next(ks), n_phys)[:n_used].reshape(DEC_BATCH, n_pages).astype(jnp.int32)
    state_ret = normal((DEPTH, DEC_BATCH, N_RET_HEADS, HEAD_DIM, HEAD_DIM)) * ret_std
    cache_mem_k = normal((DEPTH, DEC_BATCH, N_MEM, N_MEM_HEADS, HEAD_DIM))
    cache_mem_v = normal((DEPTH, DEC_BATCH, N_MEM, N_MEM_HEADS, HEAD_DIM))
    return {
        'x_prompt': x_prompt,
        'x_sample': x_sample,
        'mem_prompt': mem_prompt,
        'cache_sb_k': cache_sb_k,
        'cache_sb_v': cache_sb_v,
        'page_table': page_table,
        'state_ret': state_ret,
        'cache_mem_k': cache_mem_k,
        'cache_mem_v': cache_mem_v,
        'norm_mix': gain((DEPTH, D_MODEL)),
        'w_in': normal((DEPTH, D_MODEL, IN_WIDTH), D_MODEL ** -0.5),
        'ret_gn': gain((DEPTH, RET_WIDTH)),
        'sb_gn': gain((DEPTH, SB_WIDTH)),
        'sb_bias': SB_BIAS_INIT + normal((DEPTH, N_SB_HEADS), 0.1),
        'w_out': normal((DEPTH, MIX_WIDTH, D_MODEL), MIX_WIDTH ** -0.5),
        'norm_mem': gain((DEPTH, D_MODEL)),
        'w_ck': normal((DEPTH, D_MODEL, MEM_WIDTH), D_MODEL ** -0.5),
        'w_cv': normal((DEPTH, D_MODEL, MEM_WIDTH), D_MODEL ** -0.5),
        'norm_cross': gain((DEPTH, D_MODEL)),
        'w_cq': normal((DEPTH, D_MODEL, MEM_WIDTH), D_MODEL ** -0.5),
        'w_co': normal((DEPTH, MEM_WIDTH, D_MODEL), MEM_WIDTH ** -0.5),
        'norm_ffn': gain((DEPTH, D_MODEL)),
        'router_w': normal((DEPTH, D_MODEL, N_EXPERTS), D_MODEL ** -0.5),
        'router_bias': normal((DEPTH, N_EXPERTS), 0.01),
        'w_gate': normal((DEPTH, N_EXPERTS, D_MODEL, D_EXPERT), D_MODEL ** -0.5),
        'w_up': normal((DEPTH, N_EXPERTS, D_MODEL, D_EXPERT), D_MODEL ** -0.5),
        'w_down': normal((DEPTH, N_EXPERTS, D_EXPERT, D_MODEL), D_EXPERT ** -0.5),
        'ws_gate': normal((DEPTH, D_MODEL, D_SHARED), D_MODEL ** -0.5),
        'ws_up': normal((DEPTH, D_MODEL, D_SHARED), D_MODEL ** -0.5),
        'ws_down': normal((DEPTH, D_SHARED, D_MODEL), D_SHARED ** -0.5),
        'norm_final': gain((D_MODEL,)),
    }


def reference(x_prompt, x_sample, mem_prompt, cache_sb_k, cache_sb_v, page_table, state_ret, cache_mem_k,
              cache_mem_v, norm_mix, w_in, ret_gn, sb_gn, sb_bias, w_out, norm_mem, w_ck, w_cv, norm_cross, w_cq,
              w_co, norm_ffn, router_w, router_bias, w_gate, w_up, w_down, ws_gate, ws_up, ws_down, norm_final):
    log_g = retention_log_decay()
    dec_batch, n_pages = page_table.shape
    past_len = n_pages * PAGE_SIZE
    pos_prompt = jnp.arange(x_prompt.shape[1])
    pos_sample = past_len + jnp.arange(x_sample.shape[1])
    h_p, h_s = x_prompt, x_sample
    ret_p, k_p, v_p, mk_p, mv_p, ret_s, k_s, v_s = [], [], [], [], [], [], [], []
    for l in range(DEPTH):
        lw = (norm_mix[l], w_in[l], ret_gn[l], sb_gn[l], w_out[l], norm_cross[l], w_cq[l], w_co[l],
              norm_ffn[l], router_w[l], router_bias[l], w_gate[l], w_up[l], w_down[l],
              ws_gate[l], ws_up[l], ws_down[l])
        sb_b = sb_bias[l]
        mk, mv = mem_kv(mem_prompt, norm_mem[l], w_ck[l], w_cv[l])

        def prompt_core(rq, rk, rv, sq, sk, sv, sb_b=sb_b):
            o_r, s_r = retention_prompt(rq, rk, rv, log_g)
            return o_r, sb_prompt(sq, sk, sv, sb_b), (s_r, sk, sv)

        h_p, (s_r_p, kn_p, vn_p) = decoder_layer(h_p, pos_prompt, prompt_core, mk, mv, *lw)
        k_past = cache_sb_k[l][page_table].reshape(dec_batch, past_len, N_SB_HEADS, HEAD_DIM).astype(jnp.float32)
        v_past = cache_sb_v[l][page_table].reshape(dec_batch, past_len, N_SB_HEADS, HEAD_DIM).astype(jnp.float32)
        s_prev = state_ret[l].astype(jnp.float32)

        def sample_core(rq, rk, rv, sq, sk, sv, s_prev=s_prev, k_past=k_past, v_past=v_past, sb_b=sb_b):
            o_r, s_r = retention_chunk(rq, rk, rv, s_prev, log_g)
            return o_r, sb_sample(sq, sk, sv, k_past, v_past, sb_b), (s_r, sk, sv)

        h_s, (s_r_s, kn_s, vn_s) = decoder_layer(h_s, pos_sample, sample_core, cache_mem_k[l], cache_mem_v[l], *lw)
        ret_p.append(s_r_p)
        k_p.append(kn_p)
        v_p.append(vn_p)
        mk_p.append(mk)
        mv_p.append(mv)
        ret_s.append(s_r_s)
        k_s.append(kn_s)
        v_s.append(vn_s)
    y_prompt = rmsnorm(h_p, norm_final)
    y_sample = rmsnorm(h_s, norm_final)
    ret_state_prompt = jnp.stack(ret_p)
    sb_k_prompt = jnp.stack(k_p)
    sb_v_prompt = jnp.stack(v_p)
    mem_k_prompt = jnp.stack(mk_p)
    mem_v_prompt = jnp.stack(mv_p)
    ret_state_sample = jnp.stack(ret_s)
    sb_k_sample = jnp.stack(k_s)
    sb_v_sample = jnp.stack(v_s)
    return (y_prompt, y_sample, ret_state_prompt, sb_k_prompt, sb_v_prompt, mem_k_prompt, mem_v_prompt,
            ret_state_sample, sb_k_sample, sb_v_sample)
```

```python
import functools

import jax
import jax.numpy as jnp
from jax import lax
from jax.experimental import pallas as pl
from jax.experimental.pallas import tpu as pltpu

F32 = jnp.float32
BF16 = jnp.bfloat16

HEAD_DIM = 128
N_RET_HEADS = 8
N_SB_HEADS = 8
RET_WIDTH = N_RET_HEADS * HEAD_DIM
SB_WIDTH = N_SB_HEADS * HEAD_DIM
RET_CHUNK = 128
PAGE_SIZE = 128
ROPE_BASE = 10000.0
N_MEM_HEADS = 4
MEM_WIDTH = N_MEM_HEADS * HEAD_DIM
N_EXPERTS = 64
TOP_K = 8
ROUTED_SCALE = 2.5
RMS_EPS = 1e-6
NORM_EPS = 1e-6
QK_SCALE = HEAD_DIM ** -0.5

VMEM_LIMIT_BYTES = 56 * 1024 * 1024
SB_TILE = 256
EXPERT_ROWS = 256

NT_DIMS = (((1,), (1,)), ((), ()))
TN_DIMS = (((0,), (0,)), ((), ()))


def _params(*sem):
    return pltpu.CompilerParams(dimension_semantics=sem, vmem_limit_bytes=VMEM_LIMIT_BYTES)


def _sigmoid(x):
    return 1.0 / (1.0 + jnp.exp(-x))


def _silu(x):
    return x * _sigmoid(x)


def _mm_body(*refs, has_norm, has_res, emit_xn):
    it = iter(refs)
    x_ref = next(it)
    g_ref = next(it) if has_norm else None
    w_ref = next(it)
    r_ref = next(it) if has_res else None
    o_ref = next(it)
    xo_ref = next(it) if emit_xn else None
    xn_ref = next(it)

    @pl.when(pl.program_id(1) == 0)
    def _():
        x = x_ref[...].astype(F32)
        if has_norm:
            x = x * lax.rsqrt(jnp.mean(x * x, axis=-1, keepdims=True) + RMS_EPS)
            x = x * g_ref[...]
        xb = x.astype(BF16)
        xn_ref[...] = xb
        if emit_xn:
            xo_ref[...] = xb

    acc = jnp.dot(xn_ref[...], w_ref[...], preferred_element_type=F32)
    if has_res:
        acc = r_ref[...] + acc
    o_ref[...] = acc


def _mm(x, w, gain=None, res=None, emit_xn=False, tm=512, tn=512):
    m, k = x.shape
    n = w.shape[1]
    tm = min(tm, m)
    tn = min(tn, n)
    assert m % tm == 0 and n % tn == 0
    in_specs = [pl.BlockSpec((tm, k), lambda i, j: (i, 0))]
    args = [x]
    if gain is not None:
        in_specs.append(pl.BlockSpec((1, k), lambda i, j: (0, 0)))
        args.append(gain.reshape(1, k).astype(F32))
    in_specs.append(pl.BlockSpec((k, tn), lambda i, j: (0, j)))
    args.append(w)
    if res is not None:
        in_specs.append(pl.BlockSpec((tm, tn), lambda i, j: (i, j)))
        args.append(res)
    out_shape = [jax.ShapeDtypeStruct((m, n), F32)]
    out_specs = [pl.BlockSpec((tm, tn), lambda i, j: (i, j))]
    if emit_xn:
        out_shape.append(jax.ShapeDtypeStruct((m, k), BF16))
        out_specs.append(pl.BlockSpec((tm, k), lambda i, j: (i, 0)))
    outs = pl.pallas_call(
        functools.partial(_mm_body, has_norm=gain is not None, has_res=res is not None, emit_xn=emit_xn),
        grid=(m // tm, n // tn),
        in_specs=in_specs,
        out_specs=out_specs,
        out_shape=out_shape,
        scratch_shapes=[pltpu.VMEM((tm, k), BF16)],
        compiler_params=_params("parallel", "arbitrary"),
        name="norm_matmul",
    )(*args)
    return outs if emit_xn else outs[0]


def _rotary_tables(pos):
    half = HEAD_DIM // 2
    inv_freq = ROPE_BASE ** (-jnp.arange(half, dtype=F32) / half)
    ang = pos.astype(F32)[:, None] * inv_freq[None, :]
    cos, sin = jnp.cos(ang), jnp.sin(ang)
    return jnp.concatenate([cos, cos], axis=-1), jnp.concatenate([-sin, sin], axis=-1)


def _retention_tables(length):
    log_g = jnp.log1p(-jnp.exp2(-5.0 - jnp.arange(N_RET_HEADS, dtype=F32)))
    i = jnp.arange(length, dtype=F32)
    diff = i[:, None] - i[None, :]
    decay = jnp.where(diff >= 0, jnp.exp(jnp.maximum(diff, 0.0)[None] * log_g[:, None, None]), 0.0)
    dq = jnp.exp((i[None, :] + 1.0) * log_g[:, None])
    dk = jnp.exp((length - 1.0 - i)[None, :] * log_g[:, None])
    ds = jnp.exp(length * log_g)
    lanes = (N_RET_HEADS, length, HEAD_DIM)
    return (decay, jnp.broadcast_to(dq[:, :, None], lanes), jnp.broadcast_to(dk[:, :, None], lanes),
            jnp.broadcast_to(ds[:, None, None], (N_RET_HEADS, 8, HEAD_DIM)))


def _rotate(x, cos, sin_signed):
    return x * cos + pltpu.roll(x, HEAD_DIM // 2, 1) * sin_signed


def _retention_step(q, k, v, s, decay, dq, dk, ds):
    qb, kb, vb = q.astype(BF16), k.astype(BF16), v.astype(BF16)
    scores = lax.dot_general(qb, kb, NT_DIMS, preferred_element_type=F32) * decay
    o = jnp.dot(scores.astype(BF16), vb, preferred_element_type=F32)
    o = o + jnp.dot((q * dq).astype(BF16), s.astype(BF16), preferred_element_type=F32)
    kd = (k * dk).astype(BF16)
    s_new = ds * s + lax.dot_general(kd, vb, TN_DIMS, preferred_element_type=F32)
    return o, s_new


def _gated_layernorm(o, gn, gate):
    c = o - jnp.mean(o, axis=-1, keepdims=True)
    y = c * lax.rsqrt(jnp.mean(c * c, axis=-1, keepdims=True) + NORM_EPS)
    return (y * gn) * _silu(gate)


def _ret_prompt_body(q_ref, k_ref, v_ref, g_ref, cos_ref, sin_ref, decay_ref, dq_ref, dk_ref, ds_ref, gn_ref,
                     o_ref, sfin_ref, s_ref):
    c = pl.program_id(1)

    @pl.when(c == 0)
    def _():
        s_ref[...] = jnp.zeros_like(s_ref)

    cos, sin = cos_ref[...], sin_ref[...]
    q = _rotate(q_ref[...], cos, sin)
    k = _rotate(k_ref[...], cos, sin) * QK_SCALE
    o, s_new = _retention_step(q, k, v_ref[...], s_ref[...], decay_ref[0], dq_ref[0], dk_ref[0], ds_ref[0, 0:1, :])
    s_ref[...] = s_new
    o_ref[...] = _gated_layernorm(o, gn_ref[...], g_ref[...])

    @pl.when(c == pl.num_programs(1) - 1)
    def _():
        sfin_ref[0] = s_new


def _retention_prompt(rq, rk, rv, rg, pos, gn):
    t = rq.shape[0]
    nc = t // RET_CHUNK
    cos, sin = _rotary_tables(pos)
    decay, dq, dk, ds = _retention_tables(RET_CHUNK)
    tok = pl.BlockSpec((RET_CHUNK, HEAD_DIM), lambda h, c: (c, h))
    rot = pl.BlockSpec((RET_CHUNK, HEAD_DIM), lambda h, c: (c, 0))
    per_head = lambda rows: pl.BlockSpec((1, rows, HEAD_DIM), lambda h, c: (h, 0, 0))
    return pl.pallas_call(
        _ret_prompt_body,
        grid=(N_RET_HEADS, nc),
        in_specs=[tok, tok, tok, tok, rot, rot, per_head(RET_CHUNK), per_head(RET_CHUNK), per_head(RET_CHUNK),
                  per_head(8), pl.BlockSpec((1, HEAD_DIM), lambda h, c: (0, h))],
        out_specs=[tok, per_head(HEAD_DIM)],
        out_shape=[jax.ShapeDtypeStruct((t, RET_WIDTH), F32),
                   jax.ShapeDtypeStruct((N_RET_HEADS, HEAD_DIM, HEAD_DIM), F32)],
        scratch_shapes=[pltpu.VMEM((HEAD_DIM, HEAD_DIM), F32)],
        compiler_params=_params("parallel", "arbitrary"),
        name="retention_prompt",
    )(rq, rk, rv, rg, cos, sin, decay, dq, dk, ds, gn.reshape(1, RET_WIDTH))


def _ret_sample_body(q_ref, k_ref, v_ref, g_ref, st_ref, cos_ref, sin_ref, decay_ref, dq_ref, dk_ref, ds_ref, gn_ref,
                     o_ref, snew_ref):
    cos, sin = cos_ref[...], sin_ref[...]
    for h in range(N_RET_HEADS):
        cols = slice(h * HEAD_DIM, (h + 1) * HEAD_DIM)
        q = _rotate(q_ref[0, :, cols], cos, sin)
        k = _rotate(k_ref[0, :, cols], cos, sin) * QK_SCALE
        o, s_new = _retention_step(q, k, v_ref[0, :, cols], st_ref[0, h], decay_ref[h], dq_ref[h], dk_ref[h],
                                   ds_ref[h, 0:1, :])
        snew_ref[0, h] = s_new
        o_ref[0, :, cols] = _gated_layernorm(o, gn_ref[:, cols], g_ref[0, :, cols])


def _retention_sample(rq, rk, rv, rg, state, pos, gn):
    b, l, _ = rq.shape
    cos, sin = _rotary_tables(pos)
    decay, dq, dk, ds = _retention_tables(l)
    tok = pl.BlockSpec((1, l, RET_WIDTH), lambda i: (i, 0, 0))
    st = pl.BlockSpec((1, N_RET_HEADS, HEAD_DIM, HEAD_DIM), lambda i: (i, 0, 0, 0))
    full = lambda a: pl.BlockSpec(a.shape, lambda i: (0,) * a.ndim)
    gn2 = gn.reshape(1, RET_WIDTH)
    return pl.pallas_call(
        _ret_sample_body,
        grid=(b,),
        in_specs=[tok, tok, tok, tok, st, full(cos), full(sin), full(decay), full(dq), full(dk), full(ds), full(gn2)],
        out_specs=[tok, st],
        out_shape=[jax.ShapeDtypeStruct((b, l, RET_WIDTH), F32), jax.ShapeDtypeStruct(state.shape, F32)],
        compiler_params=_params("parallel"),
        name="retention_sample",
    )(rq, rk, rv, rg, state, cos, sin, decay, dq, dk, ds, gn2)


def _strict_lower(n):
    j = lax.broadcasted_iota(jnp.int32, (n, n), 0)
    s = lax.broadcasted_iota(jnp.int32, (n, n), 1)
    return jnp.where(j > s, 1.0, 0.0).astype(BF16)


def _sb_block(z, v_bf16, tri, carry, mask):
    soft = jnp.log1p(jnp.exp(-jnp.abs(z)))
    log_beta = jnp.minimum(z, 0.0) - soft
    log_keep = jnp.minimum(-z, 0.0) - soft
    if mask is not None:
        log_keep = jnp.where(mask, log_keep, 0.0)
    hi = log_keep.astype(BF16)
    lo = (log_keep - hi.astype(F32)).astype(BF16)
    between = (jnp.dot(hi, tri, preferred_element_type=F32) + jnp.dot(lo, tri, preferred_element_type=F32))
    a = jnp.exp(log_beta + between + carry)
    if mask is not None:
        a = jnp.where(mask, a, 0.0)
    out = jnp.dot(a.astype(BF16), v_bf16, preferred_element_type=F32)
    return out, carry + jnp.sum(log_keep, axis=-1, keepdims=True)


def _head_rmsnorm(o, gn):
    return (o * lax.rsqrt(jnp.mean(o * o, axis=-1, keepdims=True) + RMS_EPS)) * gn


def _sb_prompt_body(bias_ref, q_ref, k_ref, v_ref, gn_ref, o_ref, kb_ref, vb_ref, tri_ref):
    h = pl.program_id(0)
    i = pl.program_id(1)

    @pl.when(i == 0)
    def _():
        kb_ref[...] = k_ref[...].astype(BF16)
        vb_ref[...] = v_ref[...].astype(BF16)
        tri_ref[...] = _strict_lower(SB_TILE)

    q = q_ref[...].astype(BF16)
    bias = bias_ref[h]
    tri = tri_ref[...]

    def block(j, carry, mask):
        rows = pl.ds(pl.multiple_of(j * SB_TILE, SB_TILE), SB_TILE)
        z = lax.dot_general(q, kb_ref[rows, :], NT_DIMS, preferred_element_type=F32) * QK_SCALE + bias
        return _sb_block(z, vb_ref[rows, :], tri, carry, mask)

    r = lax.broadcasted_iota(jnp.int32, (SB_TILE, SB_TILE), 0)
    c = lax.broadcasted_iota(jnp.int32, (SB_TILE, SB_TILE), 1)
    acc, carry = block(i, jnp.zeros((SB_TILE, 1), F32), c < r)

    def body(n, state):
        acc, carry = state
        out, carry = block(i - 1 - n, carry, None)
        return acc + out, carry

    acc, _ = lax.fori_loop(0, i, body, (acc, carry))
    o_ref[...] = _head_rmsnorm(acc, gn_ref[...])


def _sb_prompt(sq, sk, sv, bias, gn):
    t = sq.shape[0]
    assert t % SB_TILE == 0
    qspec = pl.BlockSpec((SB_TILE, HEAD_DIM), lambda h, i, b: (i, h))
    kvspec = pl.BlockSpec((t, HEAD_DIM), lambda h, i, b: (0, h))
    return pl.pallas_call(
        _sb_prompt_body,
        grid_spec=pltpu.PrefetchScalarGridSpec(
            num_scalar_prefetch=1,
            grid=(N_SB_HEADS, t // SB_TILE),
            in_specs=[qspec, kvspec, kvspec, pl.BlockSpec((1, HEAD_DIM), lambda h, i, b: (0, h))],
            out_specs=qspec,
            scratch_shapes=[pltpu.VMEM((t, HEAD_DIM), BF16), pltpu.VMEM((t, HEAD_DIM), BF16),
                            pltpu.VMEM((SB_TILE, SB_TILE), BF16)],
        ),
        out_shape=jax.ShapeDtypeStruct((t, SB_WIDTH), F32),
        compiler_params=_params("parallel", "arbitrary"),
        name="stick_breaking_prompt",
    )(bias.astype(F32), sq, sk, sv, gn.reshape(1, SB_WIDTH))


def _sb_sample_body(pt_ref, bias_ref, q_ref, kn_ref, vn_ref, kp_ref, vp_ref, gn_ref, o_ref, acc_ref, carry_ref):
    p = pl.program_id(1)
    l = q_ref.shape[1]
    rows = N_SB_HEADS * l
    tri = _strict_lower(PAGE_SIZE)
    bias = jnp.concatenate([jnp.full((l, 1), bias_ref[h], F32) for h in range(N_SB_HEADS)], axis=0)
    qs = [q_ref[0, :, h * HEAD_DIM:(h + 1) * HEAD_DIM].astype(BF16) for h in range(N_SB_HEADS)]

    def logits(keys):
        z = [lax.dot_general(qs[h], keys[h], NT_DIMS, preferred_element_type=F32) for h in range(N_SB_HEADS)]
        return jnp.concatenate(z, axis=0) * QK_SCALE + bias

    def attend(z, values, carry, mask):
        soft = jnp.log1p(jnp.exp(-jnp.abs(z)))
        log_beta = jnp.minimum(z, 0.0) - soft
        log_keep = jnp.minimum(-z, 0.0) - soft
        if mask is not None:
            log_keep = jnp.where(mask, log_keep, 0.0)
        hi = log_keep.astype(BF16)
        lo = (log_keep - hi.astype(F32)).astype(BF16)
        between = jnp.dot(hi, tri, preferred_element_type=F32) + jnp.dot(lo, tri, preferred_element_type=F32)
        a = jnp.exp(log_beta + between + carry)
        if mask is not None:
            a = jnp.where(mask, a, 0.0)
        out = [jnp.dot(a[h * l:(h + 1) * l, :].astype(BF16), values[h], preferred_element_type=F32)
               for h in range(N_SB_HEADS)]
        return jnp.concatenate(out, axis=0), carry + jnp.sum(log_keep, axis=-1, keepdims=True)

    @pl.when(p == 0)
    def _():
        pad = jnp.zeros((PAGE_SIZE - l, HEAD_DIM), F32)
        kn = [jnp.concatenate([kn_ref[0, :, h * HEAD_DIM:(h + 1) * HEAD_DIM], pad], axis=0).astype(BF16)
              for h in range(N_SB_HEADS)]
        vn = [jnp.concatenate([vn_ref[0, :, h * HEAD_DIM:(h + 1) * HEAD_DIM], pad], axis=0).astype(BF16)
              for h in range(N_SB_HEADS)]
        qi = lax.broadcasted_iota(jnp.int32, (rows, PAGE_SIZE), 0) % l
        kj = lax.broadcasted_iota(jnp.int32, (rows, PAGE_SIZE), 1)
        out, carry = attend(logits(kn), vn, jnp.zeros((rows, 1), F32), kj < qi)
        acc_ref[...] = out
        carry_ref[...] = carry

    kp = [kp_ref[0, :, h, :].astype(BF16) for h in range(N_SB_HEADS)]
    vp = [vp_ref[0, :, h, :].astype(BF16) for h in range(N_SB_HEADS)]
    out, carry = attend(logits(kp), vp, carry_ref[...], None)
    acc_ref[...] += out
    carry_ref[...] = carry

    @pl.when(p == pl.num_programs(1) - 1)
    def _():
        acc = acc_ref[...]
        for h in range(N_SB_HEADS):
            cols = slice(h * HEAD_DIM, (h + 1) * HEAD_DIM)
            o_ref[0, :, cols] = _head_rmsnorm(acc[h * l:(h + 1) * l, :], gn_ref[:, cols])


def _sb_sample(sq, sk, sv, cache_k, cache_v, page_table, bias, gn):
    b, l, _ = sq.shape
    n_pages = page_table.shape[1]
    tok = pl.BlockSpec((1, l, SB_WIDTH), lambda i, p, pt, bs: (i, 0, 0))
    page = pl.BlockSpec((1, PAGE_SIZE, N_SB_HEADS, HEAD_DIM),
                        lambda i, p, pt, bs: (pt[i, n_pages - 1 - p], 0, 0, 0))
    return pl.pallas_call(
        _sb_sample_body,
        grid_spec=pltpu.PrefetchScalarGridSpec(
            num_scalar_prefetch=2,
            grid=(b, n_pages),
            in_specs=[tok, tok, tok, page, page, pl.BlockSpec((1, SB_WIDTH), lambda i, p, pt, bs: (0, 0))],
            out_specs=tok,
            scratch_shapes=[pltpu.VMEM((N_SB_HEADS * l, HEAD_DIM), F32), pltpu.VMEM((N_SB_HEADS * l, 1), F32)],
        ),
        out_shape=jax.ShapeDtypeStruct((b, l, SB_WIDTH), F32),
        compiler_params=_params("parallel", "arbitrary"),
        name="stick_breaking_sample",
    )(page_table, bias.astype(F32), sq, sk, sv, cache_k, cache_v, gn.reshape(1, SB_WIDTH))


def _cross_body(q_ref, k_ref, v_ref, o_ref):
    for h in range(N_MEM_HEADS):
        cols = slice(h * HEAD_DIM, (h + 1) * HEAD_DIM)
        q = q_ref[0, :, cols].astype(BF16)
        k = k_ref[0, :, cols].astype(BF16)
        v = v_ref[0, :, cols].astype(BF16)
        s = lax.dot_general(q, k, NT_DIMS, preferred_element_type=F32) * QK_SCALE
        e = jnp.exp(s - jnp.max(s, axis=-1, keepdims=True))
        p = e / jnp.sum(e, axis=-1, keepdims=True)
        o_ref[0, :, cols] = jnp.dot(p.astype(BF16), v, preferred_element_type=F32)


def _cross_attention(q, mem_k, mem_v, tq):
    b, t, _ = q.shape
    m = mem_k.shape[1]
    tq = min(tq, t)
    qspec = pl.BlockSpec((1, tq, MEM_WIDTH), lambda i, j: (i, j, 0))
    mspec = pl.BlockSpec((1, m, MEM_WIDTH), lambda i, j: (i, 0, 0))
    return pl.pallas_call(
        _cross_body,
        grid=(b, t // tq),
        in_specs=[qspec, mspec, mspec],
        out_specs=qspec,
        out_shape=jax.ShapeDtypeStruct(q.shape, F32),
        compiler_params=_params("parallel", "arbitrary"),
        name="memory_cross_attention",
    )(q, mem_k, mem_v)


def _expert_body(be_ref, x_ref, wg_ref, wu_ref, wd_ref, o_ref, wg_b, wu_b, wd_b):
    b = pl.program_id(0)
    prev = be_ref[jnp.maximum(b - 1, 0)]

    @pl.when(jnp.logical_or(b == 0, be_ref[b] != prev))
    def _():
        wg_b[...] = wg_ref[0].astype(BF16)
        wu_b[...] = wu_ref[0].astype(BF16)
        wd_b[...] = wd_ref[0].astype(BF16)

    x = x_ref[...]
    g = jnp.dot(x, wg_b[...], preferred_element_type=F32)
    u = jnp.dot(x, wu_b[...], preferred_element_type=F32)
    a = (_silu(g) * u).astype(BF16)
    o_ref[...] = jnp.dot(a, wd_b[...], preferred_element_type=F32)


def _expert_mlp(x, block_expert, w_gate, w_up, w_down, rows):
    r, d = x.shape
    f = w_gate.shape[2]
    wspec = lambda shape: pl.BlockSpec((1,) + shape, lambda i, be: (be[i], 0, 0))
    return pl.pallas_call(
        _expert_body,
        grid_spec=pltpu.PrefetchScalarGridSpec(
            num_scalar_prefetch=1,
            grid=(r // rows,),
            in_specs=[pl.BlockSpec((rows, d), lambda i, be: (i, 0)), wspec((d, f)), wspec((d, f)), wspec((f, d))],
            out_specs=pl.BlockSpec((rows, d), lambda i, be: (i, 0)),
            scratch_shapes=[pltpu.VMEM((d, f), BF16), pltpu.VMEM((d, f), BF16), pltpu.VMEM((f, d), BF16)],
        ),
        out_shape=jax.ShapeDtypeStruct((r, d), F32),
        compiler_params=_params("arbitrary"),
        name="expert_mlp",
    )(block_expert, x, w_gate, w_up, w_down)


def _final_body(x_ref, r_ref, s_ref, g_ref, o_ref):
    x = x_ref[...] + (r_ref[...] + s_ref[...])
    o_ref[...] = (x * lax.rsqrt(jnp.mean(x * x, axis=-1, keepdims=True) + RMS_EPS)) * g_ref[...]


def _final_norm(x, routed, shared, gain, tm=256):
    m, d = x.shape
    row = pl.BlockSpec((tm, d), lambda i: (i, 0))
    return pl.pallas_call(
        _final_body,
        grid=(m // tm,),
        in_specs=[row, row, row, pl.BlockSpec((1, d), lambda i: (0, 0))],
        out_specs=row,
        out_shape=jax.ShapeDtypeStruct((m, d), F32),
        compiler_params=_params("parallel"),
        name="residual_final_norm",
    )(x, routed, shared, gain.reshape(1, d))


def _moe(hn, logits, router_bias, w_gate, w_up, w_down, ws_gate, ws_up, ws_down):
    t, d = hn.shape
    scores = jax.nn.sigmoid(logits)
    _, idx = lax.top_k(scores + router_bias.astype(F32), TOP_K)
    sel = jnp.take_along_axis(scores, idx, axis=-1)
    gates = sel / jnp.sum(sel, axis=-1, keepdims=True) * ROUTED_SCALE

    chosen = jnp.zeros((t, N_EXPERTS), jnp.int32).at[jnp.arange(t)[:, None], idx].set(1)
    rank = jnp.cumsum(chosen, axis=0) - chosen
    counts = jnp.sum(chosen, axis=0)
    padded = (counts + EXPERT_ROWS - 1) // EXPERT_ROWS * EXPERT_ROWS
    pad_end = jnp.cumsum(padded)
    pad_start = pad_end - padded
    slot = pad_start[idx] + jnp.take_along_axis(rank, idx, axis=-1)
    n_blocks = -(-(t * TOP_K) // EXPERT_ROWS) + N_EXPERTS
    slot_tok = jnp.full((n_blocks * EXPERT_ROWS,), t, jnp.int32).at[slot.reshape(-1)].set(
        jnp.repeat(jnp.arange(t, dtype=jnp.int32), TOP_K))
    block_e = jnp.minimum(jnp.searchsorted(pad_end, jnp.arange(n_blocks) * EXPERT_ROWS, side='right'),
                          N_EXPERTS - 1).astype(jnp.int32)
    h_pad = jnp.concatenate([hn, jnp.zeros((1, d), hn.dtype)], axis=0)
    yb = _expert_mlp(h_pad[slot_tok], block_e, w_gate, w_up, w_down, EXPERT_ROWS)
    routed = jnp.sum(yb[slot] * gates[:, :, None], axis=1)

    shared = _expert_mlp(hn, jnp.zeros((t // EXPERT_ROWS,), jnp.int32), ws_gate[None], ws_up[None], ws_down[None],
                         EXPERT_ROWS)
    return routed, shared


def _split_proj(proj):
    cuts = [RET_WIDTH * n for n in range(1, 5)] + [4 * RET_WIDTH + SB_WIDTH, 4 * RET_WIDTH + 2 * SB_WIDTH]
    return jnp.split(proj, cuts, axis=-1)


def kernel(x_prompt, x_sample, mem_prompt, cache_sb_k, cache_sb_v, page_table, state_ret, cache_mem_k, cache_mem_v,
           norm_mix, w_in, ret_gn, sb_gn, sb_bias, w_out, norm_mem, w_ck, w_cv, norm_cross, w_cq, w_co, norm_ffn,
           router_w, router_bias, w_gate, w_up, w_down, ws_gate, ws_up, ws_down, norm_final):
    depth = w_in.shape[0]
    assert depth == 1 and x_prompt.shape[0] == 1
    _, t, d = x_prompt.shape
    b, l, _ = x_sample.shape
    n_pages = page_table.shape[1]
    lyr = 0
    xp = x_prompt.reshape(t, d)
    xs = x_sample.reshape(b * l, d)
    pos_p = jnp.arange(t)
    pos_s = n_pages * PAGE_SIZE + jnp.arange(l)

    w_in_b = w_in[lyr].astype(BF16)
    w_out_b = w_out[lyr].astype(BF16)
    w_cq_b = w_cq[lyr].astype(BF16)
    w_co_b = w_co[lyr].astype(BF16)

    mem = mem_prompt.reshape(-1, d)
    n_mem = mem.shape[0]
    mkv = _mm(mem, jnp.concatenate([w_ck[lyr], w_cv[lyr]], axis=1).astype(BF16), gain=norm_mem[lyr])
    mk_p, mv_p = mkv[:, :MEM_WIDTH], mkv[:, MEM_WIDTH:]

    rq, rk, rv, rg, sq, sk_p, sv_p = _split_proj(_mm(xp, w_in_b, gain=norm_mix[lyr]))
    o_ret, s_ret_p = _retention_prompt(rq, rk, rv, rg, pos_p, ret_gn[lyr])
    o_sb = _sb_prompt(sq, sk_p, sv_p, sb_bias[lyr], sb_gn[lyr])
    xp = _mm(jnp.concatenate([o_ret, o_sb], axis=-1), w_out_b, res=xp)

    rq, rk, rv, rg, sq, sk_s, sv_s = [a.reshape(b, l, -1) for a in _split_proj(_mm(xs, w_in_b, gain=norm_mix[lyr]))]
    o_ret, s_ret_s = _retention_sample(rq, rk, rv, rg, state_ret[lyr], pos_s, ret_gn[lyr])
    o_sb = _sb_sample(sq, sk_s, sv_s, cache_sb_k[lyr], cache_sb_v[lyr], page_table, sb_bias[lyr], sb_gn[lyr])
    xs = _mm(jnp.concatenate([o_ret, o_sb], axis=-1).reshape(b * l, -1), w_out_b, res=xs)

    q = _mm(xp, w_cq_b, gain=norm_cross[lyr])
    o = _cross_attention(q[None], mk_p[None], mv_p[None], tq=512)
    xp = _mm(o[0], w_co_b, res=xp)
    q = _mm(xs, w_cq_b, gain=norm_cross[lyr]).reshape(b, l, MEM_WIDTH)
    o = _cross_attention(q, cache_mem_k[lyr].reshape(b, n_mem, MEM_WIDTH),
                         cache_mem_v[lyr].reshape(b, n_mem, MEM_WIDTH), tq=l)
    xs = _mm(o.reshape(b * l, MEM_WIDTH), w_co_b, res=xs)

    x_all = jnp.concatenate([xp, xs], axis=0)
    logits, hn = _mm(x_all, router_w[lyr].astype(BF16), gain=norm_ffn[lyr], emit_xn=True)
    routed, shared = _moe(hn, logits, router_bias[lyr], w_gate[lyr], w_up[lyr], w_down[lyr],
                          ws_gate[lyr], ws_up[lyr], ws_down[lyr])
    y_all = _final_norm(x_all, routed, shared, norm_final)

    y_prompt = y_all[:t].reshape(1, t, d)
    y_sample = y_all[t:].reshape(b, l, d)
    return (y_prompt, y_sample,
            s_ret_p[None, None],
            sk_p.reshape(1, 1, t, N_SB_HEADS, HEAD_DIM), sv_p.reshape(1, 1, t, N_SB_HEADS, HEAD_DIM),
            mk_p.reshape(1, 1, n_mem, N_MEM_HEADS, HEAD_DIM), mv_p.reshape(1, 1, n_mem, N_MEM_HEADS, HEAD_DIM),
            s_ret_s[None],
            sk_s.reshape(1, b, l, N_SB_HEADS, HEAD_DIM), sv_s.reshape(1, b, l, N_SB_HEADS, HEAD_DIM))
```

```python
import functools

import jax
import jax.numpy as jnp
from jax import lax
from jax.experimental import pallas as pl
from jax.experimental.pallas import tpu as pltpu

F32 = jnp.float32
BF16 = jnp.bfloat16

HEAD_DIM = 128
N_RET_HEADS = 8
N_SB_HEADS = 8
RET_WIDTH = N_RET_HEADS * HEAD_DIM
SB_WIDTH = N_SB_HEADS * HEAD_DIM
RET_CHUNK = 128
PAGE_SIZE = 128
ROPE_BASE = 10000.0
N_MEM_HEADS = 4
MEM_WIDTH = N_MEM_HEADS * HEAD_DIM
N_EXPERTS = 64
TOP_K = 8
ROUTED_SCALE = 2.5
RMS_EPS = 1e-6
NORM_EPS = 1e-6
QK_SCALE = HEAD_DIM ** -0.5

VMEM_LIMIT_BYTES = 56 * 1024 * 1024
SB_KEYS = 256
SB_QUERIES = 512
SB_PAGES_PER_STEP = 4
EXPERT_ROWS = 256
COMBINE_ROWS = 128
LOG2E = 1.4426950408889634

NT_DIMS = (((1,), (1,)), ((), ()))
TN_DIMS = (((0,), (0,)), ((), ()))


def _params(*sem):
    return pltpu.CompilerParams(dimension_semantics=sem, vmem_limit_bytes=VMEM_LIMIT_BYTES)


def _sigmoid(x):
    return 1.0 / (1.0 + jnp.exp(-x))


def _silu(x):
    return x * _sigmoid(x)


def _mm_body(*refs, has_norm, has_res, emit_xn):
    it = iter(refs)
    x_ref = next(it)
    g_ref = next(it) if has_norm else None
    w_ref = next(it)
    r_ref = next(it) if has_res else None
    o_ref = next(it)
    xo_ref = next(it) if emit_xn else None
    xn_ref = next(it)

    @pl.when(pl.program_id(1) == 0)
    def _():
        x = x_ref[...].astype(F32)
        if has_norm:
            x = x * lax.rsqrt(jnp.mean(x * x, axis=-1, keepdims=True) + RMS_EPS)
            x = x * g_ref[...]
        xb = x.astype(BF16)
        xn_ref[...] = xb
        if emit_xn:
            xo_ref[...] = xb

    acc = jnp.dot(xn_ref[...], w_ref[...], preferred_element_type=F32)
    if has_res:
        acc = r_ref[...] + acc
    o_ref[...] = acc


def _mm(x, w, gain=None, res=None, emit_xn=False, tm=512, tn=512):
    m, k = x.shape
    n = w.shape[1]
    tm = min(tm, m)
    tn = min(tn, n)
    assert m % tm == 0 and n % tn == 0
    in_specs = [pl.BlockSpec((tm, k), lambda i, j: (i, 0))]
    args = [x]
    if gain is not None:
        in_specs.append(pl.BlockSpec((1, k), lambda i, j: (0, 0)))
        args.append(gain.reshape(1, k).astype(F32))
    in_specs.append(pl.BlockSpec((k, tn), lambda i, j: (0, j)))
    args.append(w)
    if res is not None:
        in_specs.append(pl.BlockSpec((tm, tn), lambda i, j: (i, j)))
        args.append(res)
    out_shape = [jax.ShapeDtypeStruct((m, n), F32)]
    out_specs = [pl.BlockSpec((tm, tn), lambda i, j: (i, j))]
    if emit_xn:
        out_shape.append(jax.ShapeDtypeStruct((m, k), BF16))
        out_specs.append(pl.BlockSpec((tm, k), lambda i, j: (i, 0)))
    outs = pl.pallas_call(
        functools.partial(_mm_body, has_norm=gain is not None, has_res=res is not None, emit_xn=emit_xn),
        grid=(m // tm, n // tn),
        in_specs=in_specs,
        out_specs=out_specs,
        out_shape=out_shape,
        scratch_shapes=[pltpu.VMEM((tm, k), BF16)],
        compiler_params=_params("parallel", "arbitrary"),
        name="norm_matmul",
    )(*args)
    return outs if emit_xn else outs[0]


def _rotary_tables(pos):
    half = HEAD_DIM // 2
    inv_freq = ROPE_BASE ** (-jnp.arange(half, dtype=F32) / half)
    ang = pos.astype(F32)[:, None] * inv_freq[None, :]
    cos, sin = jnp.cos(ang), jnp.sin(ang)
    return jnp.concatenate([cos, cos], axis=-1), jnp.concatenate([-sin, sin], axis=-1)


def _retention_tables(length):
    log_g = jnp.log1p(-jnp.exp2(-5.0 - jnp.arange(N_RET_HEADS, dtype=F32)))
    i = jnp.arange(length, dtype=F32)
    diff = i[:, None] - i[None, :]
    decay = jnp.where(diff >= 0, jnp.exp(jnp.maximum(diff, 0.0)[None] * log_g[:, None, None]), 0.0)
    dq = jnp.exp((i[None, :] + 1.0) * log_g[:, None])
    dk = jnp.exp((length - 1.0 - i)[None, :] * log_g[:, None])
    ds = jnp.exp(length * log_g)
    lanes = (N_RET_HEADS, length, HEAD_DIM)
    return (decay, jnp.broadcast_to(dq[:, :, None], lanes), jnp.broadcast_to(dk[:, :, None], lanes),
            jnp.broadcast_to(ds[:, None, None], (N_RET_HEADS, 8, HEAD_DIM)))


def _rotate(x, cos, sin_signed):
    return x * cos + pltpu.roll(x, HEAD_DIM // 2, 1) * sin_signed


def _retention_step(q, k, v, s, decay, dq, dk, ds):
    qb, kb, vb = q.astype(BF16), k.astype(BF16), v.astype(BF16)
    scores = lax.dot_general(qb, kb, NT_DIMS, preferred_element_type=F32) * decay
    o = jnp.dot(scores.astype(BF16), vb, preferred_element_type=F32)
    o = o + jnp.dot((q * dq).astype(BF16), s.astype(BF16), preferred_element_type=F32)
    kd = (k * dk).astype(BF16)
    s_new = ds * s + lax.dot_general(kd, vb, TN_DIMS, preferred_element_type=F32)
    return o, s_new


def _gated_layernorm(o, gn, gate):
    c = o - jnp.mean(o, axis=-1, keepdims=True)
    y = c * lax.rsqrt(jnp.mean(c * c, axis=-1, keepdims=True) + NORM_EPS)
    return (y * gn) * _silu(gate)


def _ret_prompt_body(q_ref, k_ref, v_ref, g_ref, cos_ref, sin_ref, decay_ref, dq_ref, dk_ref, ds_ref, gn_ref,
                     o_ref, sfin_ref, s_ref):
    c = pl.program_id(1)

    @pl.when(c == 0)
    def _():
        s_ref[...] = jnp.zeros_like(s_ref)

    cos, sin = cos_ref[...], sin_ref[...]
    q = _rotate(q_ref[...], cos, sin)
    k = _rotate(k_ref[...], cos, sin) * QK_SCALE
    o, s_new = _retention_step(q, k, v_ref[...], s_ref[...], decay_ref[0], dq_ref[0], dk_ref[0], ds_ref[0, 0:1, :])
    s_ref[...] = s_new
    o_ref[...] = _gated_layernorm(o, gn_ref[...], g_ref[...])

    @pl.when(c == pl.num_programs(1) - 1)
    def _():
        sfin_ref[0] = s_new


def _retention_prompt(rq, rk, rv, rg, pos, gn):
    t = rq.shape[0]
    nc = t // RET_CHUNK
    cos, sin = _rotary_tables(pos)
    decay, dq, dk, ds = _retention_tables(RET_CHUNK)
    tok = pl.BlockSpec((RET_CHUNK, HEAD_DIM), lambda h, c: (c, h))
    rot = pl.BlockSpec((RET_CHUNK, HEAD_DIM), lambda h, c: (c, 0))
    per_head = lambda rows: pl.BlockSpec((1, rows, HEAD_DIM), lambda h, c: (h, 0, 0))
    return pl.pallas_call(
        _ret_prompt_body,
        grid=(N_RET_HEADS, nc),
        in_specs=[tok, tok, tok, tok, rot, rot, per_head(RET_CHUNK), per_head(RET_CHUNK), per_head(RET_CHUNK),
                  per_head(8), pl.BlockSpec((1, HEAD_DIM), lambda h, c: (0, h))],
        out_specs=[tok, per_head(HEAD_DIM)],
        out_shape=[jax.ShapeDtypeStruct((t, RET_WIDTH), F32),
                   jax.ShapeDtypeStruct((N_RET_HEADS, HEAD_DIM, HEAD_DIM), F32)],
        scratch_shapes=[pltpu.VMEM((HEAD_DIM, HEAD_DIM), F32)],
        compiler_params=_params("parallel", "arbitrary"),
        name="retention_prompt",
    )(rq, rk, rv, rg, cos, sin, decay, dq, dk, ds, gn.reshape(1, RET_WIDTH))


def _ret_sample_body(q_ref, k_ref, v_ref, g_ref, st_ref, cos_ref, sin_ref, decay_ref, dq_ref, dk_ref, ds_ref, gn_ref,
                     o_ref, snew_ref):
    cos, sin = cos_ref[...], sin_ref[...]
    for h in range(N_RET_HEADS):
        cols = slice(h * HEAD_DIM, (h + 1) * HEAD_DIM)
        q = _rotate(q_ref[0, :, cols], cos, sin)
        k = _rotate(k_ref[0, :, cols], cos, sin) * QK_SCALE
        o, s_new = _retention_step(q, k, v_ref[0, :, cols], st_ref[0, h], decay_ref[h], dq_ref[h], dk_ref[h],
                                   ds_ref[h, 0:1, :])
        snew_ref[0, h] = s_new
        o_ref[0, :, cols] = _gated_layernorm(o, gn_ref[:, cols], g_ref[0, :, cols])


def _retention_sample(rq, rk, rv, rg, state, pos, gn):
    b, l, _ = rq.shape
    cos, sin = _rotary_tables(pos)
    decay, dq, dk, ds = _retention_tables(l)
    tok = pl.BlockSpec((1, l, RET_WIDTH), lambda i: (i, 0, 0))
    st = pl.BlockSpec((1, N_RET_HEADS, HEAD_DIM, HEAD_DIM), lambda i: (i, 0, 0, 0))
    full = lambda a: pl.BlockSpec(a.shape, lambda i: (0,) * a.ndim)
    gn2 = gn.reshape(1, RET_WIDTH)
    return pl.pallas_call(
        _ret_sample_body,
        grid=(b,),
        in_specs=[tok, tok, tok, tok, st, full(cos), full(sin), full(decay), full(dq), full(dk), full(ds), full(gn2)],
        out_specs=[tok, st],
        out_shape=[jax.ShapeDtypeStruct((b, l, RET_WIDTH), F32), jax.ShapeDtypeStruct(state.shape, F32)],
        compiler_params=_params("parallel"),
        name="retention_sample",
    )(rq, rk, rv, rg, state, cos, sin, decay, dq, dk, ds, gn2)


def _strict_lower(n):
    j = lax.broadcasted_iota(jnp.int32, (n, n), 0)
    s = lax.broadcasted_iota(jnp.int32, (n, n), 1)
    return jnp.where(j > s, 1.0, 0.0).astype(BF16)


def _sb_weights(z, tri, drop, mask):
    neg_log_keep = jnp.maximum(z, 0.0) + jnp.log(1.0 + jnp.exp2(jnp.abs(z) * -LOG2E))
    log_beta = z - neg_log_keep
    if mask is not None:
        neg_log_keep = jnp.where(mask, neg_log_keep, 0.0)
    hi = neg_log_keep.astype(BF16)
    lo = (neg_log_keep - hi.astype(F32)).astype(BF16)
    between = jnp.dot(hi, tri, preferred_element_type=F32) + jnp.dot(lo, tri, preferred_element_type=F32)
    a = jnp.exp(log_beta - between - drop)
    if mask is not None:
        a = jnp.where(mask, a, 0.0)
    return a, drop + jnp.sum(neg_log_keep, axis=-1, keepdims=True)


def _head_rmsnorm(o, gn):
    return (o * lax.rsqrt(jnp.mean(o * o, axis=-1, keepdims=True) + RMS_EPS)) * gn


def _sb_prompt_body(bias_ref, q_ref, k_ref, v_ref, gn_ref, o_ref, kb_ref, vb_ref, tri_ref):
    h = pl.program_id(0)
    i = pl.program_id(1)

    @pl.when(i == 0)
    def _():
        kb_ref[...] = k_ref[...].astype(BF16)
        vb_ref[...] = v_ref[...].astype(BF16)
        tri_ref[...] = _strict_lower(SB_KEYS)

    q = (q_ref[...] * QK_SCALE).astype(BF16)
    bias = bias_ref[h]
    tri = tri_ref[...]

    def block(j, drop, mask):
        rows = pl.ds(pl.multiple_of(j * SB_KEYS, SB_KEYS), SB_KEYS)
        z = lax.dot_general(q, kb_ref[rows, :], NT_DIMS, preferred_element_type=F32) + bias
        a, drop = _sb_weights(z, tri, drop, mask)
        return jnp.dot(a.astype(BF16), vb_ref[rows, :], preferred_element_type=F32), drop

    ratio = SB_QUERIES // SB_KEYS
    r = lax.broadcasted_iota(jnp.int32, (SB_QUERIES, SB_KEYS), 0)
    c = lax.broadcasted_iota(jnp.int32, (SB_QUERIES, SB_KEYS), 1)
    acc = jnp.zeros((SB_QUERIES, HEAD_DIM), F32)
    drop = jnp.zeros((SB_QUERIES, 1), F32)
    for d in reversed(range(ratio)):
        out, drop = block(i * ratio + d, drop, c + d * SB_KEYS < r)
        acc = acc + out

    def body(n, state):
        acc, drop = state
        out, drop = block(i * ratio - 1 - n, drop, None)
        return acc + out, drop

    acc, _ = lax.fori_loop(0, i * ratio, body, (acc, drop))
    o_ref[...] = _head_rmsnorm(acc, gn_ref[...])


def _sb_prompt(sq, sk, sv, bias, gn):
    t = sq.shape[0]
    assert t % SB_QUERIES == 0 and SB_QUERIES % SB_KEYS == 0
    qspec = pl.BlockSpec((SB_QUERIES, HEAD_DIM), lambda h, i, b: (i, h))
    kvspec = pl.BlockSpec((t, HEAD_DIM), lambda h, i, b: (0, h))
    return pl.pallas_call(
        _sb_prompt_body,
        grid_spec=pltpu.PrefetchScalarGridSpec(
            num_scalar_prefetch=1,
            grid=(N_SB_HEADS, t // SB_QUERIES),
            in_specs=[qspec, kvspec, kvspec, pl.BlockSpec((1, HEAD_DIM), lambda h, i, b: (0, h))],
            out_specs=qspec,
            scratch_shapes=[pltpu.VMEM((t, HEAD_DIM), BF16), pltpu.VMEM((t, HEAD_DIM), BF16),
                            pltpu.VMEM((SB_KEYS, SB_KEYS), BF16)],
        ),
        out_shape=jax.ShapeDtypeStruct((t, SB_WIDTH), F32),
        compiler_params=_params("parallel", "arbitrary"),
        name="stick_breaking_prompt",
    )(bias.astype(F32), sq, sk, sv, gn.reshape(1, SB_WIDTH))


def _sb_sample_body(pt_ref, bias_ref, q_ref, kn_ref, vn_ref, *rest):
    kp_refs = rest[:SB_PAGES_PER_STEP]
    vp_refs = rest[SB_PAGES_PER_STEP:2 * SB_PAGES_PER_STEP]
    gn_ref, o_ref, acc_ref, drop_ref = rest[2 * SB_PAGES_PER_STEP:]
    step = pl.program_id(1)
    l = q_ref.shape[1]
    rows = N_SB_HEADS * l
    tri = _strict_lower(PAGE_SIZE)
    bias = jnp.concatenate([jnp.full((l, 1), bias_ref[h], F32) for h in range(N_SB_HEADS)], axis=0)
    qs = [(q_ref[0, :, h * HEAD_DIM:(h + 1) * HEAD_DIM] * QK_SCALE).astype(BF16) for h in range(N_SB_HEADS)]

    def attend(keys, values, drop, mask):
        z = [lax.dot_general(qs[h], keys[h], NT_DIMS, preferred_element_type=F32) for h in range(N_SB_HEADS)]
        a, drop = _sb_weights(jnp.concatenate(z, axis=0) + bias, tri, drop, mask)
        out = [jnp.dot(a[h * l:(h + 1) * l, :].astype(BF16), values[h], preferred_element_type=F32)
               for h in range(N_SB_HEADS)]
        return jnp.concatenate(out, axis=0), drop

    @pl.when(step == 0)
    def _():
        pad = jnp.zeros((PAGE_SIZE - l, HEAD_DIM), F32)
        kn = [jnp.concatenate([kn_ref[0, :, h * HEAD_DIM:(h + 1) * HEAD_DIM], pad], axis=0).astype(BF16)
              for h in range(N_SB_HEADS)]
        vn = [jnp.concatenate([vn_ref[0, :, h * HEAD_DIM:(h + 1) * HEAD_DIM], pad], axis=0).astype(BF16)
              for h in range(N_SB_HEADS)]
        qi = lax.broadcasted_iota(jnp.int32, (rows, PAGE_SIZE), 0) % l
        kj = lax.broadcasted_iota(jnp.int32, (rows, PAGE_SIZE), 1)
        out, drop = attend(kn, vn, jnp.zeros((rows, 1), F32), kj < qi)
        acc_ref[...] = out
        drop_ref[...] = drop

    acc = acc_ref[...]
    drop = drop_ref[...]
    for kp_ref, vp_ref in zip(kp_refs, vp_refs):
        head_rows = lambda h: pl.ds(h, PAGE_SIZE, stride=N_SB_HEADS)
        kp = [kp_ref[head_rows(h), :].astype(BF16) for h in range(N_SB_HEADS)]
        vp = [vp_ref[head_rows(h), :].astype(BF16) for h in range(N_SB_HEADS)]
        out, drop = attend(kp, vp, drop, None)
        acc = acc + out
    acc_ref[...] = acc
    drop_ref[...] = drop

    @pl.when(step == pl.num_programs(1) - 1)
    def _():
        for h in range(N_SB_HEADS):
            cols = slice(h * HEAD_DIM, (h + 1) * HEAD_DIM)
            o_ref[0, :, cols] = _head_rmsnorm(acc[h * l:(h + 1) * l, :], gn_ref[:, cols])


def _sb_sample(sq, sk, sv, cache_k, cache_v, page_table, bias, gn):
    b, l, _ = sq.shape
    n_pages = page_table.shape[1]
    assert n_pages % SB_PAGES_PER_STEP == 0
    n_phys = cache_k.shape[0]
    page_rows = PAGE_SIZE * N_SB_HEADS
    cache_k = cache_k.reshape(n_phys, page_rows, HEAD_DIM)
    cache_v = cache_v.reshape(n_phys, page_rows, HEAD_DIM)
    tok = pl.BlockSpec((1, l, SB_WIDTH), lambda i, s, pt, bs: (i, 0, 0))

    def page(n):
        return pl.BlockSpec((None, page_rows, HEAD_DIM),
                            lambda i, s, pt, bs: (pt[i, n_pages - 1 - (s * SB_PAGES_PER_STEP + n)], 0, 0))

    pages = [page(n) for n in range(SB_PAGES_PER_STEP)]
    return pl.pallas_call(
        _sb_sample_body,
        grid_spec=pltpu.PrefetchScalarGridSpec(
            num_scalar_prefetch=2,
            grid=(b, n_pages // SB_PAGES_PER_STEP),
            in_specs=[tok, tok, tok] + pages + pages + [pl.BlockSpec((1, SB_WIDTH), lambda i, s, pt, bs: (0, 0))],
            out_specs=tok,
            scratch_shapes=[pltpu.VMEM((N_SB_HEADS * l, HEAD_DIM), F32), pltpu.VMEM((N_SB_HEADS * l, 1), F32)],
        ),
        out_shape=jax.ShapeDtypeStruct((b, l, SB_WIDTH), F32),
        compiler_params=_params("parallel", "arbitrary"),
        name="stick_breaking_sample",
    )(page_table, bias.astype(F32), sq, sk, sv, *([cache_k] * SB_PAGES_PER_STEP), *([cache_v] * SB_PAGES_PER_STEP),
      gn.reshape(1, SB_WIDTH))


def _cross_body(q_ref, k_ref, v_ref, o_ref):
    for h in range(N_MEM_HEADS):
        cols = slice(h * HEAD_DIM, (h + 1) * HEAD_DIM)
        q = q_ref[0, :, cols].astype(BF16)
        k = k_ref[0, :, cols].astype(BF16)
        v = v_ref[0, :, cols].astype(BF16)
        s = lax.dot_general(q, k, NT_DIMS, preferred_element_type=F32) * QK_SCALE
        e = jnp.exp(s - jnp.max(s, axis=-1, keepdims=True))
        p = e / jnp.sum(e, axis=-1, keepdims=True)
        o_ref[0, :, cols] = jnp.dot(p.astype(BF16), v, preferred_element_type=F32)


def _cross_attention(q, mem_k, mem_v, tq):
    b, t, _ = q.shape
    m = mem_k.shape[1]
    tq = min(tq, t)
    qspec = pl.BlockSpec((1, tq, MEM_WIDTH), lambda i, j: (i, j, 0))
    mspec = pl.BlockSpec((1, m, MEM_WIDTH), lambda i, j: (i, 0, 0))
    return pl.pallas_call(
        _cross_body,
        grid=(b, t // tq),
        in_specs=[qspec, mspec, mspec],
        out_specs=qspec,
        out_shape=jax.ShapeDtypeStruct(q.shape, F32),
        compiler_params=_params("parallel", "arbitrary"),
        name="memory_cross_attention",
    )(q, mem_k, mem_v)


def _expert_body(be_ref, x_ref, wg_ref, wu_ref, wd_ref, o_ref, wg_b, wu_b, wd_b):
    b = pl.program_id(0)
    prev = be_ref[jnp.maximum(b - 1, 0)]

    @pl.when(jnp.logical_or(b == 0, be_ref[b] != prev))
    def _():
        wg_b[...] = wg_ref[0].astype(BF16)
        wu_b[...] = wu_ref[0].astype(BF16)
        wd_b[...] = wd_ref[0].astype(BF16)

    x = x_ref[...]
    g = jnp.dot(x, wg_b[...], preferred_element_type=F32)
    u = jnp.dot(x, wu_b[...], preferred_element_type=F32)
    a = (_silu(g) * u).astype(BF16)
    o_ref[...] = jnp.dot(a, wd_b[...], preferred_element_type=F32)


def _expert_mlp(x, block_expert, w_gate, w_up, w_down, rows):
    r, d = x.shape
    f = w_gate.shape[2]
    wspec = lambda shape: pl.BlockSpec((1,) + shape, lambda i, be: (be[i], 0, 0))
    return pl.pallas_call(
        _expert_body,
        grid_spec=pltpu.PrefetchScalarGridSpec(
            num_scalar_prefetch=1,
            grid=(r // rows,),
            in_specs=[pl.BlockSpec((rows, d), lambda i, be: (i, 0)), wspec((d, f)), wspec((d, f)), wspec((f, d))],
            out_specs=pl.BlockSpec((rows, d), lambda i, be: (i, 0)),
            scratch_shapes=[pltpu.VMEM((d, f), BF16), pltpu.VMEM((d, f), BF16), pltpu.VMEM((f, d), BF16)],
        ),
        out_shape=jax.ShapeDtypeStruct((r, d), F32),
        compiler_params=_params("arbitrary"),
        name="expert_mlp",
    )(block_expert, x, w_gate, w_up, w_down)


def _combine_body(slot_ref, x_ref, sh_ref, gate_ref, g_ref, yb_hbm, o_ref, buf, sem):
    i = pl.program_id(0)
    tm = x_ref.shape[0]

    def start_gather(tile, b):
        base = tile * (tm * TOP_K)

        def token(r, carry):
            for k in range(TOP_K):
                src = yb_hbm.at[pl.ds(slot_ref[base + r * TOP_K + k], 1), :]
                pltpu.make_async_copy(src, buf.at[b, pl.ds(k * tm + r, 1), :], sem.at[b]).start()
            return carry

        lax.fori_loop(0, tm, token, 0)

    @pl.when(i == 0)
    def _():
        start_gather(0, 0)

    @pl.when(i + 1 < pl.num_programs(0))
    def _():
        start_gather(i + 1, (i + 1) % 2)

    b = i % 2
    pltpu.make_async_copy(yb_hbm.at[pl.ds(0, TOP_K * tm), :], buf.at[b], sem.at[b]).wait()
    routed = buf[b, 0:tm, :] * gate_ref[:, 0:1]
    for k in range(1, TOP_K):
        routed = routed + buf[b, k * tm:(k + 1) * tm, :] * gate_ref[:, k:k + 1]
    x = x_ref[...] + (routed + sh_ref[...])
    o_ref[...] = (x * lax.rsqrt(jnp.mean(x * x, axis=-1, keepdims=True) + RMS_EPS)) * g_ref[...]


def _combine_final_norm(x, yb, slot, gates, shared, gain):
    m, d = x.shape
    tm = COMBINE_ROWS
    assert m % tm == 0 and yb.shape[0] >= TOP_K * tm
    row = pl.BlockSpec((tm, d), lambda i, s: (i, 0))
    return pl.pallas_call(
        _combine_body,
        grid_spec=pltpu.PrefetchScalarGridSpec(
            num_scalar_prefetch=1,
            grid=(m // tm,),
            in_specs=[row, row, pl.BlockSpec((tm, TOP_K), lambda i, s: (i, 0)),
                      pl.BlockSpec((1, d), lambda i, s: (0, 0)), pl.BlockSpec(memory_space=pl.ANY)],
            out_specs=row,
            scratch_shapes=[pltpu.VMEM((2, TOP_K * tm, d), F32), pltpu.SemaphoreType.DMA((2,))],
        ),
        out_shape=jax.ShapeDtypeStruct((m, d), F32),
        compiler_params=_params("arbitrary"),
        name="combine_final_norm",
    )(slot.reshape(-1), x, shared, gates, gain.reshape(1, d), yb)


def _moe(hn, logits, router_bias, w_gate, w_up, w_down, ws_gate, ws_up, ws_down):
    t, d = hn.shape
    scores = jax.nn.sigmoid(logits)
    _, idx = lax.top_k(scores + router_bias.astype(F32), TOP_K)
    sel = jnp.take_along_axis(scores, idx, axis=-1)
    gates = sel / jnp.sum(sel, axis=-1, keepdims=True) * ROUTED_SCALE

    chosen = jnp.zeros((t, N_EXPERTS), jnp.int32).at[jnp.arange(t)[:, None], idx].set(1)
    rank = jnp.cumsum(chosen, axis=0) - chosen
    counts = jnp.sum(chosen, axis=0)
    padded = (counts + EXPERT_ROWS - 1) // EXPERT_ROWS * EXPERT_ROWS
    pad_end = jnp.cumsum(padded)
    pad_start = pad_end - padded
    slot = pad_start[idx] + jnp.take_along_axis(rank, idx, axis=-1)
    n_blocks = -(-(t * TOP_K) // EXPERT_ROWS) + N_EXPERTS
    slot_tok = jnp.full((n_blocks * EXPERT_ROWS,), t, jnp.int32).at[slot.reshape(-1)].set(
        jnp.repeat(jnp.arange(t, dtype=jnp.int32), TOP_K))
    block_e = jnp.minimum(jnp.searchsorted(pad_end, jnp.arange(n_blocks) * EXPERT_ROWS, side='right'),
                          N_EXPERTS - 1).astype(jnp.int32)
    h_pad = jnp.concatenate([hn, jnp.zeros((1, d), hn.dtype)], axis=0)
    yb = _expert_mlp(h_pad[slot_tok], block_e, w_gate, w_up, w_down, EXPERT_ROWS)
    shared = _expert_mlp(hn, jnp.zeros((t // EXPERT_ROWS,), jnp.int32), ws_gate[None], ws_up[None], ws_down[None],
                         EXPERT_ROWS)
    return yb, slot.astype(jnp.int32), gates, shared


def _split_proj(proj):
    cuts = [RET_WIDTH * n for n in range(1, 5)] + [4 * RET_WIDTH + SB_WIDTH, 4 * RET_WIDTH + 2 * SB_WIDTH]
    return jnp.split(proj, cuts, axis=-1)


def kernel(x_prompt, x_sample, mem_prompt, cache_sb_k, cache_sb_v, page_table, state_ret, cache_mem_k, cache_mem_v,
           norm_mix, w_in, ret_gn, sb_gn, sb_bias, w_out, norm_mem, w_ck, w_cv, norm_cross, w_cq, w_co, norm_ffn,
           router_w, router_bias, w_gate, w_up, w_down, ws_gate, ws_up, ws_down, norm_final):
    depth = w_in.shape[0]
    assert depth == 1 and x_prompt.shape[0] == 1
    _, t, d = x_prompt.shape
    b, l, _ = x_sample.shape
    n_pages = page_table.shape[1]
    lyr = 0
    xp = x_prompt.reshape(t, d)
    xs = x_sample.reshape(b * l, d)
    pos_p = jnp.arange(t)
    pos_s = n_pages * PAGE_SIZE + jnp.arange(l)

    w_in_b = w_in[lyr].astype(BF16)
    w_out_b = w_out[lyr].astype(BF16)
    w_cq_b = w_cq[lyr].astype(BF16)
    w_co_b = w_co[lyr].astype(BF16)

    mem = mem_prompt.reshape(-1, d)
    n_mem = mem.shape[0]
    mkv = _mm(mem, jnp.concatenate([w_ck[lyr], w_cv[lyr]], axis=1).astype(BF16), gain=norm_mem[lyr])
    mk_p, mv_p = mkv[:, :MEM_WIDTH], mkv[:, MEM_WIDTH:]

    rq, rk, rv, rg, sq, sk_p, sv_p = _split_proj(_mm(xp, w_in_b, gain=norm_mix[lyr]))
    o_ret, s_ret_p = _retention_prompt(rq, rk, rv, rg, pos_p, ret_gn[lyr])
    o_sb = _sb_prompt(sq, sk_p, sv_p, sb_bias[lyr], sb_gn[lyr])
    xp = _mm(jnp.concatenate([o_ret, o_sb], axis=-1), w_out_b, res=xp)

    rq, rk, rv, rg, sq, sk_s, sv_s = [a.reshape(b, l, -1) for a in _split_proj(_mm(xs, w_in_b, gain=norm_mix[lyr]))]
    o_ret, s_ret_s = _retention_sample(rq, rk, rv, rg, state_ret[lyr], pos_s, ret_gn[lyr])
    o_sb = _sb_sample(sq, sk_s, sv_s, cache_sb_k[lyr], cache_sb_v[lyr], page_table, sb_bias[lyr], sb_gn[lyr])
    xs = _mm(jnp.concatenate([o_ret, o_sb], axis=-1).reshape(b * l, -1), w_out_b, res=xs)

    q = _mm(xp, w_cq_b, gain=norm_cross[lyr])
    o = _cross_attention(q[None], mk_p[None], mv_p[None], tq=512)
    xp = _mm(o[0], w_co_b, res=xp)
    q = _mm(xs, w_cq_b, gain=norm_cross[lyr]).reshape(b, l, MEM_WIDTH)
    o = _cross_attention(q, cache_mem_k[lyr].reshape(b, n_mem, MEM_WIDTH),
                         cache_mem_v[lyr].reshape(b, n_mem, MEM_WIDTH), tq=l)
    xs = _mm(o.reshape(b * l, MEM_WIDTH), w_co_b, res=xs)

    x_all = jnp.concatenate([xp, xs], axis=0)
    logits, hn = _mm(x_all, router_w[lyr].astype(BF16), gain=norm_ffn[lyr], emit_xn=True)
    yb, slot, gates, shared = _moe(hn, logits, router_bias[lyr], w_gate[lyr], w_up[lyr], w_down[lyr],
                                   ws_gate[lyr], ws_up[lyr], ws_down[lyr])
    y_all = _combine_final_norm(x_all, yb, slot, gates, shared, norm_final)

    y_prompt = y_all[:t].reshape(1, t, d)
    y_sample = y_all[t:].reshape(b, l, d)
    return (y_prompt, y_sample,
            s_ret_p[None, None],
            sk_p.reshape(1, 1, t, N_SB_HEADS, HEAD_DIM), sv_p.reshape(1, 1, t, N_SB_HEADS, HEAD_DIM),
            mk_p.reshape(1, 1, n_mem, N_MEM_HEADS, HEAD_DIM), mv_p.reshape(1, 1, n_mem, N_MEM_HEADS, HEAD_DIM),
            s_ret_s[None],
            sk_s.reshape(1, b, l, N_SB_HEADS, HEAD_DIM), sv_s.reshape(1, b, l, N_SB_HEADS, HEAD_DIM))
```

```python
import functools

import jax
import jax.numpy as jnp
from jax import lax
from jax.experimental import pallas as pl
from jax.experimental.pallas import tpu as pltpu

F32 = jnp.float32
BF16 = jnp.bfloat16

HEAD_DIM = 128
N_RET_HEADS = 8
N_SB_HEADS = 8
RET_WIDTH = N_RET_HEADS * HEAD_DIM
SB_WIDTH = N_SB_HEADS * HEAD_DIM
RET_CHUNK = 128
PAGE_SIZE = 128
ROPE_BASE = 10000.0
N_MEM_HEADS = 4
MEM_WIDTH = N_MEM_HEADS * HEAD_DIM
N_EXPERTS = 64
TOP_K = 8
ROUTED_SCALE = 2.5
RMS_EPS = 1e-6
NORM_EPS = 1e-6
QK_SCALE = HEAD_DIM ** -0.5

VMEM_LIMIT_BYTES = 56 * 1024 * 1024
SB_KEYS = 256
SB_QUERIES = 512
SB_PAGES_PER_STEP = 8
EXPERT_ROWS = 256
COMBINE_ROWS = 128
LOG2E = 1.4426950408889634

NT_DIMS = (((1,), (1,)), ((), ()))
TN_DIMS = (((0,), (0,)), ((), ()))


def _params(*sem):
    return pltpu.CompilerParams(dimension_semantics=sem, vmem_limit_bytes=VMEM_LIMIT_BYTES)


def _sigmoid(x):
    return 1.0 / (1.0 + jnp.exp(-x))


def _silu(x):
    return x * _sigmoid(x)


def _mm_body(*refs, has_norm, has_res, emit_xn):
    it = iter(refs)
    x_ref = next(it)
    g_ref = next(it) if has_norm else None
    w_ref = next(it)
    r_ref = next(it) if has_res else None
    o_ref = next(it)
    xo_ref = next(it) if emit_xn else None
    xn_ref = next(it)

    @pl.when(pl.program_id(1) == 0)
    def _():
        x = x_ref[...].astype(F32)
        if has_norm:
            x = x * lax.rsqrt(jnp.mean(x * x, axis=-1, keepdims=True) + RMS_EPS)
            x = x * g_ref[...]
        xb = x.astype(BF16)
        xn_ref[...] = xb
        if emit_xn:
            xo_ref[...] = x

    acc = jnp.dot(xn_ref[...], w_ref[...], preferred_element_type=F32)
    if has_res:
        acc = r_ref[...] + acc
    o_ref[...] = acc


def _mm(x, w, gain=None, res=None, emit_xn=False, tm=512, tn=512):
    m, k = x.shape
    n = w.shape[1]
    tm = min(tm, m)
    tn = min(tn, n)
    assert m % tm == 0 and n % tn == 0
    in_specs = [pl.BlockSpec((tm, k), lambda i, j: (i, 0))]
    args = [x]
    if gain is not None:
        in_specs.append(pl.BlockSpec((1, k), lambda i, j: (0, 0)))
        args.append(gain.reshape(1, k).astype(F32))
    in_specs.append(pl.BlockSpec((k, tn), lambda i, j: (0, j)))
    args.append(w)
    if res is not None:
        in_specs.append(pl.BlockSpec((tm, tn), lambda i, j: (i, j)))
        args.append(res)
    out_shape = [jax.ShapeDtypeStruct((m, n), F32)]
    out_specs = [pl.BlockSpec((tm, tn), lambda i, j: (i, j))]
    if emit_xn:
        out_shape.append(jax.ShapeDtypeStruct((m, k), F32))
        out_specs.append(pl.BlockSpec((tm, k), lambda i, j: (i, 0)))
    outs = pl.pallas_call(
        functools.partial(_mm_body, has_norm=gain is not None, has_res=res is not None, emit_xn=emit_xn),
        grid=(m // tm, n // tn),
        in_specs=in_specs,
        out_specs=out_specs,
        out_shape=out_shape,
        scratch_shapes=[pltpu.VMEM((tm, k), BF16)],
        compiler_params=_params("parallel", "arbitrary"),
        name="norm_matmul",
    )(*args)
    return outs if emit_xn else outs[0]


def _rotary_tables(pos):
    half = HEAD_DIM // 2
    inv_freq = ROPE_BASE ** (-jnp.arange(half, dtype=F32) / half)
    ang = pos.astype(F32)[:, None] * inv_freq[None, :]
    cos, sin = jnp.cos(ang), jnp.sin(ang)
    return jnp.concatenate([cos, cos], axis=-1), jnp.concatenate([-sin, sin], axis=-1)


def _retention_tables(length):
    log_g = jnp.log1p(-jnp.exp2(-5.0 - jnp.arange(N_RET_HEADS, dtype=F32)))
    i = jnp.arange(length, dtype=F32)
    diff = i[:, None] - i[None, :]
    decay = jnp.where(diff >= 0, jnp.exp(jnp.maximum(diff, 0.0)[None] * log_g[:, None, None]), 0.0)
    dq = jnp.exp((i[None, :] + 1.0) * log_g[:, None])
    dk = jnp.exp((length - 1.0 - i)[None, :] * log_g[:, None])
    ds = jnp.exp(length * log_g)
    lanes = (N_RET_HEADS, length, HEAD_DIM)
    return (decay, jnp.broadcast_to(dq[:, :, None], lanes), jnp.broadcast_to(dk[:, :, None], lanes),
            jnp.broadcast_to(ds[:, None, None], (N_RET_HEADS, 8, HEAD_DIM)))


def _rotate(x, cos, sin_signed):
    return x * cos + pltpu.roll(x, HEAD_DIM // 2, 1) * sin_signed


def _retention_step(q, k, v, s, decay, dq, dk, ds):
    qb, kb, vb = q.astype(BF16), k.astype(BF16), v.astype(BF16)
    scores = lax.dot_general(qb, kb, NT_DIMS, preferred_element_type=F32) * decay
    o = jnp.dot(scores.astype(BF16), vb, preferred_element_type=F32)
    o = o + jnp.dot((q * dq).astype(BF16), s.astype(BF16), preferred_element_type=F32)
    kd = (k * dk).astype(BF16)
    s_new = ds * s + lax.dot_general(kd, vb, TN_DIMS, preferred_element_type=F32)
    return o, s_new


def _gated_layernorm(o, gn, gate):
    c = o - jnp.mean(o, axis=-1, keepdims=True)
    y = c * lax.rsqrt(jnp.mean(c * c, axis=-1, keepdims=True) + NORM_EPS)
    return (y * gn) * _silu(gate)


def _ret_prompt_body(q_ref, k_ref, v_ref, g_ref, cos_ref, sin_ref, decay_ref, dq_ref, dk_ref, ds_ref, gn_ref,
                     o_ref, sfin_ref, s_ref):
    c = pl.program_id(1)

    @pl.when(c == 0)
    def _():
        s_ref[...] = jnp.zeros_like(s_ref)

    cos, sin = cos_ref[...], sin_ref[...]
    q = _rotate(q_ref[...], cos, sin)
    k = _rotate(k_ref[...], cos, sin) * QK_SCALE
    o, s_new = _retention_step(q, k, v_ref[...], s_ref[...], decay_ref[0], dq_ref[0], dk_ref[0], ds_ref[0, 0:1, :])
    s_ref[...] = s_new
    o_ref[...] = _gated_layernorm(o, gn_ref[...], g_ref[...])

    @pl.when(c == pl.num_programs(1) - 1)
    def _():
        sfin_ref[0] = s_new


def _retention_prompt(rq, rk, rv, rg, pos, gn):
    t = rq.shape[0]
    nc = t // RET_CHUNK
    cos, sin = _rotary_tables(pos)
    decay, dq, dk, ds = _retention_tables(RET_CHUNK)
    tok = pl.BlockSpec((RET_CHUNK, HEAD_DIM), lambda h, c: (c, h))
    rot = pl.BlockSpec((RET_CHUNK, HEAD_DIM), lambda h, c: (c, 0))
    per_head = lambda rows: pl.BlockSpec((1, rows, HEAD_DIM), lambda h, c: (h, 0, 0))
    return pl.pallas_call(
        _ret_prompt_body,
        grid=(N_RET_HEADS, nc),
        in_specs=[tok, tok, tok, tok, rot, rot, per_head(RET_CHUNK), per_head(RET_CHUNK), per_head(RET_CHUNK),
                  per_head(8), pl.BlockSpec((1, HEAD_DIM), lambda h, c: (0, h))],
        out_specs=[tok, per_head(HEAD_DIM)],
        out_shape=[jax.ShapeDtypeStruct((t, RET_WIDTH), F32),
                   jax.ShapeDtypeStruct((N_RET_HEADS, HEAD_DIM, HEAD_DIM), F32)],
        scratch_shapes=[pltpu.VMEM((HEAD_DIM, HEAD_DIM), F32)],
        compiler_params=_params("parallel", "arbitrary"),
        name="retention_prompt",
    )(rq, rk, rv, rg, cos, sin, decay, dq, dk, ds, gn.reshape(1, RET_WIDTH))


def _ret_sample_body(q_ref, k_ref, v_ref, g_ref, st_ref, cos_ref, sin_ref, decay_ref, dq_ref, dk_ref, ds_ref, gn_ref,
                     o_ref, snew_ref):
    cos, sin = cos_ref[...], sin_ref[...]
    for h in range(N_RET_HEADS):
        cols = slice(h * HEAD_DIM, (h + 1) * HEAD_DIM)
        q = _rotate(q_ref[0, :, cols], cos, sin)
        k = _rotate(k_ref[0, :, cols], cos, sin) * QK_SCALE
        o, s_new = _retention_step(q, k, v_ref[0, :, cols], st_ref[0, h], decay_ref[h], dq_ref[h], dk_ref[h],
                                   ds_ref[h, 0:1, :])
        snew_ref[0, h] = s_new
        o_ref[0, :, cols] = _gated_layernorm(o, gn_ref[:, cols], g_ref[0, :, cols])


def _retention_sample(rq, rk, rv, rg, state, pos, gn):
    b, l, _ = rq.shape
    cos, sin = _rotary_tables(pos)
    decay, dq, dk, ds = _retention_tables(l)
    tok = pl.BlockSpec((1, l, RET_WIDTH), lambda i: (i, 0, 0))
    st = pl.BlockSpec((1, N_RET_HEADS, HEAD_DIM, HEAD_DIM), lambda i: (i, 0, 0, 0))
    full = lambda a: pl.BlockSpec(a.shape, lambda i: (0,) * a.ndim)
    gn2 = gn.reshape(1, RET_WIDTH)
    return pl.pallas_call(
        _ret_sample_body,
        grid=(b,),
        in_specs=[tok, tok, tok, tok, st, full(cos), full(sin), full(decay), full(dq), full(dk), full(ds), full(gn2)],
        out_specs=[tok, st],
        out_shape=[jax.ShapeDtypeStruct((b, l, RET_WIDTH), F32), jax.ShapeDtypeStruct(state.shape, F32)],
        compiler_params=_params("parallel"),
        name="retention_sample",
    )(rq, rk, rv, rg, state, cos, sin, decay, dq, dk, ds, gn2)


def _strict_lower(n):
    j = lax.broadcasted_iota(jnp.int32, (n, n), 0)
    s = lax.broadcasted_iota(jnp.int32, (n, n), 1)
    return jnp.where(j > s, 1.0, 0.0).astype(BF16)


def _sb_block_log2(z2, tri, mask):
    neg_log_keep = jnp.maximum(z2, 0.0) + jnp.log2(1.0 + jnp.exp2(-jnp.abs(z2)))
    log_beta = z2 - neg_log_keep
    if mask is not None:
        neg_log_keep = jnp.where(mask, neg_log_keep, 0.0)
    hi = neg_log_keep.astype(BF16)
    lo = (neg_log_keep - hi.astype(F32)).astype(BF16)
    between = jnp.dot(hi, tri, preferred_element_type=F32) + jnp.dot(lo, tri, preferred_element_type=F32)
    return log_beta - between, jnp.sum(neg_log_keep, axis=-1, keepdims=True)


def _head_rmsnorm(o, gn):
    return (o * lax.rsqrt(jnp.mean(o * o, axis=-1, keepdims=True) + RMS_EPS)) * gn


def _sb_prompt_body(bias_ref, q_ref, k_ref, v_ref, gn_ref, o_ref, kb_ref, vb_ref, tri_ref):
    h = pl.program_id(0)
    i = pl.program_id(1)

    @pl.when(i == 0)
    def _():
        kb_ref[...] = k_ref[...].astype(BF16)
        vb_ref[...] = v_ref[...].astype(BF16)
        tri_ref[...] = _strict_lower(SB_KEYS)

    q = (q_ref[...] * (QK_SCALE * LOG2E)).astype(BF16)
    bias = bias_ref[h] * LOG2E
    tri = tri_ref[...]

    def key_rows(j):
        return pl.ds(pl.multiple_of(j * SB_KEYS, SB_KEYS), SB_KEYS)

    def logits(j):
        return lax.dot_general(q, kb_ref[key_rows(j), :], NT_DIMS, preferred_element_type=F32) + bias

    def weighted_values(log2_w, j, mask=None):
        a = jnp.exp2(log2_w)
        if mask is not None:
            a = jnp.where(mask, a, 0.0)
        return jnp.dot(a.astype(BF16), vb_ref[key_rows(j), :], preferred_element_type=F32)

    ratio = SB_QUERIES // SB_KEYS
    r = lax.broadcasted_iota(jnp.int32, (SB_QUERIES, SB_KEYS), 0)
    c = lax.broadcasted_iota(jnp.int32, (SB_QUERIES, SB_KEYS), 1)
    acc = jnp.zeros((SB_QUERIES, HEAD_DIM), F32)
    drop = jnp.zeros((SB_QUERIES, 1), F32)
    for d in reversed(range(ratio)):
        mask = c + d * SB_KEYS < r
        log2_w, dropped = _sb_block_log2(logits(i * ratio + d), tri, mask)
        acc = acc + weighted_values(log2_w - drop, i * ratio + d, mask)
        drop = drop + dropped

    n_old = i * ratio

    def step(n, state):
        z_prev, log2_w_prev, acc, drop = state
        z_new = logits(jnp.maximum(n_old - 1 - n, 0))
        log2_w, dropped = _sb_block_log2(z_prev, tri, None)
        log2_w = log2_w - drop
        drop = jnp.where(jnp.logical_and(n >= 1, n <= n_old), drop + dropped, drop)
        out = weighted_values(log2_w_prev, jnp.clip(n_old + 1 - n, 0, n_old))
        acc = jnp.where(n >= 2, acc + out, acc)
        return z_new, log2_w, acc, drop

    zeros = jnp.zeros((SB_QUERIES, SB_KEYS), F32)
    _, _, acc, _ = lax.fori_loop(0, jnp.where(n_old > 0, n_old + 2, 0), step, (zeros, zeros, acc, drop))
    o_ref[...] = _head_rmsnorm(acc, gn_ref[...])


def _sb_prompt(sq, sk, sv, bias, gn):
    t = sq.shape[0]
    assert t % SB_QUERIES == 0 and SB_QUERIES % SB_KEYS == 0
    qspec = pl.BlockSpec((SB_QUERIES, HEAD_DIM), lambda h, i, b: (i, h))
    kvspec = pl.BlockSpec((t, HEAD_DIM), lambda h, i, b: (0, h))
    return pl.pallas_call(
        _sb_prompt_body,
        grid_spec=pltpu.PrefetchScalarGridSpec(
            num_scalar_prefetch=1,
            grid=(N_SB_HEADS, t // SB_QUERIES),
            in_specs=[qspec, kvspec, kvspec, pl.BlockSpec((1, HEAD_DIM), lambda h, i, b: (0, h))],
            out_specs=qspec,
            scratch_shapes=[pltpu.VMEM((t, HEAD_DIM), BF16), pltpu.VMEM((t, HEAD_DIM), BF16),
                            pltpu.VMEM((SB_KEYS, SB_KEYS), BF16)],
        ),
        out_shape=jax.ShapeDtypeStruct((t, SB_WIDTH), F32),
        compiler_params=_params("parallel", "arbitrary"),
        name="stick_breaking_prompt",
    )(bias.astype(F32), sq, sk, sv, gn.reshape(1, SB_WIDTH))


def _sb_sample_body(pt_ref, bias_ref, q_ref, kn_ref, vn_ref, *rest):
    kp_refs = rest[:SB_PAGES_PER_STEP]
    vp_refs = rest[SB_PAGES_PER_STEP:2 * SB_PAGES_PER_STEP]
    gn_ref, o_ref, acc_ref, drop_ref = rest[2 * SB_PAGES_PER_STEP:]
    step = pl.program_id(1)
    l = q_ref.shape[1]
    rows = N_SB_HEADS * l
    tri = _strict_lower(PAGE_SIZE)
    bias = jnp.concatenate([jnp.full((l, 1), bias_ref[h] * LOG2E, F32) for h in range(N_SB_HEADS)], axis=0)
    qs = [(q_ref[0, :, h * HEAD_DIM:(h + 1) * HEAD_DIM] * (QK_SCALE * LOG2E)).astype(BF16)
          for h in range(N_SB_HEADS)]

    def logits(keys):
        z = [lax.dot_general(qs[h], keys[h], NT_DIMS, preferred_element_type=F32) for h in range(N_SB_HEADS)]
        return jnp.concatenate(z, axis=0) + bias

    def weighted_values(a, values):
        out = [jnp.dot(a[h * l:(h + 1) * l, :].astype(BF16), values[h], preferred_element_type=F32)
               for h in range(N_SB_HEADS)]
        return jnp.concatenate(out, axis=0)

    @pl.when(step == 0)
    def _():
        pad = jnp.zeros((PAGE_SIZE - l, HEAD_DIM), F32)
        kn = [jnp.concatenate([kn_ref[0, :, h * HEAD_DIM:(h + 1) * HEAD_DIM], pad], axis=0).astype(BF16)
              for h in range(N_SB_HEADS)]
        vn = [jnp.concatenate([vn_ref[0, :, h * HEAD_DIM:(h + 1) * HEAD_DIM], pad], axis=0).astype(BF16)
              for h in range(N_SB_HEADS)]
        qi = lax.broadcasted_iota(jnp.int32, (rows, PAGE_SIZE), 0) % l
        kj = lax.broadcasted_iota(jnp.int32, (rows, PAGE_SIZE), 1)
        mask = kj < qi
        log2_w, dropped = _sb_block_log2(logits(kn), tri, mask)
        acc_ref[...] = weighted_values(jnp.where(mask, jnp.exp2(log2_w), 0.0), vn)
        drop_ref[...] = dropped

    head_rows = lambda h: pl.ds(h, PAGE_SIZE, stride=N_SB_HEADS)
    z = jnp.concatenate([logits([kp_ref[head_rows(h), :].astype(BF16) for h in range(N_SB_HEADS)])
                         for kp_ref in kp_refs], axis=0)
    log2_w, dropped = _sb_block_log2(z, tri, None)
    drop = drop_ref[...]
    acc = acc_ref[...]
    for n, vp_ref in enumerate(vp_refs):
        page = slice(n * rows, (n + 1) * rows)
        a = jnp.exp2(log2_w[page, :] - drop)
        acc = acc + weighted_values(a, [vp_ref[head_rows(h), :].astype(BF16) for h in range(N_SB_HEADS)])
        drop = drop + dropped[page, :]
    acc_ref[...] = acc
    drop_ref[...] = drop

    @pl.when(step == pl.num_programs(1) - 1)
    def _():
        for h in range(N_SB_HEADS):
            cols = slice(h * HEAD_DIM, (h + 1) * HEAD_DIM)
            o_ref[0, :, cols] = _head_rmsnorm(acc[h * l:(h + 1) * l, :], gn_ref[:, cols])


def _sb_sample(sq, sk, sv, cache_k, cache_v, page_table, bias, gn):
    b, l, _ = sq.shape
    n_pages = page_table.shape[1]
    assert n_pages % SB_PAGES_PER_STEP == 0
    n_phys = cache_k.shape[0]
    page_rows = PAGE_SIZE * N_SB_HEADS
    cache_k = cache_k.reshape(n_phys, page_rows, HEAD_DIM)
    cache_v = cache_v.reshape(n_phys, page_rows, HEAD_DIM)
    tok = pl.BlockSpec((1, l, SB_WIDTH), lambda i, s, pt, bs: (i, 0, 0))

    def page(n):
        return pl.BlockSpec((None, page_rows, HEAD_DIM),
                            lambda i, s, pt, bs: (pt[i, n_pages - 1 - (s * SB_PAGES_PER_STEP + n)], 0, 0))

    pages = [page(n) for n in range(SB_PAGES_PER_STEP)]
    return pl.pallas_call(
        _sb_sample_body,
        grid_spec=pltpu.PrefetchScalarGridSpec(
            num_scalar_prefetch=2,
            grid=(b, n_pages // SB_PAGES_PER_STEP),
            in_specs=[tok, tok, tok] + pages + pages + [pl.BlockSpec((1, SB_WIDTH), lambda i, s, pt, bs: (0, 0))],
            out_specs=tok,
            scratch_shapes=[pltpu.VMEM((N_SB_HEADS * l, HEAD_DIM), F32), pltpu.VMEM((N_SB_HEADS * l, 1), F32)],
        ),
        out_shape=jax.ShapeDtypeStruct((b, l, SB_WIDTH), F32),
        compiler_params=_params("parallel", "arbitrary"),
        name="stick_breaking_sample",
    )(page_table, bias.astype(F32), sq, sk, sv, *([cache_k] * SB_PAGES_PER_STEP), *([cache_v] * SB_PAGES_PER_STEP),
      gn.reshape(1, SB_WIDTH))


def _cross_body(q_ref, k_ref, v_ref, o_ref):
    for h in range(N_MEM_HEADS):
        cols = slice(h * HEAD_DIM, (h + 1) * HEAD_DIM)
        q = q_ref[0, :, cols].astype(BF16)
        k = k_ref[0, :, cols].astype(BF16)
        v = v_ref[0, :, cols].astype(BF16)
        s = lax.dot_general(q, k, NT_DIMS, preferred_element_type=F32) * QK_SCALE
        e = jnp.exp(s - jnp.max(s, axis=-1, keepdims=True))
        p = e / jnp.sum(e, axis=-1, keepdims=True)
        o_ref[0, :, cols] = jnp.dot(p.astype(BF16), v, preferred_element_type=F32)


def _cross_attention(q, mem_k, mem_v, tq):
    b, t, _ = q.shape
    m = mem_k.shape[1]
    tq = min(tq, t)
    qspec = pl.BlockSpec((1, tq, MEM_WIDTH), lambda i, j: (i, j, 0))
    mspec = pl.BlockSpec((1, m, MEM_WIDTH), lambda i, j: (i, 0, 0))
    return pl.pallas_call(
        _cross_body,
        grid=(b, t // tq),
        in_specs=[qspec, mspec, mspec],
        out_specs=qspec,
        out_shape=jax.ShapeDtypeStruct(q.shape, F32),
        compiler_params=_params("parallel", "arbitrary"),
        name="memory_cross_attention",
    )(q, mem_k, mem_v)


def _swiglu(x, wg_b, wu_b, wd_b):
    g = jnp.dot(x, wg_b[...], preferred_element_type=F32)
    u = jnp.dot(x, wu_b[...], preferred_element_type=F32)
    return jnp.dot((_silu(g) * u).astype(BF16), wd_b[...], preferred_element_type=F32)


def _cast_weights(wg_ref, wu_ref, wd_ref, wg_b, wu_b, wd_b):
    wg_b[...] = wg_ref[0].astype(BF16)
    wu_b[...] = wu_ref[0].astype(BF16)
    wd_b[...] = wd_ref[0].astype(BF16)


def _shared_body(x_ref, wg_ref, wu_ref, wd_ref, o_ref, wg_b, wu_b, wd_b):
    @pl.when(pl.program_id(0) == 0)
    def _():
        _cast_weights(wg_ref, wu_ref, wd_ref, wg_b, wu_b, wd_b)

    o_ref[...] = _swiglu(x_ref[...].astype(BF16), wg_b, wu_b, wd_b)


def _shared_mlp(x, w_gate, w_up, w_down, rows):
    r, d = x.shape
    f = w_gate.shape[2]
    wspec = lambda shape: pl.BlockSpec((1,) + shape, lambda i: (0, 0, 0))
    return pl.pallas_call(
        _shared_body,
        grid=(r // rows,),
        in_specs=[pl.BlockSpec((rows, d), lambda i: (i, 0)), wspec((d, f)), wspec((d, f)), wspec((f, d))],
        out_specs=pl.BlockSpec((rows, d), lambda i: (i, 0)),
        out_shape=jax.ShapeDtypeStruct((r, d), F32),
        scratch_shapes=[pltpu.VMEM((d, f), BF16), pltpu.VMEM((d, f), BF16), pltpu.VMEM((f, d), BF16)],
        compiler_params=_params("arbitrary"),
        name="shared_mlp",
    )(x, w_gate, w_up, w_down)


def _expert_body(be_ref, used_ref, tok_ref, x_hbm, wg_ref, wu_ref, wd_ref, o_ref, wg_b, wu_b, wd_b, xbuf, sem):
    b = pl.program_id(0)
    rows = o_ref.shape[0]
    n_used = used_ref[0]

    def start_gather(blk, buf):
        base = blk * rows

        def row(r, carry):
            src = x_hbm.at[pl.ds(tok_ref[base + r], 1), :]
            pltpu.make_async_copy(src, xbuf.at[buf, pl.ds(r, 1), :], sem.at[buf]).start()
            return carry

        lax.fori_loop(0, rows, row, 0, unroll=8)

    @pl.when(b == 0)
    def _():
        start_gather(0, 0)

    @pl.when(b + 1 < n_used)
    def _():
        start_gather(b + 1, (b + 1) % 2)

    @pl.when(b < n_used)
    def _():
        @pl.when(jnp.logical_or(b == 0, be_ref[b] != be_ref[jnp.maximum(b - 1, 0)]))
        def _():
            _cast_weights(wg_ref, wu_ref, wd_ref, wg_b, wu_b, wd_b)

        buf = b % 2
        pltpu.make_async_copy(x_hbm.at[pl.ds(0, rows), :], xbuf.at[buf], sem.at[buf]).wait()
        o_ref[...] = _swiglu(xbuf[buf].astype(BF16), wg_b, wu_b, wd_b)

    @pl.when(b >= n_used)
    def _():
        o_ref[...] = jnp.zeros_like(o_ref)


def _expert_mlp(x, row_token, block_expert, n_used, w_gate, w_up, w_down, rows):
    d = x.shape[1]
    f = w_gate.shape[2]
    n_blocks = block_expert.shape[0]
    wspec = lambda shape: pl.BlockSpec((1,) + shape, lambda i, be, nu, tok: (be[i], 0, 0))
    return pl.pallas_call(
        _expert_body,
        grid_spec=pltpu.PrefetchScalarGridSpec(
            num_scalar_prefetch=3,
            grid=(n_blocks,),
            in_specs=[pl.BlockSpec(memory_space=pl.ANY), wspec((d, f)), wspec((d, f)), wspec((f, d))],
            out_specs=pl.BlockSpec((rows, d), lambda i, be, nu, tok: (i, 0)),
            scratch_shapes=[pltpu.VMEM((d, f), BF16), pltpu.VMEM((d, f), BF16), pltpu.VMEM((f, d), BF16),
                            pltpu.VMEM((2, rows, d), F32), pltpu.SemaphoreType.DMA((2,))],
        ),
        out_shape=jax.ShapeDtypeStruct((n_blocks * rows, d), F32),
        compiler_params=_params("arbitrary"),
        name="expert_mlp",
    )(block_expert, n_used, row_token, x, w_gate, w_up, w_down)


def _combine_body(slot_ref, x_ref, sh_ref, gate_ref, g_ref, yb_hbm, o_ref, buf, sem):
    i = pl.program_id(0)
    tm = x_ref.shape[0]

    def start_gather(tile, b):
        base = tile * (tm * TOP_K)

        def token(r, carry):
            for k in range(TOP_K):
                src = yb_hbm.at[pl.ds(slot_ref[base + r * TOP_K + k], 1), :]
                pltpu.make_async_copy(src, buf.at[b, pl.ds(k * tm + r, 1), :], sem.at[b]).start()
            return carry

        lax.fori_loop(0, tm, token, 0)

    @pl.when(i == 0)
    def _():
        start_gather(0, 0)

    @pl.when(i + 1 < pl.num_programs(0))
    def _():
        start_gather(i + 1, (i + 1) % 2)

    b = i % 2
    pltpu.make_async_copy(yb_hbm.at[pl.ds(0, TOP_K * tm), :], buf.at[b], sem.at[b]).wait()
    routed = buf[b, 0:tm, :] * gate_ref[:, 0:1]
    for k in range(1, TOP_K):
        routed = routed + buf[b, k * tm:(k + 1) * tm, :] * gate_ref[:, k:k + 1]
    x = x_ref[...] + (routed + sh_ref[...])
    o_ref[...] = (x * lax.rsqrt(jnp.mean(x * x, axis=-1, keepdims=True) + RMS_EPS)) * g_ref[...]


def _combine_final_norm(x, yb, slot, gates, shared, gain):
    m, d = x.shape
    tm = COMBINE_ROWS
    assert m % tm == 0 and yb.shape[0] >= TOP_K * tm
    row = pl.BlockSpec((tm, d), lambda i, s: (i, 0))
    return pl.pallas_call(
        _combine_body,
        grid_spec=pltpu.PrefetchScalarGridSpec(
            num_scalar_prefetch=1,
            grid=(m // tm,),
            in_specs=[row, row, pl.BlockSpec((tm, TOP_K), lambda i, s: (i, 0)),
                      pl.BlockSpec((1, d), lambda i, s: (0, 0)), pl.BlockSpec(memory_space=pl.ANY)],
            out_specs=row,
            scratch_shapes=[pltpu.VMEM((2, TOP_K * tm, d), F32), pltpu.SemaphoreType.DMA((2,))],
        ),
        out_shape=jax.ShapeDtypeStruct((m, d), F32),
        compiler_params=_params("arbitrary"),
        name="combine_final_norm",
    )(slot.reshape(-1), x, shared, gates, gain.reshape(1, d), yb)


def _moe(hn, logits, router_bias, w_gate, w_up, w_down, ws_gate, ws_up, ws_down):
    t, d = hn.shape
    scores = jax.nn.sigmoid(logits)
    _, idx = lax.top_k(scores + router_bias.astype(F32), TOP_K)
    sel = jnp.take_along_axis(scores, idx, axis=-1)
    gates = sel / jnp.sum(sel, axis=-1, keepdims=True) * ROUTED_SCALE

    chosen = jnp.zeros((t, N_EXPERTS), jnp.int32).at[jnp.arange(t)[:, None], idx].set(1)
    rank = jnp.cumsum(chosen, axis=0) - chosen
    counts = jnp.sum(chosen, axis=0)
    padded = (counts + EXPERT_ROWS - 1) // EXPERT_ROWS * EXPERT_ROWS
    pad_end = jnp.cumsum(padded)
    pad_start = pad_end - padded
    slot = pad_start[idx] + jnp.take_along_axis(rank, idx, axis=-1)
    n_blocks = -(-(t * TOP_K) // EXPERT_ROWS) + N_EXPERTS
    slot_tok = jnp.zeros((n_blocks * EXPERT_ROWS,), jnp.int32).at[slot.reshape(-1)].set(
        jnp.repeat(jnp.arange(t, dtype=jnp.int32), TOP_K))
    block_e = jnp.minimum(jnp.searchsorted(pad_end, jnp.arange(n_blocks) * EXPERT_ROWS, side='right'),
                          N_EXPERTS - 1).astype(jnp.int32)
    n_used = (pad_end[-1:] // EXPERT_ROWS).astype(jnp.int32)
    yb = _expert_mlp(hn, slot_tok, block_e, n_used, w_gate, w_up, w_down, EXPERT_ROWS)
    shared = _shared_mlp(hn, ws_gate[None], ws_up[None], ws_down[None], EXPERT_ROWS)
    return yb, slot.astype(jnp.int32), gates, shared


def _split_proj(proj):
    cuts = [RET_WIDTH * n for n in range(1, 5)] + [4 * RET_WIDTH + SB_WIDTH, 4 * RET_WIDTH + 2 * SB_WIDTH]
    return jnp.split(proj, cuts, axis=-1)


def kernel(x_prompt, x_sample, mem_prompt, cache_sb_k, cache_sb_v, page_table, state_ret, cache_mem_k, cache_mem_v,
           norm_mix, w_in, ret_gn, sb_gn, sb_bias, w_out, norm_mem, w_ck, w_cv, norm_cross, w_cq, w_co, norm_ffn,
           router_w, router_bias, w_gate, w_up, w_down, ws_gate, ws_up, ws_down, norm_final):
    depth = w_in.shape[0]
    assert depth == 1 and x_prompt.shape[0] == 1
    _, t, d = x_prompt.shape
    b, l, _ = x_sample.shape
    n_pages = page_table.shape[1]
    lyr = 0
    xp = x_prompt.reshape(t, d)
    xs = x_sample.reshape(b * l, d)
    pos_p = jnp.arange(t)
    pos_s = n_pages * PAGE_SIZE + jnp.arange(l)

    w_in_b = w_in[lyr].astype(BF16)
    w_out_b = w_out[lyr].astype(BF16)
    w_cq_b = w_cq[lyr].astype(BF16)
    w_co_b = w_co[lyr].astype(BF16)

    mem = mem_prompt.reshape(-1, d)
    n_mem = mem.shape[0]
    mkv = _mm(mem, jnp.concatenate([w_ck[lyr], w_cv[lyr]], axis=1).astype(BF16), gain=norm_mem[lyr])
    mk_p, mv_p = mkv[:, :MEM_WIDTH], mkv[:, MEM_WIDTH:]

    rq, rk, rv, rg, sq, sk_p, sv_p = _split_proj(_mm(xp, w_in_b, gain=norm_mix[lyr]))
    o_ret, s_ret_p = _retention_prompt(rq, rk, rv, rg, pos_p, ret_gn[lyr])
    o_sb = _sb_prompt(sq, sk_p, sv_p, sb_bias[lyr], sb_gn[lyr])
    xp = _mm(jnp.concatenate([o_ret, o_sb], axis=-1), w_out_b, res=xp)

    rq, rk, rv, rg, sq, sk_s, sv_s = [a.reshape(b, l, -1) for a in _split_proj(_mm(xs, w_in_b, gain=norm_mix[lyr]))]
    o_ret, s_ret_s = _retention_sample(rq, rk, rv, rg, state_ret[lyr], pos_s, ret_gn[lyr])
    o_sb = _sb_sample(sq, sk_s, sv_s, cache_sb_k[lyr], cache_sb_v[lyr], page_table, sb_bias[lyr], sb_gn[lyr])
    xs = _mm(jnp.concatenate([o_ret, o_sb], axis=-1).reshape(b * l, -1), w_out_b, res=xs)

    q = _mm(xp, w_cq_b, gain=norm_cross[lyr])
    o = _cross_attention(q[None], mk_p[None], mv_p[None], tq=512)
    xp = _mm(o[0], w_co_b, res=xp)
    q = _mm(xs, w_cq_b, gain=norm_cross[lyr]).reshape(b, l, MEM_WIDTH)
    o = _cross_attention(q, cache_mem_k[lyr].reshape(b, n_mem, MEM_WIDTH),
                         cache_mem_v[lyr].reshape(b, n_mem, MEM_WIDTH), tq=l)
    xs = _mm(o.reshape(b * l, MEM_WIDTH), w_co_b, res=xs)

    x_all = jnp.concatenate([xp, xs], axis=0)
    logits, hn = _mm(x_all, router_w[lyr].astype(BF16), gain=norm_ffn[lyr], emit_xn=True)
    yb, slot, gates, shared = _moe(hn, logits, router_bias[lyr], w_gate[lyr], w_up[lyr], w_down[lyr],
                                   ws_gate[lyr], ws_up[lyr], ws_down[lyr])
    y_all = _combine_final_norm(x_all, yb, slot, gates, shared, norm_final)

    y_prompt = y_all[:t].reshape(1, t, d)
    y_sample = y_all[t:].reshape(b, l, d)
    return (y_prompt, y_sample,
            s_ret_p[None, None],
            sk_p.reshape(1, 1, t, N_SB_HEADS, HEAD_DIM), sv_p.reshape(1, 1, t, N_SB_HEADS, HEAD_DIM),
            mk_p.reshape(1, 1, n_mem, N_MEM_HEADS, HEAD_DIM), mv_p.reshape(1, 1, n_mem, N_MEM_HEADS, HEAD_DIM),
            s_ret_s[None],
            sk_s.reshape(1, b, l, N_SB_HEADS, HEAD_DIM), sv_s.reshape(1, b, l, N_SB_HEADS, HEAD_DIM))
```

```python
import functools

import jax
import jax.numpy as jnp
from jax import lax
from jax.experimental import pallas as pl
from jax.experimental.pallas import tpu as pltpu

F32 = jnp.float32
BF16 = jnp.bfloat16

HEAD_DIM = 128
N_RET_HEADS = 8
N_SB_HEADS = 8
RET_WIDTH = N_RET_HEADS * HEAD_DIM
SB_WIDTH = N_SB_HEADS * HEAD_DIM
RET_CHUNK = 128
PAGE_SIZE = 128
ROPE_BASE = 10000.0
N_MEM_HEADS = 4
MEM_WIDTH = N_MEM_HEADS * HEAD_DIM
N_EXPERTS = 64
TOP_K = 8
ROUTED_SCALE = 2.5
RMS_EPS = 1e-6
NORM_EPS = 1e-6
QK_SCALE = HEAD_DIM ** -0.5

VMEM_LIMIT_BYTES = 56 * 1024 * 1024
SB_KEYS = 256
SB_QUERIES = 512
SB_PAGES_PER_STEP = 8
EXPERT_ROWS = 256
COMBINE_ROWS = 128
SB_CHUNK = 128
LOG2E = 1.4426950408889634

NT_DIMS = (((1,), (1,)), ((), ()))
TN_DIMS = (((0,), (0,)), ((), ()))


def _params(*sem):
    return pltpu.CompilerParams(dimension_semantics=sem, vmem_limit_bytes=VMEM_LIMIT_BYTES)


def _sigmoid(x):
    return 1.0 / (1.0 + jnp.exp(-x))


def _silu(x):
    return x * _sigmoid(x)


def _mm_body(*refs, has_norm, has_res, emit_xn):
    it = iter(refs)
    x_ref = next(it)
    g_ref = next(it) if has_norm else None
    w_ref = next(it)
    r_ref = next(it) if has_res else None
    o_ref = next(it)
    xo_ref = next(it) if emit_xn else None
    xn_ref = next(it)

    @pl.when(pl.program_id(1) == 0)
    def _():
        x = x_ref[...].astype(F32)
        if has_norm:
            x = x * lax.rsqrt(jnp.mean(x * x, axis=-1, keepdims=True) + RMS_EPS)
            x = x * g_ref[...]
        xb = x.astype(BF16)
        xn_ref[...] = xb
        if emit_xn:
            xo_ref[...] = x

    acc = jnp.dot(xn_ref[...], w_ref[...], preferred_element_type=F32)
    if has_res:
        acc = r_ref[...] + acc
    o_ref[...] = acc


def _mm(x, w, gain=None, res=None, emit_xn=False, tm=512, tn=512):
    m, k = x.shape
    n = w.shape[1]
    tm = min(tm, m)
    tn = min(tn, n)
    assert m % tm == 0 and n % tn == 0
    in_specs = [pl.BlockSpec((tm, k), lambda i, j: (i, 0))]
    args = [x]
    if gain is not None:
        in_specs.append(pl.BlockSpec((1, k), lambda i, j: (0, 0)))
        args.append(gain.reshape(1, k).astype(F32))
    in_specs.append(pl.BlockSpec((k, tn), lambda i, j: (0, j)))
    args.append(w)
    if res is not None:
        in_specs.append(pl.BlockSpec((tm, tn), lambda i, j: (i, j)))
        args.append(res)
    out_shape = [jax.ShapeDtypeStruct((m, n), F32)]
    out_specs = [pl.BlockSpec((tm, tn), lambda i, j: (i, j))]
    if emit_xn:
        out_shape.append(jax.ShapeDtypeStruct((m, k), F32))
        out_specs.append(pl.BlockSpec((tm, k), lambda i, j: (i, 0)))
    outs = pl.pallas_call(
        functools.partial(_mm_body, has_norm=gain is not None, has_res=res is not None, emit_xn=emit_xn),
        grid=(m // tm, n // tn),
        in_specs=in_specs,
        out_specs=out_specs,
        out_shape=out_shape,
        scratch_shapes=[pltpu.VMEM((tm, k), BF16)],
        compiler_params=_params("parallel", "arbitrary"),
        name="norm_matmul",
    )(*args)
    return outs if emit_xn else outs[0]


def _rotary_tables(pos):
    half = HEAD_DIM // 2
    inv_freq = ROPE_BASE ** (-jnp.arange(half, dtype=F32) / half)
    ang = pos.astype(F32)[:, None] * inv_freq[None, :]
    cos, sin = jnp.cos(ang), jnp.sin(ang)
    return jnp.concatenate([cos, cos], axis=-1), jnp.concatenate([-sin, sin], axis=-1)


def _retention_tables(length):
    log_g = jnp.log1p(-jnp.exp2(-5.0 - jnp.arange(N_RET_HEADS, dtype=F32)))
    i = jnp.arange(length, dtype=F32)
    diff = i[:, None] - i[None, :]
    decay = jnp.where(diff >= 0, jnp.exp(jnp.maximum(diff, 0.0)[None] * log_g[:, None, None]), 0.0)
    dq = jnp.exp((i[None, :] + 1.0) * log_g[:, None])
    dk = jnp.exp((length - 1.0 - i)[None, :] * log_g[:, None])
    ds = jnp.exp(length * log_g)
    lanes = (N_RET_HEADS, length, HEAD_DIM)
    return (decay, jnp.broadcast_to(dq[:, :, None], lanes), jnp.broadcast_to(dk[:, :, None], lanes),
            jnp.broadcast_to(ds[:, None, None], (N_RET_HEADS, 8, HEAD_DIM)))


def _rotate(x, cos, sin_signed):
    return x * cos + pltpu.roll(x, HEAD_DIM // 2, 1) * sin_signed


def _retention_step(q, k, v, s, decay, dq, dk, ds):
    qb, kb, vb = q.astype(BF16), k.astype(BF16), v.astype(BF16)
    scores = lax.dot_general(qb, kb, NT_DIMS, preferred_element_type=F32) * decay
    o = jnp.dot(scores.astype(BF16), vb, preferred_element_type=F32)
    o = o + jnp.dot((q * dq).astype(BF16), s.astype(BF16), preferred_element_type=F32)
    kd = (k * dk).astype(BF16)
    s_new = ds * s + lax.dot_general(kd, vb, TN_DIMS, preferred_element_type=F32)
    return o, s_new


def _gated_layernorm(o, gn, gate):
    c = o - jnp.mean(o, axis=-1, keepdims=True)
    y = c * lax.rsqrt(jnp.mean(c * c, axis=-1, keepdims=True) + NORM_EPS)
    return (y * gn) * _silu(gate)


def _ret_prompt_body(q_ref, k_ref, v_ref, g_ref, cos_ref, sin_ref, decay_ref, dq_ref, dk_ref, ds_ref, gn_ref,
                     o_ref, sfin_ref, s_ref):
    c = pl.program_id(0)

    @pl.when(c == 0)
    def _():
        s_ref[...] = jnp.zeros_like(s_ref)

    cos, sin = cos_ref[...], sin_ref[...]
    for h in range(N_RET_HEADS):
        cols = slice(h * HEAD_DIM, (h + 1) * HEAD_DIM)
        q = _rotate(q_ref[:, cols], cos, sin)
        k = _rotate(k_ref[:, cols], cos, sin) * QK_SCALE
        o, s_new = _retention_step(q, k, v_ref[:, cols], s_ref[h], decay_ref[h], dq_ref[h], dk_ref[h],
                                   ds_ref[h, 0:1, :])
        s_ref[h] = s_new
        o_ref[:, cols] = _gated_layernorm(o, gn_ref[:, cols], g_ref[:, cols])

    @pl.when(c == pl.num_programs(0) - 1)
    def _():
        sfin_ref[...] = s_ref[...]


def _retention_prompt(proj, pos, gn):
    t = proj.shape[0]
    cos, sin = _rotary_tables(pos)
    decay, dq, dk, ds = _retention_tables(RET_CHUNK)
    segment = lambda j: pl.BlockSpec((RET_CHUNK, RET_WIDTH), lambda c: (c, j))
    rot = pl.BlockSpec((RET_CHUNK, HEAD_DIM), lambda c: (c, 0))
    full = lambda a: pl.BlockSpec(a.shape, lambda c: (0,) * a.ndim)
    gn2 = gn.reshape(1, RET_WIDTH)
    state = jax.ShapeDtypeStruct((N_RET_HEADS, HEAD_DIM, HEAD_DIM), F32)
    return pl.pallas_call(
        _ret_prompt_body,
        grid=(t // RET_CHUNK,),
        in_specs=[segment(0), segment(1), segment(2), segment(3), rot, rot, full(decay), full(dq), full(dk), full(ds),
                  full(gn2)],
        out_specs=[segment(0), pl.BlockSpec(state.shape, lambda c: (0, 0, 0))],
        out_shape=[jax.ShapeDtypeStruct((t, RET_WIDTH), F32), state],
        scratch_shapes=[pltpu.VMEM(state.shape, F32)],
        compiler_params=_params("arbitrary"),
        name="retention_prompt",
    )(proj, proj, proj, proj, cos, sin, decay, dq, dk, ds, gn2)


def _ret_sample_body(q_ref, k_ref, v_ref, g_ref, st_ref, cos_ref, sin_ref, decay_ref, dq_ref, dk_ref, ds_ref, gn_ref,
                     o_ref, snew_ref):
    cos, sin = cos_ref[...], sin_ref[...]
    for h in range(N_RET_HEADS):
        cols = slice(h * HEAD_DIM, (h + 1) * HEAD_DIM)
        q = _rotate(q_ref[0, :, cols], cos, sin)
        k = _rotate(k_ref[0, :, cols], cos, sin) * QK_SCALE
        o, s_new = _retention_step(q, k, v_ref[0, :, cols], st_ref[0, h], decay_ref[h], dq_ref[h], dk_ref[h],
                                   ds_ref[h, 0:1, :])
        snew_ref[0, h] = s_new
        o_ref[0, :, cols] = _gated_layernorm(o, gn_ref[:, cols], g_ref[0, :, cols])


def _retention_sample(proj, state, pos, gn):
    b, l, _ = proj.shape
    cos, sin = _rotary_tables(pos)
    decay, dq, dk, ds = _retention_tables(l)
    segment = lambda j: pl.BlockSpec((1, l, RET_WIDTH), lambda i: (i, 0, j))
    st = pl.BlockSpec((1, N_RET_HEADS, HEAD_DIM, HEAD_DIM), lambda i: (i, 0, 0, 0))
    full = lambda a: pl.BlockSpec(a.shape, lambda i: (0,) * a.ndim)
    gn2 = gn.reshape(1, RET_WIDTH)
    return pl.pallas_call(
        _ret_sample_body,
        grid=(b,),
        in_specs=[segment(0), segment(1), segment(2), segment(3), st, full(cos), full(sin), full(decay), full(dq),
                  full(dk), full(ds), full(gn2)],
        out_specs=[segment(0), st],
        out_shape=[jax.ShapeDtypeStruct((b, l, RET_WIDTH), F32), jax.ShapeDtypeStruct(state.shape, F32)],
        compiler_params=_params("parallel"),
        name="retention_sample",
    )(proj, proj, proj, proj, state, cos, sin, decay, dq, dk, ds, gn2)


def _suffix_matrix(n):
    j = lax.broadcasted_iota(jnp.int32, (2 * n, n), 0) % n
    s = lax.broadcasted_iota(jnp.int32, (2 * n, n), 1)
    return jnp.where(j > s, 1.0, 0.0).astype(BF16)


def _sb_block_log2(z2, tri, mask):
    rows, keys = z2.shape
    chunk = min(rows, SB_CHUNK)
    suffix = tri.shape[1]
    log2_w, dropped = [], []
    for c in range(0, rows, chunk):
        parts, later = [], None
        for s in reversed(range(0, keys, suffix)):
            zc = z2[c:c + chunk, s:s + suffix]
            neg_log_keep = jnp.maximum(zc, 0.0) + jnp.log2(1.0 + jnp.exp2(-jnp.abs(zc)))
            log_beta = zc - neg_log_keep
            if mask is not None:
                neg_log_keep = jnp.where(mask[c:c + chunk, s:s + suffix], neg_log_keep, 0.0)
            hi = neg_log_keep.astype(BF16)
            lo = (neg_log_keep - hi.astype(F32)).astype(BF16)
            between = jnp.dot(jnp.concatenate([hi, lo], axis=1), tri, preferred_element_type=F32)
            total = jnp.sum(neg_log_keep, axis=-1, keepdims=True)
            if later is None:
                parts.append(log_beta - between)
                later = total
            else:
                parts.append(log_beta - between - later)
                later = later + total
        log2_w.append(jnp.concatenate(parts[::-1], axis=1))
        dropped.append(later)
    return jnp.concatenate(log2_w, axis=0), jnp.concatenate(dropped, axis=0)


def _head_rmsnorm(o, gn):
    return (o * lax.rsqrt(jnp.mean(o * o, axis=-1, keepdims=True) + RMS_EPS)) * gn


def _sb_prompt_body(bias_ref, q_ref, k_ref, v_ref, gn_ref, o_ref, kb_ref, vb_ref, tri_ref):
    h = pl.program_id(0)
    i = pl.program_id(1)

    @pl.when(i == 0)
    def _():
        kb_ref[...] = k_ref[...].astype(BF16)
        vb_ref[...] = v_ref[...].astype(BF16)
        tri_ref[...] = _suffix_matrix(SB_KEYS)

    q = (q_ref[...] * (QK_SCALE * LOG2E)).astype(BF16)
    bias = bias_ref[h] * LOG2E
    tri = tri_ref[...]

    def key_rows(j):
        return pl.ds(pl.multiple_of(j * SB_KEYS, SB_KEYS), SB_KEYS)

    def logits(j):
        return lax.dot_general(q, kb_ref[key_rows(j), :], NT_DIMS, preferred_element_type=F32) + bias

    def weighted_values(log2_w, j, mask=None):
        a = jnp.exp2(log2_w)
        if mask is not None:
            a = jnp.where(mask, a, 0.0)
        return jnp.dot(a.astype(BF16), vb_ref[key_rows(j), :], preferred_element_type=F32)

    ratio = SB_QUERIES // SB_KEYS
    r = lax.broadcasted_iota(jnp.int32, (SB_QUERIES, SB_KEYS), 0)
    c = lax.broadcasted_iota(jnp.int32, (SB_QUERIES, SB_KEYS), 1)
    acc = jnp.zeros((SB_QUERIES, HEAD_DIM), F32)
    drop = jnp.zeros((SB_QUERIES, 1), F32)
    for d in reversed(range(ratio)):
        mask = c + d * SB_KEYS < r
        log2_w, dropped = _sb_block_log2(logits(i * ratio + d), tri, mask)
        acc = acc + weighted_values(log2_w - drop, i * ratio + d, mask)
        drop = drop + dropped

    n_old = i * ratio

    def step(n, state):
        z_prev, log2_w_prev, acc, drop = state
        z_new = logits(jnp.maximum(n_old - 1 - n, 0))
        log2_w, dropped = _sb_block_log2(z_prev, tri, None)
        out = weighted_values(log2_w_prev, jnp.clip(n_old + 1 - n, 0, n_old))
        return z_new, log2_w - drop, acc + out, drop + dropped

    idle = jnp.full((SB_QUERIES, SB_KEYS), -1e30, F32)
    _, _, acc, _ = lax.fori_loop(0, jnp.where(n_old > 0, n_old + 2, 0), step, (idle, idle, acc, drop))
    o_ref[...] = _head_rmsnorm(acc, gn_ref[...])


def _sb_prompt(proj, bias, gn):
    t, width = proj.shape
    assert t % SB_QUERIES == 0 and SB_QUERIES % SB_KEYS == 0
    q0, k0, v0 = [(width - n * SB_WIDTH) // HEAD_DIM for n in (3, 2, 1)]
    qspec = pl.BlockSpec((SB_QUERIES, HEAD_DIM), lambda h, i, b: (i, q0 + h))
    kspec = pl.BlockSpec((t, HEAD_DIM), lambda h, i, b: (0, k0 + h))
    vspec = pl.BlockSpec((t, HEAD_DIM), lambda h, i, b: (0, v0 + h))
    return pl.pallas_call(
        _sb_prompt_body,
        grid_spec=pltpu.PrefetchScalarGridSpec(
            num_scalar_prefetch=1,
            grid=(N_SB_HEADS, t // SB_QUERIES),
            in_specs=[qspec, kspec, vspec, pl.BlockSpec((1, HEAD_DIM), lambda h, i, b: (0, h))],
            out_specs=pl.BlockSpec((SB_QUERIES, HEAD_DIM), lambda h, i, b: (i, h)),
            scratch_shapes=[pltpu.VMEM((t, HEAD_DIM), BF16), pltpu.VMEM((t, HEAD_DIM), BF16),
                            pltpu.VMEM((2 * SB_KEYS, SB_KEYS), BF16)],
        ),
        out_shape=jax.ShapeDtypeStruct((t, SB_WIDTH), F32),
        compiler_params=_params("parallel", "arbitrary"),
        name="stick_breaking_prompt",
    )(bias.astype(F32), proj, proj, proj, gn.reshape(1, SB_WIDTH))


def _sb_sample_body(pt_ref, bias_ref, q_ref, kn_ref, vn_ref, *rest):
    kp_refs = rest[:SB_PAGES_PER_STEP]
    vp_refs = rest[SB_PAGES_PER_STEP:2 * SB_PAGES_PER_STEP]
    gn_ref, o_ref, acc_ref, drop_ref = rest[2 * SB_PAGES_PER_STEP:]
    step = pl.program_id(1)
    l = q_ref.shape[1]
    rows = N_SB_HEADS * l
    tri = _suffix_matrix(PAGE_SIZE)
    bias = jnp.concatenate([jnp.full((l, 1), bias_ref[h] * LOG2E, F32) for h in range(N_SB_HEADS)], axis=0)
    qs = [(q_ref[0, :, h * HEAD_DIM:(h + 1) * HEAD_DIM] * (QK_SCALE * LOG2E)).astype(BF16)
          for h in range(N_SB_HEADS)]

    def logits(keys):
        z = [lax.dot_general(qs[h], keys[h], NT_DIMS, preferred_element_type=F32) for h in range(N_SB_HEADS)]
        return jnp.concatenate(z, axis=0) + bias

    def weighted_values(a, values):
        out = [jnp.dot(a[h * l:(h + 1) * l, :].astype(BF16), values[h], preferred_element_type=F32)
               for h in range(N_SB_HEADS)]
        return jnp.concatenate(out, axis=0)

    @pl.when(step == 0)
    def _():
        pad = jnp.zeros((PAGE_SIZE - l, HEAD_DIM), F32)
        kn = [jnp.concatenate([kn_ref[0, :, h * HEAD_DIM:(h + 1) * HEAD_DIM], pad], axis=0).astype(BF16)
              for h in range(N_SB_HEADS)]
        vn = [jnp.concatenate([vn_ref[0, :, h * HEAD_DIM:(h + 1) * HEAD_DIM], pad], axis=0).astype(BF16)
              for h in range(N_SB_HEADS)]
        qi = lax.broadcasted_iota(jnp.int32, (rows, PAGE_SIZE), 0) % l
        kj = lax.broadcasted_iota(jnp.int32, (rows, PAGE_SIZE), 1)
        mask = kj < qi
        log2_w, dropped = _sb_block_log2(logits(kn), tri, mask)
        acc_ref[...] = weighted_values(jnp.where(mask, jnp.exp2(log2_w), 0.0), vn)
        drop_ref[...] = dropped

    head_rows = lambda h: pl.ds(h, PAGE_SIZE, stride=N_SB_HEADS)
    z = jnp.concatenate([logits([kp_ref[head_rows(h), :].astype(BF16) for h in range(N_SB_HEADS)])
                         for kp_ref in kp_refs], axis=0)
    log2_w, dropped = _sb_block_log2(z, tri, None)
    drop = drop_ref[...]
    acc = acc_ref[...]
    for n, vp_ref in enumerate(vp_refs):
        page = slice(n * rows, (n + 1) * rows)
        a = jnp.exp2(log2_w[page, :] - drop)
        acc = acc + weighted_values(a, [vp_ref[head_rows(h), :].astype(BF16) for h in range(N_SB_HEADS)])
        drop = drop + dropped[page, :]
    acc_ref[...] = acc
    drop_ref[...] = drop

    @pl.when(step == pl.num_programs(1) - 1)
    def _():
        for h in range(N_SB_HEADS):
            cols = slice(h * HEAD_DIM, (h + 1) * HEAD_DIM)
            o_ref[0, :, cols] = _head_rmsnorm(acc[h * l:(h + 1) * l, :], gn_ref[:, cols])


def _sb_sample(proj, cache_k, cache_v, page_table, bias, gn):
    b, l, width = proj.shape
    n_pages = page_table.shape[1]
    assert n_pages % SB_PAGES_PER_STEP == 0
    n_phys = cache_k.shape[0]
    page_rows = PAGE_SIZE * N_SB_HEADS
    cache_k = cache_k.reshape(n_phys, page_rows, HEAD_DIM)
    cache_v = cache_v.reshape(n_phys, page_rows, HEAD_DIM)
    segment = lambda j: pl.BlockSpec((1, l, SB_WIDTH), lambda i, s, pt, bs: (i, 0, j))
    last = width // SB_WIDTH - 1

    def page(n):
        return pl.BlockSpec((None, page_rows, HEAD_DIM),
                            lambda i, s, pt, bs: (pt[i, n_pages - 1 - (s * SB_PAGES_PER_STEP + n)], 0, 0))

    pages = [page(n) for n in range(SB_PAGES_PER_STEP)]
    return pl.pallas_call(
        _sb_sample_body,
        grid_spec=pltpu.PrefetchScalarGridSpec(
            num_scalar_prefetch=2,
            grid=(b, n_pages // SB_PAGES_PER_STEP),
            in_specs=[segment(last - 2), segment(last - 1), segment(last)] + pages + pages
                     + [pl.BlockSpec((1, SB_WIDTH), lambda i, s, pt, bs: (0, 0))],
            out_specs=segment(0),
            scratch_shapes=[pltpu.VMEM((N_SB_HEADS * l, HEAD_DIM), F32), pltpu.VMEM((N_SB_HEADS * l, 1), F32)],
        ),
        out_shape=jax.ShapeDtypeStruct((b, l, SB_WIDTH), F32),
        compiler_params=_params("parallel", "arbitrary"),
        name="stick_breaking_sample",
    )(page_table, bias.astype(F32), proj, proj, proj, *([cache_k] * SB_PAGES_PER_STEP),
      *([cache_v] * SB_PAGES_PER_STEP), gn.reshape(1, SB_WIDTH))


def _cross_body(q_ref, k_ref, v_ref, o_ref):
    for h in range(N_MEM_HEADS):
        cols = slice(h * HEAD_DIM, (h + 1) * HEAD_DIM)
        q = q_ref[0, :, cols].astype(BF16)
        k = k_ref[0, :, cols].astype(BF16)
        v = v_ref[0, :, cols].astype(BF16)
        s = lax.dot_general(q, k, NT_DIMS, preferred_element_type=F32) * QK_SCALE
        e = jnp.exp(s - jnp.max(s, axis=-1, keepdims=True))
        p = e / jnp.sum(e, axis=-1, keepdims=True)
        o_ref[0, :, cols] = jnp.dot(p.astype(BF16), v, preferred_element_type=F32)


def _cross_attention(q, mem_k, mem_v, tq):
    b, t, _ = q.shape
    m = mem_k.shape[1]
    tq = min(tq, t)
    qspec = pl.BlockSpec((1, tq, MEM_WIDTH), lambda i, j: (i, j, 0))
    mspec = pl.BlockSpec((1, m, MEM_WIDTH), lambda i, j: (i, 0, 0))
    return pl.pallas_call(
        _cross_body,
        grid=(b, t // tq),
        in_specs=[qspec, mspec, mspec],
        out_specs=qspec,
        out_shape=jax.ShapeDtypeStruct(q.shape, F32),
        compiler_params=_params("parallel", "arbitrary"),
        name="memory_cross_attention",
    )(q, mem_k, mem_v)


def _swiglu(x, wg_b, wu_b, wd_b):
    g = jnp.dot(x, wg_b[...], preferred_element_type=F32)
    u = jnp.dot(x, wu_b[...], preferred_element_type=F32)
    return jnp.dot((_silu(g) * u).astype(BF16), wd_b[...], preferred_element_type=F32)


def _cast_weights(wg_ref, wu_ref, wd_ref, wg_b, wu_b, wd_b):
    wg_b[...] = wg_ref[0].astype(BF16)
    wu_b[...] = wu_ref[0].astype(BF16)
    wd_b[...] = wd_ref[0].astype(BF16)


def _shared_body(x_ref, wg_ref, wu_ref, wd_ref, o_ref, wg_b, wu_b, wd_b):
    @pl.when(pl.program_id(0) == 0)
    def _():
        _cast_weights(wg_ref, wu_ref, wd_ref, wg_b, wu_b, wd_b)

    o_ref[...] = _swiglu(x_ref[...].astype(BF16), wg_b, wu_b, wd_b)


def _shared_mlp(x, w_gate, w_up, w_down, rows):
    r, d = x.shape
    f = w_gate.shape[2]
    wspec = lambda shape: pl.BlockSpec((1,) + shape, lambda i: (0, 0, 0))
    return pl.pallas_call(
        _shared_body,
        grid=(r // rows,),
        in_specs=[pl.BlockSpec((rows, d), lambda i: (i, 0)), wspec((d, f)), wspec((d, f)), wspec((f, d))],
        out_specs=pl.BlockSpec((rows, d), lambda i: (i, 0)),
        out_shape=jax.ShapeDtypeStruct((r, d), F32),
        scratch_shapes=[pltpu.VMEM((d, f), BF16), pltpu.VMEM((d, f), BF16), pltpu.VMEM((f, d), BF16)],
        compiler_params=_params("arbitrary"),
        name="shared_mlp",
    )(x, w_gate, w_up, w_down)


def _expert_body(be_ref, used_ref, tok_ref, x_hbm, wg_ref, wu_ref, wd_ref, o_ref, wg_b, wu_b, wd_b, xbuf, sem):
    b = pl.program_id(0)
    rows = o_ref.shape[0]
    n_used = used_ref[0]

    def start_gather(blk, buf):
        base = blk * rows

        def row(r, carry):
            src = x_hbm.at[pl.ds(tok_ref[base + r], 1), :]
            pltpu.make_async_copy(src, xbuf.at[buf, pl.ds(r, 1), :], sem.at[buf]).start()
            return carry

        lax.fori_loop(0, rows, row, 0, unroll=8)

    @pl.when(b == 0)
    def _():
        start_gather(0, 0)

    @pl.when(b + 1 < n_used)
    def _():
        start_gather(b + 1, (b + 1) % 2)

    @pl.when(b < n_used)
    def _():
        @pl.when(jnp.logical_or(b == 0, be_ref[b] != be_ref[jnp.maximum(b - 1, 0)]))
        def _():
            _cast_weights(wg_ref, wu_ref, wd_ref, wg_b, wu_b, wd_b)

        buf = b % 2
        pltpu.make_async_copy(x_hbm.at[pl.ds(0, rows), :], xbuf.at[buf], sem.at[buf]).wait()
        o_ref[...] = _swiglu(xbuf[buf].astype(BF16), wg_b, wu_b, wd_b)

    @pl.when(b >= n_used)
    def _():
        o_ref[...] = jnp.zeros_like(o_ref)


def _expert_mlp(x, row_token, block_expert, n_used, w_gate, w_up, w_down, rows):
    d = x.shape[1]
    f = w_gate.shape[2]
    n_blocks = block_expert.shape[0]
    wspec = lambda shape: pl.BlockSpec((1,) + shape, lambda i, be, nu, tok: (be[i], 0, 0))
    return pl.pallas_call(
        _expert_body,
        grid_spec=pltpu.PrefetchScalarGridSpec(
            num_scalar_prefetch=3,
            grid=(n_blocks,),
            in_specs=[pl.BlockSpec(memory_space=pl.ANY), wspec((d, f)), wspec((d, f)), wspec((f, d))],
            out_specs=pl.BlockSpec((rows, d), lambda i, be, nu, tok: (i, 0)),
            scratch_shapes=[pltpu.VMEM((d, f), BF16), pltpu.VMEM((d, f), BF16), pltpu.VMEM((f, d), BF16),
                            pltpu.VMEM((2, rows, d), F32), pltpu.SemaphoreType.DMA((2,))],
        ),
        out_shape=jax.ShapeDtypeStruct((n_blocks * rows, d), F32),
        compiler_params=_params("arbitrary"),
        name="expert_mlp",
    )(block_expert, n_used, row_token, x, w_gate, w_up, w_down)


def _combine_body(slot_ref, x_ref, sh_ref, gate_ref, g_ref, yb_hbm, o_ref, buf, sem):
    i = pl.program_id(0)
    tm = x_ref.shape[0]

    def start_gather(tile, b):
        base = tile * (tm * TOP_K)

        def token(r, carry):
            for k in range(TOP_K):
                src = yb_hbm.at[pl.ds(slot_ref[base + r * TOP_K + k], 1), :]
                pltpu.make_async_copy(src, buf.at[b, pl.ds(k * tm + r, 1), :], sem.at[b]).start()
            return carry

        lax.fori_loop(0, tm, token, 0)

    @pl.when(i == 0)
    def _():
        start_gather(0, 0)

    @pl.when(i + 1 < pl.num_programs(0))
    def _():
        start_gather(i + 1, (i + 1) % 2)

    b = i % 2
    pltpu.make_async_copy(yb_hbm.at[pl.ds(0, TOP_K * tm), :], buf.at[b], sem.at[b]).wait()
    routed = buf[b, 0:tm, :] * gate_ref[:, 0:1]
    for k in range(1, TOP_K):
        routed = routed + buf[b, k * tm:(k + 1) * tm, :] * gate_ref[:, k:k + 1]
    x = x_ref[...] + (routed + sh_ref[...])
    o_ref[...] = (x * lax.rsqrt(jnp.mean(x * x, axis=-1, keepdims=True) + RMS_EPS)) * g_ref[...]


def _combine_final_norm(x, yb, slot, gates, shared, gain):
    m, d = x.shape
    tm = COMBINE_ROWS
    assert m % tm == 0 and yb.shape[0] >= TOP_K * tm
    row = pl.BlockSpec((tm, d), lambda i, s: (i, 0))
    return pl.pallas_call(
        _combine_body,
        grid_spec=pltpu.PrefetchScalarGridSpec(
            num_scalar_prefetch=1,
            grid=(m // tm,),
            in_specs=[row, row, pl.BlockSpec((tm, TOP_K), lambda i, s: (i, 0)),
                      pl.BlockSpec((1, d), lambda i, s: (0, 0)), pl.BlockSpec(memory_space=pl.ANY)],
            out_specs=row,
            scratch_shapes=[pltpu.VMEM((2, TOP_K * tm, d), F32), pltpu.SemaphoreType.DMA((2,))],
        ),
        out_shape=jax.ShapeDtypeStruct((m, d), F32),
        compiler_params=_params("arbitrary"),
        name="combine_final_norm",
    )(slot.reshape(-1), x, shared, gates, gain.reshape(1, d), yb)


def _route_body(lg_ref, rb_ref, idx_ref, gate_ref, rank_ref, cnt_ref, seen_ref):
    @pl.when(pl.program_id(0) == 0)
    def _():
        seen_ref[...] = jnp.zeros_like(seen_ref)

    tm, n_exp = lg_ref.shape
    scores = _sigmoid(lg_ref[...])
    biased = scores + rb_ref[...]
    expert = lax.broadcasted_iota(jnp.int32, (tm, n_exp), 1).astype(F32)
    chosen = jnp.zeros((tm, n_exp), F32)
    picks = []
    for _ in range(TOP_K):
        best = jnp.max(biased, axis=-1, keepdims=True)
        first = jnp.min(jnp.where(biased == best, expert, float(n_exp)), axis=-1, keepdims=True)
        hit = expert == first
        picks.append((first, hit, jnp.sum(jnp.where(hit, scores, 0.0), axis=-1, keepdims=True)))
        biased = jnp.where(hit, -jnp.inf, biased)
        chosen = jnp.where(hit, 1.0, chosen)

    r = lax.broadcasted_iota(jnp.int32, (tm, tm), 0)
    c = lax.broadcasted_iota(jnp.int32, (tm, tm), 1)
    earlier = jnp.where(c < r, 1.0, 0.0).astype(BF16)
    rank_all = jnp.dot(earlier, chosen.astype(BF16), preferred_element_type=F32) + seen_ref[...]

    total = picks[0][2]
    for _, _, sel in picks[1:]:
        total = total + sel
    col = lax.broadcasted_iota(jnp.int32, (tm, TOP_K), 1)
    idx = jnp.zeros((tm, TOP_K), F32)
    gate = jnp.zeros((tm, TOP_K), F32)
    rank = jnp.zeros((tm, TOP_K), F32)
    for k, (first, hit, sel) in enumerate(picks):
        idx = jnp.where(col == k, first, idx)
        gate = jnp.where(col == k, sel / total * ROUTED_SCALE, gate)
        rank = jnp.where(col == k, jnp.sum(jnp.where(hit, rank_all, 0.0), axis=-1, keepdims=True), rank)
    idx_ref[...] = idx.astype(jnp.int32)
    gate_ref[...] = gate
    rank_ref[...] = rank.astype(jnp.int32)
    seen_ref[...] += jnp.sum(chosen, axis=0, keepdims=True)
    cnt_ref[...] = seen_ref[...].astype(jnp.int32)


def _route(logits, router_bias, tm=256):
    t, n_exp = logits.shape
    assert t % tm == 0
    row = lambda width: pl.BlockSpec((tm, width), lambda i: (i, 0))
    one = pl.BlockSpec((1, n_exp), lambda i: (0, 0))
    return pl.pallas_call(
        _route_body,
        grid=(t // tm,),
        in_specs=[row(n_exp), one],
        out_specs=[row(TOP_K), row(TOP_K), row(TOP_K), one],
        out_shape=[jax.ShapeDtypeStruct((t, TOP_K), jnp.int32), jax.ShapeDtypeStruct((t, TOP_K), F32),
                   jax.ShapeDtypeStruct((t, TOP_K), jnp.int32), jax.ShapeDtypeStruct((1, n_exp), jnp.int32)],
        scratch_shapes=[pltpu.VMEM((1, n_exp), F32)],
        compiler_params=_params("arbitrary"),
        name="router_topk",
    )(logits, router_bias.reshape(1, n_exp).astype(F32))


def _moe(hn, logits, router_bias, w_gate, w_up, w_down, ws_gate, ws_up, ws_down):
    t, d = hn.shape
    idx, gates, rank, counts = _route(logits, router_bias)

    padded = (counts[0] + EXPERT_ROWS - 1) // EXPERT_ROWS * EXPERT_ROWS
    pad_end = jnp.cumsum(padded)
    pad_start = pad_end - padded
    slot = pad_start[idx] + rank
    n_blocks = -(-(t * TOP_K) // EXPERT_ROWS) + N_EXPERTS
    slot_tok = jnp.zeros((n_blocks * EXPERT_ROWS,), jnp.int32).at[slot.reshape(-1)].set(
        jnp.repeat(jnp.arange(t, dtype=jnp.int32), TOP_K))
    block_e = jnp.minimum(jnp.searchsorted(pad_end, jnp.arange(n_blocks) * EXPERT_ROWS, side='right'),
                          N_EXPERTS - 1).astype(jnp.int32)
    n_used = (pad_end[-1:] // EXPERT_ROWS).astype(jnp.int32)
    yb = _expert_mlp(hn, slot_tok, block_e, n_used, w_gate, w_up, w_down, EXPERT_ROWS)
    shared = _shared_mlp(hn, ws_gate[None], ws_up[None], ws_down[None], EXPERT_ROWS)
    return yb, slot.astype(jnp.int32), gates, shared


def kernel(x_prompt, x_sample, mem_prompt, cache_sb_k, cache_sb_v, page_table, state_ret, cache_mem_k, cache_mem_v,
           norm_mix, w_in, ret_gn, sb_gn, sb_bias, w_out, norm_mem, w_ck, w_cv, norm_cross, w_cq, w_co, norm_ffn,
           router_w, router_bias, w_gate, w_up, w_down, ws_gate, ws_up, ws_down, norm_final):
    depth = w_in.shape[0]
    assert depth == 1 and x_prompt.shape[0] == 1
    _, t, d = x_prompt.shape
    b, l, _ = x_sample.shape
    n_pages = page_table.shape[1]
    lyr = 0
    xp = x_prompt.reshape(t, d)
    xs = x_sample.reshape(b * l, d)
    pos_p = jnp.arange(t)
    pos_s = n_pages * PAGE_SIZE + jnp.arange(l)

    w_in_b = w_in[lyr].astype(BF16)
    w_out_b = w_out[lyr].astype(BF16)
    w_cq_b = w_cq[lyr].astype(BF16)
    w_co_b = w_co[lyr].astype(BF16)

    mem = mem_prompt.reshape(-1, d)
    n_mem = mem.shape[0]
    mkv = _mm(mem, jnp.concatenate([w_ck[lyr], w_cv[lyr]], axis=1).astype(BF16), gain=norm_mem[lyr])
    mk_p, mv_p = mkv[:, :MEM_WIDTH], mkv[:, MEM_WIDTH:]

    proj = _mm(xp, w_in_b, gain=norm_mix[lyr])
    sk_p, sv_p = proj[:, -2 * SB_WIDTH:-SB_WIDTH], proj[:, -SB_WIDTH:]
    o_ret, s_ret_p = _retention_prompt(proj, pos_p, ret_gn[lyr])
    o_sb = _sb_prompt(proj, sb_bias[lyr], sb_gn[lyr])
    xp = _mm(jnp.concatenate([o_ret, o_sb], axis=-1), w_out_b, res=xp)

    proj = _mm(xs, w_in_b, gain=norm_mix[lyr]).reshape(b, l, -1)
    sk_s, sv_s = proj[:, :, -2 * SB_WIDTH:-SB_WIDTH], proj[:, :, -SB_WIDTH:]
    o_ret, s_ret_s = _retention_sample(proj, state_ret[lyr], pos_s, ret_gn[lyr])
    o_sb = _sb_sample(proj, cache_sb_k[lyr], cache_sb_v[lyr], page_table, sb_bias[lyr], sb_gn[lyr])
    xs = _mm(jnp.concatenate([o_ret, o_sb], axis=-1).reshape(b * l, -1), w_out_b, res=xs)

    q = _mm(xp, w_cq_b, gain=norm_cross[lyr])
    o = _cross_attention(q[None], mk_p[None], mv_p[None], tq=512)
    xp = _mm(o[0], w_co_b, res=xp)
    q = _mm(xs, w_cq_b, gain=norm_cross[lyr]).reshape(b, l, MEM_WIDTH)
    o = _cross_attention(q, cache_mem_k[lyr].reshape(b, n_mem, MEM_WIDTH),
                         cache_mem_v[lyr].reshape(b, n_mem, MEM_WIDTH), tq=l)
    xs = _mm(o.reshape(b * l, MEM_WIDTH), w_co_b, res=xs)

    x_all = jnp.concatenate([xp, xs], axis=0)
    logits, hn = _mm(x_all, router_w[lyr].astype(BF16), gain=norm_ffn[lyr], emit_xn=True)
    yb, slot, gates, shared = _moe(hn, logits, router_bias[lyr], w_gate[lyr], w_up[lyr], w_down[lyr],
                                   ws_gate[lyr], ws_up[lyr], ws_down[lyr])
    y_all = _combine_final_norm(x_all, yb, slot, gates, shared, norm_final)

    y_prompt = y_all[:t].reshape(1, t, d)
    y_sample = y_all[t:].reshape(b, l, d)
    return (y_prompt, y_sample,
            s_ret_p[None, None],
            sk_p.reshape(1, 1, t, N_SB_HEADS, HEAD_DIM), sv_p.reshape(1, 1, t, N_SB_HEADS, HEAD_DIM),
            mk_p.reshape(1, 1, n_mem, N_MEM_HEADS, HEAD_DIM), mv_p.reshape(1, 1, n_mem, N_MEM_HEADS, HEAD_DIM),
            s_ret_s[None],
            sk_s.reshape(1, b, l, N_SB_HEADS, HEAD_DIM), sv_s.reshape(1, b, l, N_SB_HEADS, HEAD_DIM))
```

```python
import functools

import jax
import jax.numpy as jnp
from jax import lax
from jax.experimental import pallas as pl
from jax.experimental.pallas import tpu as pltpu

F32 = jnp.float32
BF16 = jnp.bfloat16

HEAD_DIM = 128
N_RET_HEADS = 8
N_SB_HEADS = 8
RET_WIDTH = N_RET_HEADS * HEAD_DIM
SB_WIDTH = N_SB_HEADS * HEAD_DIM
RET_CHUNK = 128
PAGE_SIZE = 128
ROPE_BASE = 10000.0
N_MEM_HEADS = 4
MEM_WIDTH = N_MEM_HEADS * HEAD_DIM
N_EXPERTS = 64
TOP_K = 8
ROUTED_SCALE = 2.5
RMS_EPS = 1e-6
NORM_EPS = 1e-6
QK_SCALE = HEAD_DIM ** -0.5

VMEM_LIMIT_BYTES = 56 * 1024 * 1024
SB_KEYS = 256
SB_QUERIES = 512
SB_PAGES_PER_STEP = 8
EXPERT_ROWS = 256
COMBINE_ROWS = 128
SB_CHUNK = 128
LOG2E = 1.4426950408889634

NT_DIMS = (((1,), (1,)), ((), ()))
TN_DIMS = (((0,), (0,)), ((), ()))


def _params(*sem):
    return pltpu.CompilerParams(dimension_semantics=sem, vmem_limit_bytes=VMEM_LIMIT_BYTES)


def _sigmoid(x):
    return 1.0 / (1.0 + jnp.exp(-x))


def _silu(x):
    return x * _sigmoid(x)


def _mm_body(*refs, has_norm, has_res, emit_xn):
    it = iter(refs)
    x_ref = next(it)
    g_ref = next(it) if has_norm else None
    w_ref = next(it)
    r_ref = next(it) if has_res else None
    o_ref = next(it)
    xo_ref = next(it) if emit_xn else None
    xn_ref = next(it)

    @pl.when(pl.program_id(1) == 0)
    def _():
        x = x_ref[...].astype(F32)
        if has_norm:
            x = x * lax.rsqrt(jnp.mean(x * x, axis=-1, keepdims=True) + RMS_EPS)
            x = x * g_ref[...]
        xb = x.astype(BF16)
        xn_ref[...] = xb
        if emit_xn:
            xo_ref[...] = x

    acc = jnp.dot(xn_ref[...], w_ref[...], preferred_element_type=F32)
    if has_res:
        acc = r_ref[...] + acc
    o_ref[...] = acc


def _mm(x, w, gain=None, res=None, emit_xn=False, tm=512, tn=512):
    m, k = x.shape
    n = w.shape[1]
    tm = min(tm, m)
    tn = min(tn, n)
    assert m % tm == 0 and n % tn == 0
    in_specs = [pl.BlockSpec((tm, k), lambda i, j: (i, 0))]
    args = [x]
    if gain is not None:
        in_specs.append(pl.BlockSpec((1, k), lambda i, j: (0, 0)))
        args.append(gain.reshape(1, k).astype(F32))
    in_specs.append(pl.BlockSpec((k, tn), lambda i, j: (0, j)))
    args.append(w)
    if res is not None:
        in_specs.append(pl.BlockSpec((tm, tn), lambda i, j: (i, j)))
        args.append(res)
    out_shape = [jax.ShapeDtypeStruct((m, n), F32)]
    out_specs = [pl.BlockSpec((tm, tn), lambda i, j: (i, j))]
    if emit_xn:
        out_shape.append(jax.ShapeDtypeStruct((m, k), F32))
        out_specs.append(pl.BlockSpec((tm, k), lambda i, j: (i, 0)))
    outs = pl.pallas_call(
        functools.partial(_mm_body, has_norm=gain is not None, has_res=res is not None, emit_xn=emit_xn),
        grid=(m // tm, n // tn),
        in_specs=in_specs,
        out_specs=out_specs,
        out_shape=out_shape,
        scratch_shapes=[pltpu.VMEM((tm, k), BF16)],
        compiler_params=_params("parallel", "arbitrary"),
        name="norm_matmul",
    )(*args)
    return outs if emit_xn else outs[0]


def _rotary_tables(pos):
    half = HEAD_DIM // 2
    inv_freq = ROPE_BASE ** (-jnp.arange(half, dtype=F32) / half)
    ang = pos.astype(F32)[:, None] * inv_freq[None, :]
    cos, sin = jnp.cos(ang), jnp.sin(ang)
    return jnp.concatenate([cos, cos], axis=-1), jnp.concatenate([-sin, sin], axis=-1)


def _retention_tables(length):
    log_g = jnp.log1p(-jnp.exp2(-5.0 - jnp.arange(N_RET_HEADS, dtype=F32)))
    i = jnp.arange(length, dtype=F32)
    diff = i[:, None] - i[None, :]
    decay = jnp.where(diff >= 0, jnp.exp(jnp.maximum(diff, 0.0)[None] * log_g[:, None, None]), 0.0)
    dq = jnp.exp((i[None, :] + 1.0) * log_g[:, None])
    dk = jnp.exp((length - 1.0 - i)[None, :] * log_g[:, None])
    ds = jnp.exp(length * log_g)
    lanes = (N_RET_HEADS, length, HEAD_DIM)
    return (decay, jnp.broadcast_to(dq[:, :, None], lanes), jnp.broadcast_to(dk[:, :, None], lanes),
            jnp.broadcast_to(ds[:, None, None], (N_RET_HEADS, 8, HEAD_DIM)))


def _rotate(x, cos, sin_signed):
    return x * cos + pltpu.roll(x, HEAD_DIM // 2, 1) * sin_signed


def _retention_step(q, k, v, s, decay, dq, dk, ds):
    qb, kb, vb = q.astype(BF16), k.astype(BF16), v.astype(BF16)
    scores = lax.dot_general(qb, kb, NT_DIMS, preferred_element_type=F32) * decay
    o = jnp.dot(scores.astype(BF16), vb, preferred_element_type=F32)
    o = o + jnp.dot((q * dq).astype(BF16), s.astype(BF16), preferred_element_type=F32)
    kd = (k * dk).astype(BF16)
    s_new = ds * s + lax.dot_general(kd, vb, TN_DIMS, preferred_element_type=F32)
    return o, s_new


def _gated_layernorm(o, gn, gate):
    c = o - jnp.mean(o, axis=-1, keepdims=True)
    y = c * lax.rsqrt(jnp.mean(c * c, axis=-1, keepdims=True) + NORM_EPS)
    return (y * gn) * _silu(gate)


def _ret_prompt_body(q_ref, k_ref, v_ref, g_ref, cos_ref, sin_ref, decay_ref, dq_ref, dk_ref, ds_ref, gn_ref,
                     o_ref, sfin_ref, s_ref):
    c = pl.program_id(0)

    @pl.when(c == 0)
    def _():
        s_ref[...] = jnp.zeros_like(s_ref)

    cos, sin = cos_ref[...], sin_ref[...]
    for h in range(N_RET_HEADS):
        cols = slice(h * HEAD_DIM, (h + 1) * HEAD_DIM)
        q = _rotate(q_ref[:, cols], cos, sin)
        k = _rotate(k_ref[:, cols], cos, sin) * QK_SCALE
        o, s_new = _retention_step(q, k, v_ref[:, cols], s_ref[h], decay_ref[h], dq_ref[h], dk_ref[h],
                                   ds_ref[h, 0:1, :])
        s_ref[h] = s_new
        o_ref[:, cols] = _gated_layernorm(o, gn_ref[:, cols], g_ref[:, cols])

    @pl.when(c == pl.num_programs(0) - 1)
    def _():
        sfin_ref[...] = s_ref[...]


def _retention_prompt(proj, pos, gn):
    t = proj.shape[0]
    cos, sin = _rotary_tables(pos)
    decay, dq, dk, ds = _retention_tables(RET_CHUNK)
    segment = lambda j: pl.BlockSpec((RET_CHUNK, RET_WIDTH), lambda c: (c, j))
    rot = pl.BlockSpec((RET_CHUNK, HEAD_DIM), lambda c: (c, 0))
    full = lambda a: pl.BlockSpec(a.shape, lambda c: (0,) * a.ndim)
    gn2 = gn.reshape(1, RET_WIDTH)
    state = jax.ShapeDtypeStruct((N_RET_HEADS, HEAD_DIM, HEAD_DIM), F32)
    return pl.pallas_call(
        _ret_prompt_body,
        grid=(t // RET_CHUNK,),
        in_specs=[segment(0), segment(1), segment(2), segment(3), rot, rot, full(decay), full(dq), full(dk), full(ds),
                  full(gn2)],
        out_specs=[segment(0), pl.BlockSpec(state.shape, lambda c: (0, 0, 0))],
        out_shape=[jax.ShapeDtypeStruct((t, RET_WIDTH), F32), state],
        scratch_shapes=[pltpu.VMEM(state.shape, F32)],
        compiler_params=_params("arbitrary"),
        name="retention_prompt",
    )(proj, proj, proj, proj, cos, sin, decay, dq, dk, ds, gn2)


def _ret_sample_body(q_ref, k_ref, v_ref, g_ref, st_ref, cos_ref, sin_ref, decay_ref, dq_ref, dk_ref, ds_ref, gn_ref,
                     o_ref, snew_ref):
    cos, sin = cos_ref[...], sin_ref[...]
    for h in range(N_RET_HEADS):
        cols = slice(h * HEAD_DIM, (h + 1) * HEAD_DIM)
        q = _rotate(q_ref[0, :, cols], cos, sin)
        k = _rotate(k_ref[0, :, cols], cos, sin) * QK_SCALE
        o, s_new = _retention_step(q, k, v_ref[0, :, cols], st_ref[0, h], decay_ref[h], dq_ref[h], dk_ref[h],
                                   ds_ref[h, 0:1, :])
        snew_ref[0, h] = s_new
        o_ref[0, :, cols] = _gated_layernorm(o, gn_ref[:, cols], g_ref[0, :, cols])


def _retention_sample(proj, state, pos, gn):
    b, l, _ = proj.shape
    cos, sin = _rotary_tables(pos)
    decay, dq, dk, ds = _retention_tables(l)
    segment = lambda j: pl.BlockSpec((1, l, RET_WIDTH), lambda i: (i, 0, j))
    st = pl.BlockSpec((1, N_RET_HEADS, HEAD_DIM, HEAD_DIM), lambda i: (i, 0, 0, 0))
    full = lambda a: pl.BlockSpec(a.shape, lambda i: (0,) * a.ndim)
    gn2 = gn.reshape(1, RET_WIDTH)
    return pl.pallas_call(
        _ret_sample_body,
        grid=(b,),
        in_specs=[segment(0), segment(1), segment(2), segment(3), st, full(cos), full(sin), full(decay), full(dq),
                  full(dk), full(ds), full(gn2)],
        out_specs=[segment(0), st],
        out_shape=[jax.ShapeDtypeStruct((b, l, RET_WIDTH), F32), jax.ShapeDtypeStruct(state.shape, F32)],
        compiler_params=_params("parallel"),
        name="retention_sample",
    )(proj, proj, proj, proj, state, cos, sin, decay, dq, dk, ds, gn2)


def _suffix_matrix(n):
    j = lax.broadcasted_iota(jnp.int32, (2 * n, n), 0) % n
    s = lax.broadcasted_iota(jnp.int32, (2 * n, n), 1)
    return jnp.where(j > s, 1.0, 0.0).astype(BF16)


def _sb_block_log2(z2, tri, mask):
    rows, keys = z2.shape
    chunk = min(rows, SB_CHUNK)
    suffix = tri.shape[1]
    log2_w, dropped = [], []
    for c in range(0, rows, chunk):
        parts, later = [], None
        for s in reversed(range(0, keys, suffix)):
            zc = z2[c:c + chunk, s:s + suffix]
            neg_log_keep = jnp.maximum(zc, 0.0) + jnp.log2(1.0 + jnp.exp2(-jnp.abs(zc)))
            log_beta = zc - neg_log_keep
            if mask is not None:
                neg_log_keep = jnp.where(mask[c:c + chunk, s:s + suffix], neg_log_keep, 0.0)
            hi = neg_log_keep.astype(BF16)
            lo = (neg_log_keep - hi.astype(F32)).astype(BF16)
            between = jnp.dot(jnp.concatenate([hi, lo], axis=1), tri, preferred_element_type=F32)
            total = jnp.sum(neg_log_keep, axis=-1, keepdims=True)
            if later is None:
                parts.append(log_beta - between)
                later = total
            else:
                parts.append(log_beta - between - later)
                later = later + total
        log2_w.append(jnp.concatenate(parts[::-1], axis=1))
        dropped.append(later)
    return jnp.concatenate(log2_w, axis=0), jnp.concatenate(dropped, axis=0)


def _head_rmsnorm(o, gn):
    return (o * lax.rsqrt(jnp.mean(o * o, axis=-1, keepdims=True) + RMS_EPS)) * gn


def _sb_prompt_body(bias_ref, q_ref, k_ref, v_ref, gn_ref, o_ref, kb_ref, vb_ref, tri_ref):
    h = pl.program_id(0)
    i = pl.program_id(1)

    @pl.when(i == 0)
    def _():
        kb_ref[...] = k_ref[...].astype(BF16)
        vb_ref[...] = v_ref[...].astype(BF16)
        tri_ref[...] = _suffix_matrix(SB_KEYS)

    q = (q_ref[...] * (QK_SCALE * LOG2E)).astype(BF16)
    bias = bias_ref[h] * LOG2E
    tri = tri_ref[...]

    def key_rows(j):
        return pl.ds(pl.multiple_of(j * SB_KEYS, SB_KEYS), SB_KEYS)

    def logits(j):
        return lax.dot_general(q, kb_ref[key_rows(j), :], NT_DIMS, preferred_element_type=F32) + bias

    def weighted_values(log2_w, j, mask=None):
        a = jnp.exp2(log2_w)
        if mask is not None:
            a = jnp.where(mask, a, 0.0)
        return jnp.dot(a.astype(BF16), vb_ref[key_rows(j), :], preferred_element_type=F32)

    ratio = SB_QUERIES // SB_KEYS
    r = lax.broadcasted_iota(jnp.int32, (SB_QUERIES, SB_KEYS), 0)
    c = lax.broadcasted_iota(jnp.int32, (SB_QUERIES, SB_KEYS), 1)
    acc = jnp.zeros((SB_QUERIES, HEAD_DIM), F32)
    drop = jnp.zeros((SB_QUERIES, 1), F32)
    for d in reversed(range(ratio)):
        mask = c + d * SB_KEYS < r
        log2_w, dropped = _sb_block_log2(logits(i * ratio + d), tri, mask)
        acc = acc + weighted_values(log2_w - drop, i * ratio + d, mask)
        drop = drop + dropped

    n_old = i * ratio

    def step(n, state):
        z_prev, log2_w_prev, acc, drop = state
        z_new = logits(jnp.maximum(n_old - 1 - n, 0))
        log2_w, dropped = _sb_block_log2(z_prev, tri, None)
        out = weighted_values(log2_w_prev, jnp.clip(n_old + 1 - n, 0, n_old))
        return z_new, log2_w - drop, acc + out, drop + dropped

    idle = jnp.full((SB_QUERIES, SB_KEYS), -1e30, F32)
    _, _, acc, _ = lax.fori_loop(0, jnp.where(n_old > 0, n_old + 2, 0), step, (idle, idle, acc, drop))
    o_ref[...] = _head_rmsnorm(acc, gn_ref[...])


def _sb_prompt(proj, bias, gn):
    t, width = proj.shape
    assert t % SB_QUERIES == 0 and SB_QUERIES % SB_KEYS == 0
    q0, k0, v0 = [(width - n * SB_WIDTH) // HEAD_DIM for n in (3, 2, 1)]
    qspec = pl.BlockSpec((SB_QUERIES, HEAD_DIM), lambda h, i, b: (i, q0 + h))
    kspec = pl.BlockSpec((t, HEAD_DIM), lambda h, i, b: (0, k0 + h))
    vspec = pl.BlockSpec((t, HEAD_DIM), lambda h, i, b: (0, v0 + h))
    return pl.pallas_call(
        _sb_prompt_body,
        grid_spec=pltpu.PrefetchScalarGridSpec(
            num_scalar_prefetch=1,
            grid=(N_SB_HEADS, t // SB_QUERIES),
            in_specs=[qspec, kspec, vspec, pl.BlockSpec((1, HEAD_DIM), lambda h, i, b: (0, h))],
            out_specs=pl.BlockSpec((SB_QUERIES, HEAD_DIM), lambda h, i, b: (i, h)),
            scratch_shapes=[pltpu.VMEM((t, HEAD_DIM), BF16), pltpu.VMEM((t, HEAD_DIM), BF16),
                            pltpu.VMEM((2 * SB_KEYS, SB_KEYS), BF16)],
        ),
        out_shape=jax.ShapeDtypeStruct((t, SB_WIDTH), F32),
        compiler_params=_params("parallel", "arbitrary"),
        name="stick_breaking_prompt",
    )(bias.astype(F32), proj, proj, proj, gn.reshape(1, SB_WIDTH))


def _sb_sample_body(pt_ref, bias_ref, q_ref, kn_ref, vn_ref, *rest):
    kp_refs = rest[:SB_PAGES_PER_STEP]
    vp_refs = rest[SB_PAGES_PER_STEP:2 * SB_PAGES_PER_STEP]
    gn_ref, o_ref, acc_ref, drop_ref = rest[2 * SB_PAGES_PER_STEP:]
    step = pl.program_id(1)
    l = q_ref.shape[1]
    rows = N_SB_HEADS * l
    tri = _suffix_matrix(PAGE_SIZE)
    bias = jnp.concatenate([jnp.full((l, 1), bias_ref[h] * LOG2E, F32) for h in range(N_SB_HEADS)], axis=0)
    qs = [(q_ref[0, :, h * HEAD_DIM:(h + 1) * HEAD_DIM] * (QK_SCALE * LOG2E)).astype(BF16)
          for h in range(N_SB_HEADS)]

    def logits(keys):
        z = [lax.dot_general(qs[h], keys[h], NT_DIMS, preferred_element_type=F32) for h in range(N_SB_HEADS)]
        return jnp.concatenate(z, axis=0) + bias

    def weighted_values(a, values):
        out = [jnp.dot(a[h * l:(h + 1) * l, :].astype(BF16), values[h], preferred_element_type=F32)
               for h in range(N_SB_HEADS)]
        return jnp.concatenate(out, axis=0)

    @pl.when(step == 0)
    def _():
        pad = jnp.zeros((PAGE_SIZE - l, HEAD_DIM), F32)
        kn = [jnp.concatenate([kn_ref[0, :, h * HEAD_DIM:(h + 1) * HEAD_DIM], pad], axis=0).astype(BF16)
              for h in range(N_SB_HEADS)]
        vn = [jnp.concatenate([vn_ref[0, :, h * HEAD_DIM:(h + 1) * HEAD_DIM], pad], axis=0).astype(BF16)
              for h in range(N_SB_HEADS)]
        qi = lax.broadcasted_iota(jnp.int32, (rows, PAGE_SIZE), 0) % l
        kj = lax.broadcasted_iota(jnp.int32, (rows, PAGE_SIZE), 1)
        mask = kj < qi
        log2_w, dropped = _sb_block_log2(logits(kn), tri, mask)
        acc_ref[...] = weighted_values(jnp.where(mask, jnp.exp2(log2_w), 0.0), vn)
        drop_ref[...] = dropped

    head_rows = lambda h: pl.ds(h, PAGE_SIZE, stride=N_SB_HEADS)
    z = jnp.concatenate([logits([kp_ref[head_rows(h), :].astype(BF16) for h in range(N_SB_HEADS)])
                         for kp_ref in kp_refs], axis=0)
    log2_w, dropped = _sb_block_log2(z, tri, None)
    drop = drop_ref[...]
    acc = acc_ref[...]
    for n, vp_ref in enumerate(vp_refs):
        page = slice(n * rows, (n + 1) * rows)
        a = jnp.exp2(log2_w[page, :] - drop)
        acc = acc + weighted_values(a, [vp_ref[head_rows(h), :].astype(BF16) for h in range(N_SB_HEADS)])
        drop = drop + dropped[page, :]
    acc_ref[...] = acc
    drop_ref[...] = drop

    @pl.when(step == pl.num_programs(1) - 1)
    def _():
        for h in range(N_SB_HEADS):
            cols = slice(h * HEAD_DIM, (h + 1) * HEAD_DIM)
            o_ref[0, :, cols] = _head_rmsnorm(acc[h * l:(h + 1) * l, :], gn_ref[:, cols])


def _sb_sample(proj, cache_k, cache_v, page_table, bias, gn):
    b, l, width = proj.shape
    n_pages = page_table.shape[1]
    assert n_pages % SB_PAGES_PER_STEP == 0
    n_phys = cache_k.shape[0]
    page_rows = PAGE_SIZE * N_SB_HEADS
    cache_k = cache_k.reshape(n_phys, page_rows, HEAD_DIM)
    cache_v = cache_v.reshape(n_phys, page_rows, HEAD_DIM)
    segment = lambda j: pl.BlockSpec((1, l, SB_WIDTH), lambda i, s, pt, bs: (i, 0, j))
    last = width // SB_WIDTH - 1

    def page(n):
        return pl.BlockSpec((None, page_rows, HEAD_DIM),
                            lambda i, s, pt, bs: (pt[i, n_pages - 1 - (s * SB_PAGES_PER_STEP + n)], 0, 0))

    pages = [page(n) for n in range(SB_PAGES_PER_STEP)]
    return pl.pallas_call(
        _sb_sample_body,
        grid_spec=pltpu.PrefetchScalarGridSpec(
            num_scalar_prefetch=2,
            grid=(b, n_pages // SB_PAGES_PER_STEP),
            in_specs=[segment(last - 2), segment(last - 1), segment(last)] + pages + pages
                     + [pl.BlockSpec((1, SB_WIDTH), lambda i, s, pt, bs: (0, 0))],
            out_specs=segment(0),
            scratch_shapes=[pltpu.VMEM((N_SB_HEADS * l, HEAD_DIM), F32), pltpu.VMEM((N_SB_HEADS * l, 1), F32)],
        ),
        out_shape=jax.ShapeDtypeStruct((b, l, SB_WIDTH), F32),
        compiler_params=_params("parallel", "arbitrary"),
        name="stick_breaking_sample",
    )(page_table, bias.astype(F32), proj, proj, proj, *([cache_k] * SB_PAGES_PER_STEP),
      *([cache_v] * SB_PAGES_PER_STEP), gn.reshape(1, SB_WIDTH))


def _cross_body(q_ref, k_ref, v_ref, o_ref):
    for h in range(N_MEM_HEADS):
        cols = slice(h * HEAD_DIM, (h + 1) * HEAD_DIM)
        q = q_ref[0, :, cols].astype(BF16)
        k = k_ref[0, :, cols].astype(BF16)
        v = v_ref[0, :, cols].astype(BF16)
        s = lax.dot_general(q, k, NT_DIMS, preferred_element_type=F32) * QK_SCALE
        e = jnp.exp(s - jnp.max(s, axis=-1, keepdims=True))
        p = e / jnp.sum(e, axis=-1, keepdims=True)
        o_ref[0, :, cols] = jnp.dot(p.astype(BF16), v, preferred_element_type=F32)


def _cross_attention(q, mem_k, mem_v, tq):
    b, t, _ = q.shape
    m = mem_k.shape[1]
    tq = min(tq, t)
    qspec = pl.BlockSpec((1, tq, MEM_WIDTH), lambda i, j: (i, j, 0))
    mspec = pl.BlockSpec((1, m, MEM_WIDTH), lambda i, j: (i, 0, 0))
    return pl.pallas_call(
        _cross_body,
        grid=(b, t // tq),
        in_specs=[qspec, mspec, mspec],
        out_specs=qspec,
        out_shape=jax.ShapeDtypeStruct(q.shape, F32),
        compiler_params=_params("parallel", "arbitrary"),
        name="memory_cross_attention",
    )(q, mem_k, mem_v)


def _swiglu(x, wg_b, wu_b, wd_b):
    g = jnp.dot(x, wg_b[...], preferred_element_type=F32)
    u = jnp.dot(x, wu_b[...], preferred_element_type=F32)
    return jnp.dot((_silu(g) * u).astype(BF16), wd_b[...], preferred_element_type=F32)


def _cast_weights(wg_ref, wu_ref, wd_ref, wg_b, wu_b, wd_b):
    wg_b[...] = wg_ref[0].astype(BF16)
    wu_b[...] = wu_ref[0].astype(BF16)
    wd_b[...] = wd_ref[0].astype(BF16)


def _shared_body(x_ref, wg_ref, wu_ref, wd_ref, o_ref, wg_b, wu_b, wd_b):
    @pl.when(pl.program_id(0) == 0)
    def _():
        _cast_weights(wg_ref, wu_ref, wd_ref, wg_b, wu_b, wd_b)

    o_ref[...] = _swiglu(x_ref[...].astype(BF16), wg_b, wu_b, wd_b)


def _shared_mlp(x, w_gate, w_up, w_down, rows):
    r, d = x.shape
    f = w_gate.shape[2]
    wspec = lambda shape: pl.BlockSpec((1,) + shape, lambda i: (0, 0, 0))
    return pl.pallas_call(
        _shared_body,
        grid=(r // rows,),
        in_specs=[pl.BlockSpec((rows, d), lambda i: (i, 0)), wspec((d, f)), wspec((d, f)), wspec((f, d))],
        out_specs=pl.BlockSpec((rows, d), lambda i: (i, 0)),
        out_shape=jax.ShapeDtypeStruct((r, d), F32),
        scratch_shapes=[pltpu.VMEM((d, f), BF16), pltpu.VMEM((d, f), BF16), pltpu.VMEM((f, d), BF16)],
        compiler_params=_params("arbitrary"),
        name="shared_mlp",
    )(x, w_gate, w_up, w_down)


def _expert_body(be_ref, used_ref, tok_ref, x_hbm, wg_ref, wu_ref, wd_ref, o_ref, wg_b, wu_b, wd_b, xbuf0, xbuf1, sem):
    b = pl.program_id(0)
    rows = o_ref.shape[0]
    n_used = used_ref[0]
    bufs = (xbuf0, xbuf1)

    def wait_gather(parity):
        pltpu.make_async_copy(x_hbm.at[pl.ds(0, rows), :], bufs[parity], sem.at[parity]).wait()

    @pl.when(b == 0)
    def _():
        def row(r, carry):
            pltpu.make_async_copy(x_hbm.at[pl.ds(tok_ref[r], 1), :], xbuf0.at[pl.ds(r, 1), :], sem.at[0]).start()
            return carry

        lax.fori_loop(0, rows, row, 0, unroll=8)

    @pl.when(jnp.logical_and(b < n_used, jnp.logical_or(b == 0, be_ref[b] != be_ref[jnp.maximum(b - 1, 0)])))
    def _():
        _cast_weights(wg_ref, wu_ref, wd_ref, wg_b, wu_b, wd_b)

    def run(parity):
        nxt = jnp.minimum(b + 1, n_used - 1) * rows
        wait_gather(parity)
        x = bufs[parity][...].astype(BF16)
        for r in range(rows):
            src = x_hbm.at[pl.ds(tok_ref[nxt + r], 1), :]
            pltpu.make_async_copy(src, bufs[1 - parity].at[pl.ds(r, 1), :], sem.at[1 - parity]).start()
        o_ref[...] = _swiglu(x, wg_b, wu_b, wd_b)

        @pl.when(b == n_used - 1)
        def _():
            wait_gather(1 - parity)

    for parity in (0, 1):
        pl.when(jnp.logical_and(b < n_used, b % 2 == parity))(functools.partial(run, parity))

    @pl.when(b >= n_used)
    def _():
        o_ref[...] = jnp.zeros_like(o_ref)


def _expert_mlp(x, row_token, block_expert, n_used, w_gate, w_up, w_down, rows):
    d = x.shape[1]
    f = w_gate.shape[2]
    n_blocks = block_expert.shape[0]
    wspec = lambda shape: pl.BlockSpec((1,) + shape, lambda i, be, nu, tok: (be[i], 0, 0))
    return pl.pallas_call(
        _expert_body,
        grid_spec=pltpu.PrefetchScalarGridSpec(
            num_scalar_prefetch=3,
            grid=(n_blocks,),
            in_specs=[pl.BlockSpec(memory_space=pl.ANY), wspec((d, f)), wspec((d, f)), wspec((f, d))],
            out_specs=pl.BlockSpec((rows, d), lambda i, be, nu, tok: (i, 0)),
            scratch_shapes=[pltpu.VMEM((d, f), BF16), pltpu.VMEM((d, f), BF16), pltpu.VMEM((f, d), BF16),
                            pltpu.VMEM((rows, d), F32), pltpu.VMEM((rows, d), F32), pltpu.SemaphoreType.DMA((2,))],
        ),
        out_shape=jax.ShapeDtypeStruct((n_blocks * rows, d), F32),
        compiler_params=_params("arbitrary"),
        name="expert_mlp",
    )(block_expert, n_used, row_token, x, w_gate, w_up, w_down)


def _combine_body(slot_ref, x_ref, sh_ref, gate_ref, g_ref, yb_hbm, o_ref, buf, sem):
    i = pl.program_id(0)
    tm = x_ref.shape[0]

    def start_gather(tile, b):
        base = tile * (tm * TOP_K)

        def token(r, carry):
            for k in range(TOP_K):
                src = yb_hbm.at[pl.ds(slot_ref[base + r * TOP_K + k], 1), :]
                pltpu.make_async_copy(src, buf.at[b, pl.ds(k * tm + r, 1), :], sem.at[b]).start()
            return carry

        lax.fori_loop(0, tm, token, 0)

    @pl.when(i == 0)
    def _():
        start_gather(0, 0)

    @pl.when(i + 1 < pl.num_programs(0))
    def _():
        start_gather(i + 1, (i + 1) % 2)

    b = i % 2
    pltpu.make_async_copy(yb_hbm.at[pl.ds(0, TOP_K * tm), :], buf.at[b], sem.at[b]).wait()
    routed = buf[b, 0:tm, :] * gate_ref[:, 0:1]
    for k in range(1, TOP_K):
        routed = routed + buf[b, k * tm:(k + 1) * tm, :] * gate_ref[:, k:k + 1]
    x = x_ref[...] + (routed + sh_ref[...])
    o_ref[...] = (x * lax.rsqrt(jnp.mean(x * x, axis=-1, keepdims=True) + RMS_EPS)) * g_ref[...]


def _combine_final_norm(x, yb, slot, gates, shared, gain):
    m, d = x.shape
    tm = COMBINE_ROWS
    assert m % tm == 0 and yb.shape[0] >= TOP_K * tm
    row = pl.BlockSpec((tm, d), lambda i, s: (i, 0))
    return pl.pallas_call(
        _combine_body,
        grid_spec=pltpu.PrefetchScalarGridSpec(
            num_scalar_prefetch=1,
            grid=(m // tm,),
            in_specs=[row, row, pl.BlockSpec((tm, TOP_K), lambda i, s: (i, 0)),
                      pl.BlockSpec((1, d), lambda i, s: (0, 0)), pl.BlockSpec(memory_space=pl.ANY)],
            out_specs=row,
            scratch_shapes=[pltpu.VMEM((2, TOP_K * tm, d), F32), pltpu.SemaphoreType.DMA((2,))],
        ),
        out_shape=jax.ShapeDtypeStruct((m, d), F32),
        compiler_params=_params("arbitrary"),
        name="combine_final_norm",
    )(slot.reshape(-1), x, shared, gates, gain.reshape(1, d), yb)


def _route_body(lg_ref, rb_ref, idx_ref, gate_ref, rank_ref, cnt_ref, seen_ref):
    @pl.when(pl.program_id(0) == 0)
    def _():
        seen_ref[...] = jnp.zeros_like(seen_ref)

    tm, n_exp = lg_ref.shape
    scores = _sigmoid(lg_ref[...])
    biased = scores + rb_ref[...]
    expert = lax.broadcasted_iota(jnp.int32, (tm, n_exp), 1).astype(F32)
    chosen = jnp.zeros((tm, n_exp), F32)
    picks = []
    for _ in range(TOP_K):
        best = jnp.max(biased, axis=-1, keepdims=True)
        first = jnp.min(jnp.where(biased == best, expert, float(n_exp)), axis=-1, keepdims=True)
        hit = expert == first
        picks.append((first, hit, jnp.sum(jnp.where(hit, scores, 0.0), axis=-1, keepdims=True)))
        biased = jnp.where(hit, -jnp.inf, biased)
        chosen = jnp.where(hit, 1.0, chosen)

    r = lax.broadcasted_iota(jnp.int32, (tm, tm), 0)
    c = lax.broadcasted_iota(jnp.int32, (tm, tm), 1)
    earlier = jnp.where(c < r, 1.0, 0.0).astype(BF16)
    rank_all = jnp.dot(earlier, chosen.astype(BF16), preferred_element_type=F32) + seen_ref[...]

    total = picks[0][2]
    for _, _, sel in picks[1:]:
        total = total + sel
    col = lax.broadcasted_iota(jnp.int32, (tm, TOP_K), 1)
    idx = jnp.zeros((tm, TOP_K), F32)
    gate = jnp.zeros((tm, TOP_K), F32)
    rank = jnp.zeros((tm, TOP_K), F32)
    for k, (first, hit, sel) in enumerate(picks):
        idx = jnp.where(col == k, first, idx)
        gate = jnp.where(col == k, sel / total * ROUTED_SCALE, gate)
        rank = jnp.where(col == k, jnp.sum(jnp.where(hit, rank_all, 0.0), axis=-1, keepdims=True), rank)
    idx_ref[...] = idx.astype(jnp.int32)
    gate_ref[...] = gate
    rank_ref[...] = rank.astype(jnp.int32)
    seen_ref[...] += jnp.sum(chosen, axis=0, keepdims=True)
    cnt_ref[...] = seen_ref[...].astype(jnp.int32)


def _route(logits, router_bias, tm=256):
    t, n_exp = logits.shape
    assert t % tm == 0
    row = lambda width: pl.BlockSpec((tm, width), lambda i: (i, 0))
    one = pl.BlockSpec((1, n_exp), lambda i: (0, 0))
    return pl.pallas_call(
        _route_body,
        grid=(t // tm,),
        in_specs=[row(n_exp), one],
        out_specs=[row(TOP_K), row(TOP_K), row(TOP_K), one],
        out_shape=[jax.ShapeDtypeStruct((t, TOP_K), jnp.int32), jax.ShapeDtypeStruct((t, TOP_K), F32),
                   jax.ShapeDtypeStruct((t, TOP_K), jnp.int32), jax.ShapeDtypeStruct((1, n_exp), jnp.int32)],
        scratch_shapes=[pltpu.VMEM((1, n_exp), F32)],
        compiler_params=_params("arbitrary"),
        name="router_topk",
    )(logits, router_bias.reshape(1, n_exp).astype(F32))


def _moe(hn, logits, router_bias, w_gate, w_up, w_down, ws_gate, ws_up, ws_down):
    t, d = hn.shape
    idx, gates, rank, counts = _route(logits, router_bias)

    padded = (counts[0] + EXPERT_ROWS - 1) // EXPERT_ROWS * EXPERT_ROWS
    pad_end = jnp.cumsum(padded)
    pad_start = pad_end - padded
    slot = pad_start[idx] + rank
    n_blocks = -(-(t * TOP_K) // EXPERT_ROWS) + N_EXPERTS
    block_start = jnp.arange(n_blocks, dtype=jnp.int32) * EXPERT_ROWS
    block_e = jnp.minimum(jnp.sum(pad_end[None, :] <= block_start[:, None], axis=1), N_EXPERTS - 1).astype(jnp.int32)
    n_used = (pad_end[-1:] // EXPERT_ROWS).astype(jnp.int32)
    _, tok_sorted = lax.sort_key_val(slot.reshape(-1), jnp.repeat(jnp.arange(t, dtype=jnp.int32), TOP_K))
    first = jnp.cumsum(counts[0]) - counts[0]
    slot_e = jnp.repeat(block_e, EXPERT_ROWS)
    within = jnp.arange(n_blocks * EXPERT_ROWS, dtype=jnp.int32) - pad_start[slot_e]
    entry = jnp.clip(first[slot_e] + within, 0, t * TOP_K - 1)
    slot_tok = jnp.where(within < counts[0][slot_e], tok_sorted[entry], 0).astype(jnp.int32)
    yb = _expert_mlp(hn, slot_tok, block_e, n_used, w_gate, w_up, w_down, EXPERT_ROWS)
    shared = _shared_mlp(hn, ws_gate[None], ws_up[None], ws_down[None], EXPERT_ROWS)
    return yb, slot.astype(jnp.int32), gates, shared


def kernel(x_prompt, x_sample, mem_prompt, cache_sb_k, cache_sb_v, page_table, state_ret, cache_mem_k, cache_mem_v,
           norm_mix, w_in, ret_gn, sb_gn, sb_bias, w_out, norm_mem, w_ck, w_cv, norm_cross, w_cq, w_co, norm_ffn,
           router_w, router_bias, w_gate, w_up, w_down, ws_gate, ws_up, ws_down, norm_final):
    depth = w_in.shape[0]
    assert depth == 1 and x_prompt.shape[0] == 1
    _, t, d = x_prompt.shape
    b, l, _ = x_sample.shape
    n_pages = page_table.shape[1]
    lyr = 0
    xp = x_prompt.reshape(t, d)
    xs = x_sample.reshape(b * l, d)
    pos_p = jnp.arange(t)
    pos_s = n_pages * PAGE_SIZE + jnp.arange(l)

    w_in_b = w_in[lyr].astype(BF16)
    w_out_b = w_out[lyr].astype(BF16)
    w_cq_b = w_cq[lyr].astype(BF16)
    w_co_b = w_co[lyr].astype(BF16)

    mem = mem_prompt.reshape(-1, d)
    n_mem = mem.shape[0]
    mkv = _mm(mem, jnp.concatenate([w_ck[lyr], w_cv[lyr]], axis=1).astype(BF16), gain=norm_mem[lyr])
    mk_p, mv_p = mkv[:, :MEM_WIDTH], mkv[:, MEM_WIDTH:]

    proj = _mm(xp, w_in_b, gain=norm_mix[lyr])
    sk_p, sv_p = proj[:, -2 * SB_WIDTH:-SB_WIDTH], proj[:, -SB_WIDTH:]
    o_ret, s_ret_p = _retention_prompt(proj, pos_p, ret_gn[lyr])
    o_sb = _sb_prompt(proj, sb_bias[lyr], sb_gn[lyr])
    xp = _mm(jnp.concatenate([o_ret, o_sb], axis=-1), w_out_b, res=xp)

    proj = _mm(xs, w_in_b, gain=norm_mix[lyr]).reshape(b, l, -1)
    sk_s, sv_s = proj[:, :, -2 * SB_WIDTH:-SB_WIDTH], proj[:, :, -SB_WIDTH:]
    o_ret, s_ret_s = _retention_sample(proj, state_ret[lyr], pos_s, ret_gn[lyr])
    o_sb = _sb_sample(proj, cache_sb_k[lyr], cache_sb_v[lyr], page_table, sb_bias[lyr], sb_gn[lyr])
    xs = _mm(jnp.concatenate([o_ret, o_sb], axis=-1).reshape(b * l, -1), w_out_b, res=xs)

    q = _mm(xp, w_cq_b, gain=norm_cross[lyr])
    o = _cross_attention(q[None], mk_p[None], mv_p[None], tq=512)
    xp = _mm(o[0], w_co_b, res=xp)
    q = _mm(xs, w_cq_b, gain=norm_cross[lyr]).reshape(b, l, MEM_WIDTH)
    o = _cross_attention(q, cache_mem_k[lyr].reshape(b, n_mem, MEM_WIDTH),
                         cache_mem_v[lyr].reshape(b, n_mem, MEM_WIDTH), tq=l)
    xs = _mm(o.reshape(b * l, MEM_WIDTH), w_co_b, res=xs)

    x_all = jnp.concatenate([xp, xs], axis=0)
    logits, hn = _mm(x_all, router_w[lyr].astype(BF16), gain=norm_ffn[lyr], emit_xn=True)
    yb, slot, gates, shared = _moe(hn, logits, router_bias[lyr], w_gate[lyr], w_up[lyr], w_down[lyr],
                                   ws_gate[lyr], ws_up[lyr], ws_down[lyr])
    y_all = _combine_final_norm(x_all, yb, slot, gates, shared, norm_final)

    y_prompt = y_all[:t].reshape(1, t, d)
    y_sample = y_all[t:].reshape(b, l, d)
    return (y_prompt, y_sample,
            s_ret_p[None, None],
            sk_p.reshape(1, 1, t, N_SB_HEADS, HEAD_DIM), sv_p.reshape(1, 1, t, N_SB_HEADS, HEAD_DIM),
            mk_p.reshape(1, 1, n_mem, N_MEM_HEADS, HEAD_DIM), mv_p.reshape(1, 1, n_mem, N_MEM_HEADS, HEAD_DIM),
            s_ret_s[None],
            sk_s.reshape(1, b, l, N_SB_HEADS, HEAD_DIM), sv_s.reshape(1, b, l, N_SB_HEADS, HEAD_DIM))
```

```python
import functools

import jax
import jax.numpy as jnp
from jax import lax
from jax.experimental import pallas as pl
from jax.experimental.pallas import tpu as pltpu

F32 = jnp.float32
BF16 = jnp.bfloat16

HEAD_DIM = 128
N_RET_HEADS = 8
N_SB_HEADS = 8
RET_WIDTH = N_RET_HEADS * HEAD_DIM
SB_WIDTH = N_SB_HEADS * HEAD_DIM
RET_CHUNK = 128
PAGE_SIZE = 128
ROPE_BASE = 10000.0
N_MEM_HEADS = 4
MEM_WIDTH = N_MEM_HEADS * HEAD_DIM
N_EXPERTS = 64
TOP_K = 8
ROUTED_SCALE = 2.5
RMS_EPS = 1e-6
NORM_EPS = 1e-6
QK_SCALE = HEAD_DIM ** -0.5

VMEM_LIMIT_BYTES = 56 * 1024 * 1024
SB_KEYS = 256
SB_QUERIES = 512
SB_PAGES_PER_STEP = 8
EXPERT_ROWS = 256
COMBINE_ROWS = 128
SB_CHUNK = 128
LOG2E = 1.4426950408889634

NT_DIMS = (((1,), (1,)), ((), ()))
TN_DIMS = (((0,), (0,)), ((), ()))


def _params(*sem):
    return pltpu.CompilerParams(dimension_semantics=sem, vmem_limit_bytes=VMEM_LIMIT_BYTES)


def _sigmoid(x):
    return 1.0 / (1.0 + jnp.exp(-x))


def _silu(x):
    return x * _sigmoid(x)


def _mm_body(*refs, has_norm, has_res, emit_xn):
    it = iter(refs)
    x_ref = next(it)
    g_ref = next(it) if has_norm else None
    w_ref = next(it)
    r_ref = next(it) if has_res else None
    o_ref = next(it)
    xo_ref = next(it) if emit_xn else None
    xn_ref = next(it)

    @pl.when(pl.program_id(1) == 0)
    def _():
        x = x_ref[...].astype(F32)
        if has_norm:
            x = x * lax.rsqrt(jnp.mean(x * x, axis=-1, keepdims=True) + RMS_EPS)
            x = x * g_ref[...]
        xb = x.astype(BF16)
        xn_ref[...] = xb
        if emit_xn:
            xo_ref[...] = x

    acc = jnp.dot(xn_ref[...], w_ref[...], preferred_element_type=F32)
    if has_res:
        acc = r_ref[...] + acc
    o_ref[...] = acc


def _mm(x, w, gain=None, res=None, emit_xn=False, tm=1024, tn=1024):
    m, k = x.shape
    n = w.shape[1]
    tm = min(tm, m)
    tn = min(tn, n)
    assert m % tm == 0 and n % tn == 0
    in_specs = [pl.BlockSpec((tm, k), lambda i, j: (i, 0))]
    args = [x]
    if gain is not None:
        in_specs.append(pl.BlockSpec((1, k), lambda i, j: (0, 0)))
        args.append(gain.reshape(1, k).astype(F32))
    in_specs.append(pl.BlockSpec((k, tn), lambda i, j: (0, j)))
    args.append(w)
    if res is not None:
        in_specs.append(pl.BlockSpec((tm, tn), lambda i, j: (i, j)))
        args.append(res)
    out_shape = [jax.ShapeDtypeStruct((m, n), F32)]
    out_specs = [pl.BlockSpec((tm, tn), lambda i, j: (i, j))]
    if emit_xn:
        out_shape.append(jax.ShapeDtypeStruct((m, k), F32))
        out_specs.append(pl.BlockSpec((tm, k), lambda i, j: (i, 0)))
    outs = pl.pallas_call(
        functools.partial(_mm_body, has_norm=gain is not None, has_res=res is not None, emit_xn=emit_xn),
        grid=(m // tm, n // tn),
        in_specs=in_specs,
        out_specs=out_specs,
        out_shape=out_shape,
        scratch_shapes=[pltpu.VMEM((tm, k), BF16)],
        compiler_params=_params("parallel", "arbitrary"),
        name="norm_matmul",
    )(*args)
    return outs if emit_xn else outs[0]


def _rotary_tables(pos):
    half = HEAD_DIM // 2
    inv_freq = ROPE_BASE ** (-jnp.arange(half, dtype=F32) / half)
    ang = pos.astype(F32)[:, None] * inv_freq[None, :]
    cos, sin = jnp.cos(ang), jnp.sin(ang)
    return jnp.concatenate([cos, cos], axis=-1), jnp.concatenate([-sin, sin], axis=-1)


def _retention_tables(length):
    log_g = jnp.log1p(-jnp.exp2(-5.0 - jnp.arange(N_RET_HEADS, dtype=F32)))
    i = jnp.arange(length, dtype=F32)
    diff = i[:, None] - i[None, :]
    decay = jnp.where(diff >= 0, jnp.exp(jnp.maximum(diff, 0.0)[None] * log_g[:, None, None]), 0.0)
    dq = jnp.exp((i[None, :] + 1.0) * log_g[:, None])
    dk = jnp.exp((length - 1.0 - i)[None, :] * log_g[:, None])
    ds = jnp.exp(length * log_g)
    lanes = (N_RET_HEADS, length, HEAD_DIM)
    return (decay, jnp.broadcast_to(dq[:, :, None], lanes), jnp.broadcast_to(dk[:, :, None], lanes),
            jnp.broadcast_to(ds[:, None, None], (N_RET_HEADS, 8, HEAD_DIM)))


def _rotate(x, cos, sin_signed):
    return x * cos + pltpu.roll(x, HEAD_DIM // 2, 1) * sin_signed


def _retention_step(q, k, v, s, decay, dq, dk, ds):
    qb, kb, vb = q.astype(BF16), k.astype(BF16), v.astype(BF16)
    scores = lax.dot_general(qb, kb, NT_DIMS, preferred_element_type=F32) * decay
    o = jnp.dot(scores.astype(BF16), vb, preferred_element_type=F32)
    o = o + jnp.dot((q * dq).astype(BF16), s.astype(BF16), preferred_element_type=F32)
    kd = (k * dk).astype(BF16)
    s_new = ds * s + lax.dot_general(kd, vb, TN_DIMS, preferred_element_type=F32)
    return o, s_new


def _gated_layernorm(o, gn, gate):
    c = o - jnp.mean(o, axis=-1, keepdims=True)
    y = c * lax.rsqrt(jnp.mean(c * c, axis=-1, keepdims=True) + NORM_EPS)
    return (y * gn) * _silu(gate)


def _ret_prompt_body(q_ref, k_ref, v_ref, g_ref, cos_ref, sin_ref, decay_ref, dq_ref, dk_ref, ds_ref, gn_ref,
                     o_ref, sfin_ref, s_ref):
    c = pl.program_id(0)

    @pl.when(c == 0)
    def _():
        s_ref[...] = jnp.zeros_like(s_ref)

    cos, sin = cos_ref[...], sin_ref[...]
    for h in range(N_RET_HEADS):
        cols = slice(h * HEAD_DIM, (h + 1) * HEAD_DIM)
        q = _rotate(q_ref[:, cols], cos, sin)
        k = _rotate(k_ref[:, cols], cos, sin) * QK_SCALE
        o, s_new = _retention_step(q, k, v_ref[:, cols], s_ref[h], decay_ref[h], dq_ref[h], dk_ref[h],
                                   ds_ref[h, 0:1, :])
        s_ref[h] = s_new
        o_ref[:, cols] = _gated_layernorm(o, gn_ref[:, cols], g_ref[:, cols])

    @pl.when(c == pl.num_programs(0) - 1)
    def _():
        sfin_ref[...] = s_ref[...]


def _retention_prompt(proj, pos, gn):
    t = proj.shape[0]
    cos, sin = _rotary_tables(pos)
    decay, dq, dk, ds = _retention_tables(RET_CHUNK)
    segment = lambda j: pl.BlockSpec((RET_CHUNK, RET_WIDTH), lambda c: (c, j))
    rot = pl.BlockSpec((RET_CHUNK, HEAD_DIM), lambda c: (c, 0))
    full = lambda a: pl.BlockSpec(a.shape, lambda c: (0,) * a.ndim)
    gn2 = gn.reshape(1, RET_WIDTH)
    state = jax.ShapeDtypeStruct((N_RET_HEADS, HEAD_DIM, HEAD_DIM), F32)
    return pl.pallas_call(
        _ret_prompt_body,
        grid=(t // RET_CHUNK,),
        in_specs=[segment(0), segment(1), segment(2), segment(3), rot, rot, full(decay), full(dq), full(dk), full(ds),
                  full(gn2)],
        out_specs=[segment(0), pl.BlockSpec(state.shape, lambda c: (0, 0, 0))],
        out_shape=[jax.ShapeDtypeStruct((t, RET_WIDTH), F32), state],
        scratch_shapes=[pltpu.VMEM(state.shape, F32)],
        compiler_params=_params("arbitrary"),
        name="retention_prompt",
    )(proj, proj, proj, proj, cos, sin, decay, dq, dk, ds, gn2)


def _ret_sample_body(q_ref, k_ref, v_ref, g_ref, st_ref, cos_ref, sin_ref, decay_ref, dq_ref, dk_ref, ds_ref, gn_ref,
                     o_ref, snew_ref):
    cos, sin = cos_ref[...], sin_ref[...]
    for h in range(N_RET_HEADS):
        cols = slice(h * HEAD_DIM, (h + 1) * HEAD_DIM)
        q = _rotate(q_ref[0, :, cols], cos, sin)
        k = _rotate(k_ref[0, :, cols], cos, sin) * QK_SCALE
        o, s_new = _retention_step(q, k, v_ref[0, :, cols], st_ref[0, h], decay_ref[h], dq_ref[h], dk_ref[h],
                                   ds_ref[h, 0:1, :])
        snew_ref[0, h] = s_new
        o_ref[0, :, cols] = _gated_layernorm(o, gn_ref[:, cols], g_ref[0, :, cols])


def _retention_sample(proj, state, pos, gn):
    b, l, _ = proj.shape
    cos, sin = _rotary_tables(pos)
    decay, dq, dk, ds = _retention_tables(l)
    segment = lambda j: pl.BlockSpec((1, l, RET_WIDTH), lambda i: (i, 0, j))
    st = pl.BlockSpec((1, N_RET_HEADS, HEAD_DIM, HEAD_DIM), lambda i: (i, 0, 0, 0))
    full = lambda a: pl.BlockSpec(a.shape, lambda i: (0,) * a.ndim)
    gn2 = gn.reshape(1, RET_WIDTH)
    return pl.pallas_call(
        _ret_sample_body,
        grid=(b,),
        in_specs=[segment(0), segment(1), segment(2), segment(3), st, full(cos), full(sin), full(decay), full(dq),
                  full(dk), full(ds), full(gn2)],
        out_specs=[segment(0), st],
        out_shape=[jax.ShapeDtypeStruct((b, l, RET_WIDTH), F32), jax.ShapeDtypeStruct(state.shape, F32)],
        compiler_params=_params("parallel"),
        name="retention_sample",
    )(proj, proj, proj, proj, state, cos, sin, decay, dq, dk, ds, gn2)


def _suffix_matrix(n):
    j = lax.broadcasted_iota(jnp.int32, (n, n), 0)
    s = lax.broadcasted_iota(jnp.int32, (n, n), 1)
    return jnp.where(j > s, 1.0, 0.0).astype(BF16)


def _neg_abs(x):
    bits = lax.bitcast_convert_type(x, jnp.uint32) | jnp.uint32(0x80000000)
    return lax.bitcast_convert_type(bits, F32)


def _sb_block_log2(z2, tri, mask):
    rows, keys = z2.shape
    chunk = min(rows, SB_CHUNK)
    suffix = tri.shape[1]
    log2_w, dropped = [], []
    for c in range(0, rows, chunk):
        parts, later = [], None
        for s in reversed(range(0, keys, suffix)):
            zc = z2[c:c + chunk, s:s + suffix]
            neg_log_keep = jnp.maximum(zc, 0.0) + jnp.log2(1.0 + jnp.exp2(_neg_abs(zc)))
            log_beta = zc - neg_log_keep
            if mask is not None:
                neg_log_keep = jnp.where(mask[c:c + chunk, s:s + suffix], neg_log_keep, 0.0)
            between = jnp.dot(neg_log_keep.astype(BF16), tri, preferred_element_type=F32)
            total = jnp.sum(neg_log_keep, axis=-1, keepdims=True)
            if later is None:
                parts.append(log_beta - between)
                later = total
            else:
                parts.append(log_beta - between - later)
                later = later + total
        log2_w.append(jnp.concatenate(parts[::-1], axis=1))
        dropped.append(later)
    return jnp.concatenate(log2_w, axis=0), jnp.concatenate(dropped, axis=0)


def _head_rmsnorm(o, gn):
    return (o * lax.rsqrt(jnp.mean(o * o, axis=-1, keepdims=True) + RMS_EPS)) * gn


def _sb_prompt_body(bias_ref, q_ref, k_ref, v_ref, gn_ref, o_ref, kb_ref, vb_ref, tri_ref):
    h = pl.program_id(0)
    i = pl.program_id(1)

    @pl.when(i == 0)
    def _():
        kb_ref[...] = k_ref[...].astype(BF16)
        vb_ref[...] = v_ref[...].astype(BF16)
        tri_ref[...] = _suffix_matrix(SB_KEYS)

    q = (q_ref[...] * (QK_SCALE * LOG2E)).astype(BF16)
    bias = bias_ref[h] * LOG2E
    tri = tri_ref[...]

    def key_rows(j):
        return pl.ds(pl.multiple_of(j * SB_KEYS, SB_KEYS), SB_KEYS)

    def logits(j):
        return lax.dot_general(q, kb_ref[key_rows(j), :], NT_DIMS, preferred_element_type=F32) + bias

    def weighted_values(log2_w, j, mask=None):
        a = jnp.exp2(log2_w)
        if mask is not None:
            a = jnp.where(mask, a, 0.0)
        return jnp.dot(a.astype(BF16), vb_ref[key_rows(j), :], preferred_element_type=F32)

    ratio = SB_QUERIES // SB_KEYS
    r = lax.broadcasted_iota(jnp.int32, (SB_QUERIES, SB_KEYS), 0)
    c = lax.broadcasted_iota(jnp.int32, (SB_QUERIES, SB_KEYS), 1)
    acc = jnp.zeros((SB_QUERIES, HEAD_DIM), F32)
    drop = jnp.zeros((SB_QUERIES, 1), F32)
    for d in reversed(range(ratio)):
        mask = c + d * SB_KEYS < r
        log2_w, dropped = _sb_block_log2(logits(i * ratio + d), tri, mask)
        acc = acc + weighted_values(log2_w - drop, i * ratio + d, mask)
        drop = drop + dropped

    n_old = i * ratio

    def step(n, state):
        z_prev, log2_w_prev, acc, drop = state
        z_new = logits(jnp.maximum(n_old - 1 - n, 0))
        log2_w, dropped = _sb_block_log2(z_prev, tri, None)
        out = weighted_values(log2_w_prev, jnp.clip(n_old + 1 - n, 0, n_old))
        return z_new, log2_w - drop, acc + out, drop + dropped

    idle = jnp.full((SB_QUERIES, SB_KEYS), -1e30, F32)
    _, _, acc, _ = lax.fori_loop(0, jnp.where(n_old > 0, n_old + 2, 0), step, (idle, idle, acc, drop))
    o_ref[...] = _head_rmsnorm(acc, gn_ref[...])


def _sb_prompt(proj, bias, gn):
    t, width = proj.shape
    assert t % SB_QUERIES == 0 and SB_QUERIES % SB_KEYS == 0
    q0, k0, v0 = [(width - n * SB_WIDTH) // HEAD_DIM for n in (3, 2, 1)]
    qspec = pl.BlockSpec((SB_QUERIES, HEAD_DIM), lambda h, i, b: (i, q0 + h))
    kspec = pl.BlockSpec((t, HEAD_DIM), lambda h, i, b: (0, k0 + h))
    vspec = pl.BlockSpec((t, HEAD_DIM), lambda h, i, b: (0, v0 + h))
    return pl.pallas_call(
        _sb_prompt_body,
        grid_spec=pltpu.PrefetchScalarGridSpec(
            num_scalar_prefetch=1,
            grid=(N_SB_HEADS, t // SB_QUERIES),
            in_specs=[qspec, kspec, vspec, pl.BlockSpec((1, HEAD_DIM), lambda h, i, b: (0, h))],
            out_specs=pl.BlockSpec((SB_QUERIES, HEAD_DIM), lambda h, i, b: (i, h)),
            scratch_shapes=[pltpu.VMEM((t, HEAD_DIM), BF16), pltpu.VMEM((t, HEAD_DIM), BF16),
                            pltpu.VMEM((SB_KEYS, SB_KEYS), BF16)],
        ),
        out_shape=jax.ShapeDtypeStruct((t, SB_WIDTH), F32),
        compiler_params=_params("parallel", "arbitrary"),
        name="stick_breaking_prompt",
    )(bias.astype(F32), proj, proj, proj, gn.reshape(1, SB_WIDTH))


def _sb_sample_body(pt_ref, bias_ref, q_ref, kn_ref, vn_ref, *rest):
    kp_refs = rest[:SB_PAGES_PER_STEP]
    vp_refs = rest[SB_PAGES_PER_STEP:2 * SB_PAGES_PER_STEP]
    gn_ref, o_ref, acc_ref, drop_ref = rest[2 * SB_PAGES_PER_STEP:]
    step = pl.program_id(1)
    l = q_ref.shape[1]
    rows = N_SB_HEADS * l
    tri = _suffix_matrix(PAGE_SIZE)
    bias = jnp.concatenate([jnp.full((l, 1), bias_ref[h] * LOG2E, F32) for h in range(N_SB_HEADS)], axis=0)
    qs = [(q_ref[0, :, h * HEAD_DIM:(h + 1) * HEAD_DIM] * (QK_SCALE * LOG2E)).astype(BF16)
          for h in range(N_SB_HEADS)]

    def logits(keys):
        z = [lax.dot_general(qs[h], keys[h], NT_DIMS, preferred_element_type=F32) for h in range(N_SB_HEADS)]
        return jnp.concatenate(z, axis=0) + bias

    def weighted_values(a, values):
        out = [jnp.dot(a[h * l:(h + 1) * l, :].astype(BF16), values[h], preferred_element_type=F32)
               for h in range(N_SB_HEADS)]
        return jnp.concatenate(out, axis=0)

    @pl.when(step == 0)
    def _():
        pad = jnp.zeros((PAGE_SIZE - l, HEAD_DIM), F32)
        kn = [jnp.concatenate([kn_ref[0, :, h * HEAD_DIM:(h + 1) * HEAD_DIM], pad], axis=0).astype(BF16)
              for h in range(N_SB_HEADS)]
        vn = [jnp.concatenate([vn_ref[0, :, h * HEAD_DIM:(h + 1) * HEAD_DIM], pad], axis=0).astype(BF16)
              for h in range(N_SB_HEADS)]
        qi = lax.broadcasted_iota(jnp.int32, (rows, PAGE_SIZE), 0) % l
        kj = lax.broadcasted_iota(jnp.int32, (rows, PAGE_SIZE), 1)
        mask = kj < qi
        log2_w, dropped = _sb_block_log2(logits(kn), tri, mask)
        acc_ref[...] = weighted_values(jnp.where(mask, jnp.exp2(log2_w), 0.0), vn)
        drop_ref[...] = dropped

    head_rows = lambda h: pl.ds(h, PAGE_SIZE, stride=N_SB_HEADS)
    z = jnp.concatenate([logits([kp_ref[head_rows(h), :].astype(BF16) for h in range(N_SB_HEADS)])
                         for kp_ref in kp_refs], axis=0)
    log2_w, dropped = _sb_block_log2(z, tri, None)
    drop = drop_ref[...]
    acc = acc_ref[...]
    for n, vp_ref in enumerate(vp_refs):
        page = slice(n * rows, (n + 1) * rows)
        a = jnp.exp2(log2_w[page, :] - drop)
        acc = acc + weighted_values(a, [vp_ref[head_rows(h), :].astype(BF16) for h in range(N_SB_HEADS)])
        drop = drop + dropped[page, :]
    acc_ref[...] = acc
    drop_ref[...] = drop

    @pl.when(step == pl.num_programs(1) - 1)
    def _():
        for h in range(N_SB_HEADS):
            cols = slice(h * HEAD_DIM, (h + 1) * HEAD_DIM)
            o_ref[0, :, cols] = _head_rmsnorm(acc[h * l:(h + 1) * l, :], gn_ref[:, cols])


def _sb_sample(proj, cache_k, cache_v, page_table, bias, gn):
    b, l, width = proj.shape
    n_pages = page_table.shape[1]
    assert n_pages % SB_PAGES_PER_STEP == 0
    n_phys = cache_k.shape[0]
    page_rows = PAGE_SIZE * N_SB_HEADS
    cache_k = cache_k.reshape(n_phys, page_rows, HEAD_DIM)
    cache_v = cache_v.reshape(n_phys, page_rows, HEAD_DIM)
    segment = lambda j: pl.BlockSpec((1, l, SB_WIDTH), lambda i, s, pt, bs: (i, 0, j))
    last = width // SB_WIDTH - 1

    def page(n):
        return pl.BlockSpec((None, page_rows, HEAD_DIM),
                            lambda i, s, pt, bs: (pt[i, n_pages - 1 - (s * SB_PAGES_PER_STEP + n)], 0, 0))

    pages = [page(n) for n in range(SB_PAGES_PER_STEP)]
    return pl.pallas_call(
        _sb_sample_body,
        grid_spec=pltpu.PrefetchScalarGridSpec(
            num_scalar_prefetch=2,
            grid=(b, n_pages // SB_PAGES_PER_STEP),
            in_specs=[segment(last - 2), segment(last - 1), segment(last)] + pages + pages
                     + [pl.BlockSpec((1, SB_WIDTH), lambda i, s, pt, bs: (0, 0))],
            out_specs=segment(0),
            scratch_shapes=[pltpu.VMEM((N_SB_HEADS * l, HEAD_DIM), F32), pltpu.VMEM((N_SB_HEADS * l, 1), F32)],
        ),
        out_shape=jax.ShapeDtypeStruct((b, l, SB_WIDTH), F32),
        compiler_params=_params("parallel", "arbitrary"),
        name="stick_breaking_sample",
    )(page_table, bias.astype(F32), proj, proj, proj, *([cache_k] * SB_PAGES_PER_STEP),
      *([cache_v] * SB_PAGES_PER_STEP), gn.reshape(1, SB_WIDTH))


def _cross_body(q_ref, k_ref, v_ref, o_ref):
    n_mem = k_ref.shape[0] // N_MEM_HEADS
    for h in range(N_MEM_HEADS):
        cols = slice(h * HEAD_DIM, (h + 1) * HEAD_DIM)
        head_rows = pl.ds(h, n_mem, stride=N_MEM_HEADS)
        q = q_ref[0, :, cols].astype(BF16)
        k = k_ref[head_rows, :].astype(BF16)
        v = v_ref[head_rows, :].astype(BF16)
        s = lax.dot_general(q, k, NT_DIMS, preferred_element_type=F32) * QK_SCALE
        e = jnp.exp(s - jnp.max(s, axis=-1, keepdims=True))
        p = e / jnp.sum(e, axis=-1, keepdims=True)
        o_ref[0, :, cols] = jnp.dot(p.astype(BF16), v, preferred_element_type=F32)


def _cross_attention(q, mem_k, mem_v, tq):
    b, t, _ = q.shape
    m = mem_k.shape[1]
    tq = min(tq, t)
    qspec = pl.BlockSpec((1, tq, MEM_WIDTH), lambda i, j: (i, j, 0))
    mspec = pl.BlockSpec((None, m, HEAD_DIM), lambda i, j: (i, 0, 0))
    return pl.pallas_call(
        _cross_body,
        grid=(b, t // tq),
        in_specs=[qspec, mspec, mspec],
        out_specs=qspec,
        out_shape=jax.ShapeDtypeStruct(q.shape, F32),
        compiler_params=_params("parallel", "arbitrary"),
        name="memory_cross_attention",
    )(q, mem_k, mem_v)


def _swiglu(x, wg_b, wu_b, wd_b):
    g = jnp.dot(x, wg_b[...], preferred_element_type=F32)
    u = jnp.dot(x, wu_b[...], preferred_element_type=F32)
    return jnp.dot((_silu(g) * u).astype(BF16), wd_b[...], preferred_element_type=F32)


def _cast_weights(wg_ref, wu_ref, wd_ref, wg_b, wu_b, wd_b):
    wg_b[...] = wg_ref[0].astype(BF16)
    wu_b[...] = wu_ref[0].astype(BF16)
    wd_b[...] = wd_ref[0].astype(BF16)


def _shared_body(x_ref, wg_ref, wu_ref, wd_ref, o_ref, wg_b, wu_b, wd_b):
    @pl.when(pl.program_id(0) == 0)
    def _():
        _cast_weights(wg_ref, wu_ref, wd_ref, wg_b, wu_b, wd_b)

    o_ref[...] = _swiglu(x_ref[...].astype(BF16), wg_b, wu_b, wd_b)


def _shared_mlp(x, w_gate, w_up, w_down, rows):
    r, d = x.shape
    f = w_gate.shape[2]
    wspec = lambda shape: pl.BlockSpec((1,) + shape, lambda i: (0, 0, 0))
    return pl.pallas_call(
        _shared_body,
        grid=(r // rows,),
        in_specs=[pl.BlockSpec((rows, d), lambda i: (i, 0)), wspec((d, f)), wspec((d, f)), wspec((f, d))],
        out_specs=pl.BlockSpec((rows, d), lambda i: (i, 0)),
        out_shape=jax.ShapeDtypeStruct((r, d), F32),
        scratch_shapes=[pltpu.VMEM((d, f), BF16), pltpu.VMEM((d, f), BF16), pltpu.VMEM((f, d), BF16)],
        compiler_params=_params("arbitrary"),
        name="shared_mlp",
    )(x, w_gate, w_up, w_down)


def _expert_body(be_ref, used_ref, shift_ref, tok_ref, x_hbm, wg_ref, wu_ref, wd_ref, o_ref, wg_b, wu_b, wd_b,
                 xbuf0, xbuf1, sem):
    b = pl.program_id(0)
    rows = o_ref.shape[0]
    n_used = used_ref[0]
    bufs = (xbuf0, xbuf1)
    last_entry = tok_ref.shape[0] - 1

    def row_token(first_entry, r):
        return tok_ref[jnp.minimum(first_entry + r, last_entry)]

    def first_entry(blk):
        return blk * rows - shift_ref[blk]

    def wait_gather(parity):
        pltpu.make_async_copy(x_hbm.at[pl.ds(0, rows), :], bufs[parity], sem.at[parity]).wait()

    @pl.when(b == 0)
    def _():
        def row(r, carry):
            src = x_hbm.at[pl.ds(row_token(first_entry(0), r), 1), :]
            pltpu.make_async_copy(src, xbuf0.at[pl.ds(r, 1), :], sem.at[0]).start()
            return carry

        lax.fori_loop(0, rows, row, 0, unroll=8)

    @pl.when(jnp.logical_and(b < n_used, jnp.logical_or(b == 0, be_ref[b] != be_ref[jnp.maximum(b - 1, 0)])))
    def _():
        _cast_weights(wg_ref, wu_ref, wd_ref, wg_b, wu_b, wd_b)

    def run(parity):
        nxt = first_entry(jnp.minimum(b + 1, n_used - 1))
        wait_gather(parity)
        x = bufs[parity][...].astype(BF16)
        for r in range(rows):
            src = x_hbm.at[pl.ds(row_token(nxt, r), 1), :]
            pltpu.make_async_copy(src, bufs[1 - parity].at[pl.ds(r, 1), :], sem.at[1 - parity]).start()
        o_ref[...] = _swiglu(x, wg_b, wu_b, wd_b)

        @pl.when(b == n_used - 1)
        def _():
            wait_gather(1 - parity)

    for parity in (0, 1):
        pl.when(jnp.logical_and(b < n_used, b % 2 == parity))(functools.partial(run, parity))

    @pl.when(b >= n_used)
    def _():
        o_ref[...] = jnp.zeros_like(o_ref)


def _expert_mlp(x, slot_token, block_shift, block_expert, n_used, w_gate, w_up, w_down, rows):
    d = x.shape[1]
    f = w_gate.shape[2]
    n_blocks = block_expert.shape[0]
    wspec = lambda shape: pl.BlockSpec((1,) + shape, lambda i, be, nu, sh, tok: (be[i], 0, 0))
    return pl.pallas_call(
        _expert_body,
        grid_spec=pltpu.PrefetchScalarGridSpec(
            num_scalar_prefetch=4,
            grid=(n_blocks,),
            in_specs=[pl.BlockSpec(memory_space=pl.ANY), wspec((d, f)), wspec((d, f)), wspec((f, d))],
            out_specs=pl.BlockSpec((rows, d), lambda i, be, nu, sh, tok: (i, 0)),
            scratch_shapes=[pltpu.VMEM((d, f), BF16), pltpu.VMEM((d, f), BF16), pltpu.VMEM((f, d), BF16),
                            pltpu.VMEM((rows, d), F32), pltpu.VMEM((rows, d), F32), pltpu.SemaphoreType.DMA((2,))],
        ),
        out_shape=jax.ShapeDtypeStruct((n_blocks * rows, d), F32),
        compiler_params=_params("arbitrary"),
        name="expert_mlp",
    )(block_expert, n_used, block_shift, slot_token, x, w_gate, w_up, w_down)


def _combine_body(slot_ref, x_ref, sh_ref, gate_ref, g_ref, yb_hbm, o_ref, buf, sem):
    i = pl.program_id(0)
    tm = x_ref.shape[0]

    def start_gather(tile, b):
        base = tile * (tm * TOP_K)

        def token(r, carry):
            for k in range(TOP_K):
                src = yb_hbm.at[pl.ds(slot_ref[base + r * TOP_K + k], 1), :]
                pltpu.make_async_copy(src, buf.at[b, pl.ds(k * tm + r, 1), :], sem.at[b]).start()
            return carry

        lax.fori_loop(0, tm, token, 0)

    @pl.when(i == 0)
    def _():
        start_gather(0, 0)

    @pl.when(i + 1 < pl.num_programs(0))
    def _():
        start_gather(i + 1, (i + 1) % 2)

    b = i % 2
    pltpu.make_async_copy(yb_hbm.at[pl.ds(0, TOP_K * tm), :], buf.at[b], sem.at[b]).wait()
    routed = buf[b, 0:tm, :] * gate_ref[:, 0:1]
    for k in range(1, TOP_K):
        routed = routed + buf[b, k * tm:(k + 1) * tm, :] * gate_ref[:, k:k + 1]
    x = x_ref[...] + (routed + sh_ref[...])
    o_ref[...] = (x * lax.rsqrt(jnp.mean(x * x, axis=-1, keepdims=True) + RMS_EPS)) * g_ref[...]


def _combine_final_norm(x, yb, slot, gates, shared, gain):
    m, d = x.shape
    tm = COMBINE_ROWS
    assert m % tm == 0 and yb.shape[0] >= TOP_K * tm
    row = pl.BlockSpec((tm, d), lambda i, s: (i, 0))
    return pl.pallas_call(
        _combine_body,
        grid_spec=pltpu.PrefetchScalarGridSpec(
            num_scalar_prefetch=1,
            grid=(m // tm,),
            in_specs=[row, row, pl.BlockSpec((tm, TOP_K), lambda i, s: (i, 0)),
                      pl.BlockSpec((1, d), lambda i, s: (0, 0)), pl.BlockSpec(memory_space=pl.ANY)],
            out_specs=row,
            scratch_shapes=[pltpu.VMEM((2, TOP_K * tm, d), F32), pltpu.SemaphoreType.DMA((2,))],
        ),
        out_shape=jax.ShapeDtypeStruct((m, d), F32),
        compiler_params=_params("arbitrary"),
        name="combine_final_norm",
    )(slot.reshape(-1), x, shared, gates, gain.reshape(1, d), yb)


def _route_body(lg_ref, rb_ref, idx_ref, gate_ref, rank_ref, cnt_ref, seen_ref):
    @pl.when(pl.program_id(0) == 0)
    def _():
        seen_ref[...] = jnp.zeros_like(seen_ref)

    tm, n_exp = lg_ref.shape
    scores = _sigmoid(lg_ref[...])
    biased = scores + rb_ref[...]
    expert = lax.broadcasted_iota(jnp.int32, (tm, n_exp), 1).astype(F32)
    chosen = jnp.zeros((tm, n_exp), F32)
    picks = []
    for _ in range(TOP_K):
        best = jnp.max(biased, axis=-1, keepdims=True)
        first = jnp.min(jnp.where(biased == best, expert, float(n_exp)), axis=-1, keepdims=True)
        hit = expert == first
        picks.append((first, hit, jnp.sum(jnp.where(hit, scores, 0.0), axis=-1, keepdims=True)))
        biased = jnp.where(hit, -jnp.inf, biased)
        chosen = jnp.where(hit, 1.0, chosen)

    r = lax.broadcasted_iota(jnp.int32, (tm, tm), 0)
    c = lax.broadcasted_iota(jnp.int32, (tm, tm), 1)
    earlier = jnp.where(c < r, 1.0, 0.0).astype(BF16)
    rank_all = jnp.dot(earlier, chosen.astype(BF16), preferred_element_type=F32) + seen_ref[...]

    total = picks[0][2]
    for _, _, sel in picks[1:]:
        total = total + sel
    col = lax.broadcasted_iota(jnp.int32, (tm, TOP_K), 1)
    idx = jnp.zeros((tm, TOP_K), F32)
    gate = jnp.zeros((tm, TOP_K), F32)
    rank = jnp.zeros((tm, TOP_K), F32)
    for k, (first, hit, sel) in enumerate(picks):
        idx = jnp.where(col == k, first, idx)
        gate = jnp.where(col == k, sel / total * ROUTED_SCALE, gate)
        rank = jnp.where(col == k, jnp.sum(jnp.where(hit, rank_all, 0.0), axis=-1, keepdims=True), rank)
    idx_ref[...] = idx.astype(jnp.int32)
    gate_ref[...] = gate
    rank_ref[...] = rank.astype(jnp.int32)
    seen_ref[...] += jnp.sum(chosen, axis=0, keepdims=True)
    cnt_ref[...] = seen_ref[...].astype(jnp.int32)


def _route(logits, router_bias, tm=256):
    t, n_exp = logits.shape
    assert t % tm == 0
    row = lambda width: pl.BlockSpec((tm, width), lambda i: (i, 0))
    one = pl.BlockSpec((1, n_exp), lambda i: (0, 0))
    return pl.pallas_call(
        _route_body,
        grid=(t // tm,),
        in_specs=[row(n_exp), one],
        out_specs=[row(TOP_K), row(TOP_K), row(TOP_K), one],
        out_shape=[jax.ShapeDtypeStruct((t, TOP_K), jnp.int32), jax.ShapeDtypeStruct((t, TOP_K), F32),
                   jax.ShapeDtypeStruct((t, TOP_K), jnp.int32), jax.ShapeDtypeStruct((1, n_exp), jnp.int32)],
        scratch_shapes=[pltpu.VMEM((1, n_exp), F32)],
        compiler_params=_params("arbitrary"),
        name="router_topk",
    )(logits, router_bias.reshape(1, n_exp).astype(F32))


def _moe(hn, logits, router_bias, w_gate, w_up, w_down, ws_gate, ws_up, ws_down):
    t, d = hn.shape
    idx, gates, rank, counts = _route(logits, router_bias)

    padded = (counts[0] + EXPERT_ROWS - 1) // EXPERT_ROWS * EXPERT_ROWS
    pad_end = jnp.cumsum(padded)
    pad_start = pad_end - padded
    slot = pad_start[idx] + rank
    n_blocks = -(-(t * TOP_K) // EXPERT_ROWS) + N_EXPERTS
    block_start = jnp.arange(n_blocks, dtype=jnp.int32) * EXPERT_ROWS
    block_e = jnp.minimum(jnp.sum(pad_end[None, :] <= block_start[:, None], axis=1), N_EXPERTS - 1).astype(jnp.int32)
    n_used = (pad_end[-1:] // EXPERT_ROWS).astype(jnp.int32)
    _, tok_sorted = lax.sort_key_val(slot.reshape(-1), jnp.repeat(jnp.arange(t, dtype=jnp.int32), TOP_K))
    first = jnp.cumsum(counts[0]) - counts[0]
    block_shift = (pad_start - first)[block_e].astype(jnp.int32)
    yb = _expert_mlp(hn, tok_sorted, block_shift, block_e, n_used, w_gate, w_up, w_down, EXPERT_ROWS)
    shared = _shared_mlp(hn, ws_gate[None], ws_up[None], ws_down[None], EXPERT_ROWS)
    return yb, slot.astype(jnp.int32), gates, shared


def kernel(x_prompt, x_sample, mem_prompt, cache_sb_k, cache_sb_v, page_table, state_ret, cache_mem_k, cache_mem_v,
           norm_mix, w_in, ret_gn, sb_gn, sb_bias, w_out, norm_mem, w_ck, w_cv, norm_cross, w_cq, w_co, norm_ffn,
           router_w, router_bias, w_gate, w_up, w_down, ws_gate, ws_up, ws_down, norm_final):
    depth = w_in.shape[0]
    assert depth == 1 and x_prompt.shape[0] == 1
    _, t, d = x_prompt.shape
    b, l, _ = x_sample.shape
    n_pages = page_table.shape[1]
    lyr = 0
    xp = x_prompt.reshape(t, d)
    xs = x_sample.reshape(b * l, d)
    pos_p = jnp.arange(t)
    pos_s = n_pages * PAGE_SIZE + jnp.arange(l)

    w_in_b = w_in[lyr].astype(BF16)
    w_out_b = w_out[lyr].astype(BF16)
    w_cq_b = w_cq[lyr].astype(BF16)
    w_co_b = w_co[lyr].astype(BF16)

    mem = mem_prompt.reshape(-1, d)
    n_mem = mem.shape[0]
    mkv = _mm(mem, jnp.concatenate([w_ck[lyr], w_cv[lyr]], axis=1).astype(BF16), gain=norm_mem[lyr])
    mk_p, mv_p = mkv[:, :MEM_WIDTH], mkv[:, MEM_WIDTH:]

    proj = _mm(xp, w_in_b, gain=norm_mix[lyr])
    sk_p, sv_p = proj[:, -2 * SB_WIDTH:-SB_WIDTH], proj[:, -SB_WIDTH:]
    o_ret, s_ret_p = _retention_prompt(proj, pos_p, ret_gn[lyr])
    o_sb = _sb_prompt(proj, sb_bias[lyr], sb_gn[lyr])
    xp = _mm(jnp.concatenate([o_ret, o_sb], axis=-1), w_out_b, res=xp)

    proj = _mm(xs, w_in_b, gain=norm_mix[lyr]).reshape(b, l, -1)
    sk_s, sv_s = proj[:, :, -2 * SB_WIDTH:-SB_WIDTH], proj[:, :, -SB_WIDTH:]
    o_ret, s_ret_s = _retention_sample(proj, state_ret[lyr], pos_s, ret_gn[lyr])
    o_sb = _sb_sample(proj, cache_sb_k[lyr], cache_sb_v[lyr], page_table, sb_bias[lyr], sb_gn[lyr])
    xs = _mm(jnp.concatenate([o_ret, o_sb], axis=-1).reshape(b * l, -1), w_out_b, res=xs)

    q = _mm(xp, w_cq_b, gain=norm_cross[lyr])
    mem_rows = n_mem * N_MEM_HEADS
    o = _cross_attention(q[None], mk_p.reshape(1, mem_rows, HEAD_DIM), mv_p.reshape(1, mem_rows, HEAD_DIM), tq=512)
    xp = _mm(o[0], w_co_b, res=xp)
    q = _mm(xs, w_cq_b, gain=norm_cross[lyr]).reshape(b, l, MEM_WIDTH)
    o = _cross_attention(q, cache_mem_k[lyr].reshape(b, mem_rows, HEAD_DIM),
                         cache_mem_v[lyr].reshape(b, mem_rows, HEAD_DIM), tq=l)
    xs = _mm(o.reshape(b * l, MEM_WIDTH), w_co_b, res=xs)

    x_all = jnp.concatenate([xp, xs], axis=0)
    logits, hn = _mm(x_all, router_w[lyr].astype(BF16), gain=norm_ffn[lyr], emit_xn=True)
    yb, slot, gates, shared = _moe(hn, logits, router_bias[lyr], w_gate[lyr], w_up[lyr], w_down[lyr],
                                   ws_gate[lyr], ws_up[lyr], ws_down[lyr])
    y_all = _combine_final_norm(x_all, yb, slot, gates, shared, norm_final)

    y_prompt = y_all[:t].reshape(1, t, d)
    y_sample = y_all[t:].reshape(b, l, d)
    return (y_prompt, y_sample,
            s_ret_p[None, None],
            sk_p.reshape(1, 1, t, N_SB_HEADS, HEAD_DIM), sv_p.reshape(1, 1, t, N_SB_HEADS, HEAD_DIM),
            mk_p.reshape(1, 1, n_mem, N_MEM_HEADS, HEAD_DIM), mv_p.reshape(1, 1, n_mem, N_MEM_HEADS, HEAD_DIM),
            s_ret_s[None],
            sk_s.reshape(1, b, l, N_SB_HEADS, HEAD_DIM), sv_s.reshape(1, b, l, N_SB_HEADS, HEAD_DIM))
```

```python
import functools

import jax
import jax.numpy as jnp
import numpy as np
from jax import lax
from jax.experimental import pallas as pl
from jax.experimental.pallas import tpu as pltpu

F32 = jnp.float32
BF16 = jnp.bfloat16

HEAD_DIM = 128
N_RET_HEADS = 8
N_SB_HEADS = 8
RET_WIDTH = N_RET_HEADS * HEAD_DIM
SB_WIDTH = N_SB_HEADS * HEAD_DIM
RET_CHUNK = 128
PAGE_SIZE = 128
ROPE_BASE = 10000.0
N_MEM_HEADS = 4
MEM_WIDTH = N_MEM_HEADS * HEAD_DIM
N_EXPERTS = 64
TOP_K = 8
ROUTED_SCALE = 2.5
RMS_EPS = 1e-6
NORM_EPS = 1e-6
QK_SCALE = HEAD_DIM ** -0.5

VMEM_LIMIT_BYTES = 56 * 1024 * 1024
SB_KEYS = 256
SB_QUERIES = 512
SB_PAGES_PER_STEP = 8
EXPERT_ROWS = 256
COMBINE_ROWS = 128
SB_CHUNK = 128
RET_REQUESTS_PER_STEP = 4
LOG2E = 1.4426950408889634

NT_DIMS = (((1,), (1,)), ((), ()))
TN_DIMS = (((0,), (0,)), ((), ()))


def _params(*sem):
    return pltpu.CompilerParams(dimension_semantics=sem, vmem_limit_bytes=VMEM_LIMIT_BYTES)


def _sigmoid(x):
    return 1.0 / (1.0 + jnp.exp(-x))


def _silu(x):
    return x * _sigmoid(x)


def _mm_body(*refs, has_norm, has_res, emit_xn):
    it = iter(refs)
    x_ref = next(it)
    g_ref = next(it) if has_norm else None
    w_ref = next(it)
    r_ref = next(it) if has_res else None
    o_ref = next(it)
    xo_ref = next(it) if emit_xn else None
    xn_ref = next(it)

    @pl.when(pl.program_id(1) == 0)
    def _():
        x = x_ref[...].astype(F32)
        if has_norm:
            x = x * lax.rsqrt(jnp.mean(x * x, axis=-1, keepdims=True) + RMS_EPS)
            x = x * g_ref[...]
        xb = x.astype(BF16)
        xn_ref[...] = xb
        if emit_xn:
            xo_ref[...] = x

    acc = jnp.dot(xn_ref[...], w_ref[...], preferred_element_type=F32)
    if has_res:
        acc = r_ref[...] + acc
    o_ref[...] = acc


def _mm(x, w, gain=None, res=None, emit_xn=False, tm=1024, tn=1024):
    m, k = x.shape
    n = w.shape[1]
    tm = min(tm, m)
    tn = min(tn, n)
    assert m % tm == 0 and n % tn == 0
    in_specs = [pl.BlockSpec((tm, k), lambda i, j: (i, 0))]
    args = [x]
    if gain is not None:
        in_specs.append(pl.BlockSpec((1, k), lambda i, j: (0, 0)))
        args.append(gain.reshape(1, k).astype(F32))
    in_specs.append(pl.BlockSpec((k, tn), lambda i, j: (0, j)))
    args.append(w)
    if res is not None:
        in_specs.append(pl.BlockSpec((tm, tn), lambda i, j: (i, j)))
        args.append(res)
    out_shape = [jax.ShapeDtypeStruct((m, n), F32)]
    out_specs = [pl.BlockSpec((tm, tn), lambda i, j: (i, j))]
    if emit_xn:
        out_shape.append(jax.ShapeDtypeStruct((m, k), F32))
        out_specs.append(pl.BlockSpec((tm, k), lambda i, j: (i, 0)))
    outs = pl.pallas_call(
        functools.partial(_mm_body, has_norm=gain is not None, has_res=res is not None, emit_xn=emit_xn),
        grid=(m // tm, n // tn),
        in_specs=in_specs,
        out_specs=out_specs,
        out_shape=out_shape,
        scratch_shapes=[pltpu.VMEM((tm, k), BF16)],
        compiler_params=_params("parallel", "arbitrary"),
        name="norm_matmul",
    )(*args)
    return outs if emit_xn else outs[0]


def _rotary_tables(pos):
    half = HEAD_DIM // 2
    inv_freq = np.float32(ROPE_BASE) ** (-np.arange(half, dtype=np.float32) / np.float32(half))
    ang = pos.astype(np.float32)[:, None] * inv_freq[None, :]
    cos, sin = np.cos(ang), np.sin(ang)
    return (jnp.asarray(np.concatenate([cos, cos], axis=-1), F32),
            jnp.asarray(np.concatenate([-sin, sin], axis=-1), F32))


def _retention_tables(length):
    f32 = np.float32
    log_g = np.log1p(-np.exp2(f32(-5.0) - np.arange(N_RET_HEADS, dtype=f32)))
    i = np.arange(length, dtype=f32)
    diff = i[:, None] - i[None, :]
    decay = np.where(diff >= 0, np.exp(np.maximum(diff, f32(0.0))[None] * log_g[:, None, None]), f32(0.0))
    dq = np.exp((i[None, :] + f32(1.0)) * log_g[:, None])
    dk = np.exp((f32(length) - f32(1.0) - i)[None, :] * log_g[:, None])
    ds = np.exp(f32(length) * log_g)
    lanes = (N_RET_HEADS, length, HEAD_DIM)
    tables = (decay, np.broadcast_to(dq[:, :, None], lanes), np.broadcast_to(dk[:, :, None], lanes),
              np.broadcast_to(ds[:, None, None], (N_RET_HEADS, 8, HEAD_DIM)))
    return tuple(jnp.asarray(np.ascontiguousarray(a), F32) for a in tables)


def _rotate(x, cos, sin_signed):
    return x * cos + pltpu.roll(x, HEAD_DIM // 2, 1) * sin_signed


def _retention_step(q, k, v, s, decay, dq, dk, ds):
    qb, kb, vb = q.astype(BF16), k.astype(BF16), v.astype(BF16)
    scores = lax.dot_general(qb, kb, NT_DIMS, preferred_element_type=F32) * decay
    o = jnp.dot(scores.astype(BF16), vb, preferred_element_type=F32)
    o = o + jnp.dot((q * dq).astype(BF16), s.astype(BF16), preferred_element_type=F32)
    kd = (k * dk).astype(BF16)
    s_new = ds * s + lax.dot_general(kd, vb, TN_DIMS, preferred_element_type=F32)
    return o, s_new


def _gated_layernorm(o, gn, gate):
    c = o - jnp.mean(o, axis=-1, keepdims=True)
    y = c * lax.rsqrt(jnp.mean(c * c, axis=-1, keepdims=True) + NORM_EPS)
    return (y * gn) * _silu(gate)


def _ret_prompt_body(q_ref, k_ref, v_ref, g_ref, cos_ref, sin_ref, decay_ref, dq_ref, dk_ref, ds_ref, gn_ref,
                     o_ref, sfin_ref, s_ref):
    c = pl.program_id(0)

    @pl.when(c == 0)
    def _():
        s_ref[...] = jnp.zeros_like(s_ref)

    cos, sin = cos_ref[...], sin_ref[...]
    for h in range(N_RET_HEADS):
        cols = slice(h * HEAD_DIM, (h + 1) * HEAD_DIM)
        q = _rotate(q_ref[:, cols], cos, sin)
        k = _rotate(k_ref[:, cols], cos, sin) * QK_SCALE
        o, s_new = _retention_step(q, k, v_ref[:, cols], s_ref[h], decay_ref[h], dq_ref[h], dk_ref[h],
                                   ds_ref[h, 0:1, :])
        s_ref[h] = s_new
        o_ref[:, cols] = _gated_layernorm(o, gn_ref[:, cols], g_ref[:, cols])

    @pl.when(c == pl.num_programs(0) - 1)
    def _():
        sfin_ref[...] = s_ref[...]


def _retention_prompt(proj, pos, gn):
    t = proj.shape[0]
    cos, sin = _rotary_tables(pos)
    decay, dq, dk, ds = _retention_tables(RET_CHUNK)
    segment = lambda j: pl.BlockSpec((RET_CHUNK, RET_WIDTH), lambda c: (c, j))
    rot = pl.BlockSpec((RET_CHUNK, HEAD_DIM), lambda c: (c, 0))
    full = lambda a: pl.BlockSpec(a.shape, lambda c: (0,) * a.ndim)
    gn2 = gn.reshape(1, RET_WIDTH)
    state = jax.ShapeDtypeStruct((N_RET_HEADS, HEAD_DIM, HEAD_DIM), F32)
    return pl.pallas_call(
        _ret_prompt_body,
        grid=(t // RET_CHUNK,),
        in_specs=[segment(0), segment(1), segment(2), segment(3), rot, rot, full(decay), full(dq), full(dk), full(ds),
                  full(gn2)],
        out_specs=[segment(0), pl.BlockSpec(state.shape, lambda c: (0, 0, 0))],
        out_shape=[jax.ShapeDtypeStruct((t, RET_WIDTH), F32), state],
        scratch_shapes=[pltpu.VMEM(state.shape, F32)],
        compiler_params=_params("arbitrary"),
        name="retention_prompt",
    )(proj, proj, proj, proj, cos, sin, decay, dq, dk, ds, gn2)


def _ret_sample_body(q_ref, k_ref, v_ref, g_ref, st_ref, cos_ref, sin_ref, decay_ref, dq_ref, dk_ref, ds_ref, gn_ref,
                     o_ref, snew_ref):
    cos, sin = cos_ref[...], sin_ref[...]
    for r in range(q_ref.shape[0]):
        for h in range(N_RET_HEADS):
            cols = slice(h * HEAD_DIM, (h + 1) * HEAD_DIM)
            q = _rotate(q_ref[r, :, cols], cos, sin)
            k = _rotate(k_ref[r, :, cols], cos, sin) * QK_SCALE
            o, s_new = _retention_step(q, k, v_ref[r, :, cols], st_ref[r, h], decay_ref[h], dq_ref[h], dk_ref[h],
                                       ds_ref[h, 0:1, :])
            snew_ref[r, h] = s_new
            o_ref[r, :, cols] = _gated_layernorm(o, gn_ref[:, cols], g_ref[r, :, cols])


def _retention_sample(proj, state, pos, gn):
    b, l, _ = proj.shape
    cos, sin = _rotary_tables(pos)
    decay, dq, dk, ds = _retention_tables(l)
    per_step = RET_REQUESTS_PER_STEP
    assert b % per_step == 0
    segment = lambda j: pl.BlockSpec((per_step, l, RET_WIDTH), lambda i: (i, 0, j))
    st = pl.BlockSpec((per_step, N_RET_HEADS, HEAD_DIM, HEAD_DIM), lambda i: (i, 0, 0, 0))
    full = lambda a: pl.BlockSpec(a.shape, lambda i: (0,) * a.ndim)
    gn2 = gn.reshape(1, RET_WIDTH)
    return pl.pallas_call(
        _ret_sample_body,
        grid=(b // per_step,),
        in_specs=[segment(0), segment(1), segment(2), segment(3), st, full(cos), full(sin), full(decay), full(dq),
                  full(dk), full(ds), full(gn2)],
        out_specs=[segment(0), st],
        out_shape=[jax.ShapeDtypeStruct((b, l, RET_WIDTH), F32), jax.ShapeDtypeStruct(state.shape, F32)],
        compiler_params=_params("parallel"),
        name="retention_sample",
    )(proj, proj, proj, proj, state, cos, sin, decay, dq, dk, ds, gn2)


def _suffix_matrix(n):
    j = lax.broadcasted_iota(jnp.int32, (n, n), 0)
    s = lax.broadcasted_iota(jnp.int32, (n, n), 1)
    return jnp.where(j > s, 1.0, 0.0).astype(BF16)


def _neg_abs(x):
    bits = lax.bitcast_convert_type(x, jnp.uint32) | jnp.uint32(0x80000000)
    return lax.bitcast_convert_type(bits, F32)


def _sb_block_log2(z2, tri, mask):
    rows, keys = z2.shape
    chunk = min(rows, SB_CHUNK)
    suffix = tri.shape[1]
    log2_w, dropped = [], []
    for c in range(0, rows, chunk):
        parts, later = [], None
        for s in reversed(range(0, keys, suffix)):
            zc = z2[c:c + chunk, s:s + suffix]
            neg_log_keep = jnp.maximum(zc, 0.0) + jnp.log2(1.0 + jnp.exp2(_neg_abs(zc)))
            log_beta = zc - neg_log_keep
            if mask is not None:
                neg_log_keep = jnp.where(mask[c:c + chunk, s:s + suffix], neg_log_keep, 0.0)
            between = jnp.dot(neg_log_keep.astype(BF16), tri, preferred_element_type=F32)
            total = jnp.sum(neg_log_keep, axis=-1, keepdims=True)
            if later is None:
                parts.append(log_beta - between)
                later = total
            else:
                parts.append(log_beta - between - later)
                later = later + total
        log2_w.append(jnp.concatenate(parts[::-1], axis=1))
        dropped.append(later)
    return jnp.concatenate(log2_w, axis=0), jnp.concatenate(dropped, axis=0)


def _head_rmsnorm(o, gn):
    return (o * lax.rsqrt(jnp.mean(o * o, axis=-1, keepdims=True) + RMS_EPS)) * gn


def _sb_prompt_body(bias_ref, q_ref, k_ref, v_ref, gn_ref, o_ref, ko_ref, vo_ref, kb_ref, vb_ref, tri_ref):
    h = pl.program_id(0)
    i = pl.program_id(1)

    @pl.when(i == 0)
    def _():
        k, v = k_ref[...], v_ref[...]
        ko_ref[...] = k
        vo_ref[...] = v
        kb_ref[...] = k.astype(BF16)
        vb_ref[...] = v.astype(BF16)
        tri_ref[...] = _suffix_matrix(SB_KEYS)

    q = (q_ref[...] * (QK_SCALE * LOG2E)).astype(BF16)
    bias = bias_ref[h] * LOG2E
    tri = tri_ref[...]

    def key_rows(j):
        return pl.ds(pl.multiple_of(j * SB_KEYS, SB_KEYS), SB_KEYS)

    def logits(j):
        return lax.dot_general(q, kb_ref[key_rows(j), :], NT_DIMS, preferred_element_type=F32) + bias

    def weighted_values(log2_w, j, mask=None):
        a = jnp.exp2(log2_w)
        if mask is not None:
            a = jnp.where(mask, a, 0.0)
        return jnp.dot(a.astype(BF16), vb_ref[key_rows(j), :], preferred_element_type=F32)

    ratio = SB_QUERIES // SB_KEYS
    r = lax.broadcasted_iota(jnp.int32, (SB_QUERIES, SB_KEYS), 0)
    c = lax.broadcasted_iota(jnp.int32, (SB_QUERIES, SB_KEYS), 1)
    acc = jnp.zeros((SB_QUERIES, HEAD_DIM), F32)
    drop = jnp.zeros((SB_QUERIES, 1), F32)
    for d in reversed(range(ratio)):
        mask = c + d * SB_KEYS < r
        log2_w, dropped = _sb_block_log2(logits(i * ratio + d), tri, mask)
        acc = acc + weighted_values(log2_w - drop, i * ratio + d, mask)
        drop = drop + dropped

    n_old = i * ratio

    def step(n, state):
        z_prev, log2_w_prev, acc, drop = state
        z_new = logits(jnp.maximum(n_old - 1 - n, 0))
        log2_w, dropped = _sb_block_log2(z_prev, tri, None)
        out = weighted_values(log2_w_prev, jnp.clip(n_old + 1 - n, 0, n_old))
        return z_new, log2_w - drop, acc + out, drop + dropped

    idle = jnp.full((SB_QUERIES, SB_KEYS), -1e30, F32)
    _, _, acc, _ = lax.fori_loop(0, jnp.where(n_old > 0, n_old + 2, 0), step, (idle, idle, acc, drop))
    o_ref[...] = _head_rmsnorm(acc, gn_ref[...])


def _sb_prompt(proj, bias, gn):
    t, width = proj.shape
    assert t % SB_QUERIES == 0 and SB_QUERIES % SB_KEYS == 0
    q0, k0, v0 = [(width - n * SB_WIDTH) // HEAD_DIM for n in (3, 2, 1)]
    qspec = pl.BlockSpec((SB_QUERIES, HEAD_DIM), lambda h, i, b: (i, q0 + h))
    kspec = pl.BlockSpec((t, HEAD_DIM), lambda h, i, b: (0, k0 + h))
    vspec = pl.BlockSpec((t, HEAD_DIM), lambda h, i, b: (0, v0 + h))
    return pl.pallas_call(
        _sb_prompt_body,
        grid_spec=pltpu.PrefetchScalarGridSpec(
            num_scalar_prefetch=1,
            grid=(N_SB_HEADS, t // SB_QUERIES),
            in_specs=[qspec, kspec, vspec, pl.BlockSpec((1, HEAD_DIM), lambda h, i, b: (0, h))],
            out_specs=[pl.BlockSpec((SB_QUERIES, HEAD_DIM), lambda h, i, b: (i, h)),
                       pl.BlockSpec((t, HEAD_DIM), lambda h, i, b: (0, h)),
                       pl.BlockSpec((t, HEAD_DIM), lambda h, i, b: (0, h))],
            scratch_shapes=[pltpu.VMEM((t, HEAD_DIM), BF16), pltpu.VMEM((t, HEAD_DIM), BF16),
                            pltpu.VMEM((SB_KEYS, SB_KEYS), BF16)],
        ),
        out_shape=[jax.ShapeDtypeStruct((t, SB_WIDTH), F32)] * 3,
        compiler_params=_params("parallel", "arbitrary"),
        name="stick_breaking_prompt",
    )(bias.astype(F32), proj, proj, proj, gn.reshape(1, SB_WIDTH))


def _sb_sample_body(pt_ref, bias_ref, q_ref, kn_ref, vn_ref, *rest):
    kp_refs = rest[:SB_PAGES_PER_STEP]
    vp_refs = rest[SB_PAGES_PER_STEP:2 * SB_PAGES_PER_STEP]
    gn_ref, o_ref, acc_ref, drop_ref = rest[2 * SB_PAGES_PER_STEP:]
    step = pl.program_id(1)
    l = q_ref.shape[1]
    rows = N_SB_HEADS * l
    tri = _suffix_matrix(PAGE_SIZE)
    bias = jnp.concatenate([jnp.full((l, 1), bias_ref[h] * LOG2E, F32) for h in range(N_SB_HEADS)], axis=0)
    qs = [(q_ref[0, :, h * HEAD_DIM:(h + 1) * HEAD_DIM] * (QK_SCALE * LOG2E)).astype(BF16)
          for h in range(N_SB_HEADS)]

    def logits(keys):
        z = [lax.dot_general(qs[h], keys[h], NT_DIMS, preferred_element_type=F32) for h in range(N_SB_HEADS)]
        return jnp.concatenate(z, axis=0) + bias

    def weighted_values(a, values):
        out = [jnp.dot(a[h * l:(h + 1) * l, :].astype(BF16), values[h], preferred_element_type=F32)
               for h in range(N_SB_HEADS)]
        return jnp.concatenate(out, axis=0)

    @pl.when(step == 0)
    def _():
        pad = jnp.zeros((PAGE_SIZE - l, HEAD_DIM), F32)
        kn = [jnp.concatenate([kn_ref[0, :, h * HEAD_DIM:(h + 1) * HEAD_DIM], pad], axis=0).astype(BF16)
              for h in range(N_SB_HEADS)]
        vn = [jnp.concatenate([vn_ref[0, :, h * HEAD_DIM:(h + 1) * HEAD_DIM], pad], axis=0).astype(BF16)
              for h in range(N_SB_HEADS)]
        qi = lax.broadcasted_iota(jnp.int32, (rows, PAGE_SIZE), 0) % l
        kj = lax.broadcasted_iota(jnp.int32, (rows, PAGE_SIZE), 1)
        mask = kj < qi
        log2_w, dropped = _sb_block_log2(logits(kn), tri, mask)
        acc_ref[...] = weighted_values(jnp.where(mask, jnp.exp2(log2_w), 0.0), vn)
        drop_ref[...] = dropped

    head_rows = lambda h: pl.ds(h, PAGE_SIZE, stride=N_SB_HEADS)
    z = jnp.concatenate([logits([kp_ref[head_rows(h), :].astype(BF16) for h in range(N_SB_HEADS)])
                         for kp_ref in kp_refs], axis=0)
    log2_w, dropped = _sb_block_log2(z, tri, None)
    drop = drop_ref[...]
    acc = acc_ref[...]
    for n, vp_ref in enumerate(vp_refs):
        page = slice(n * rows, (n + 1) * rows)
        a = jnp.exp2(log2_w[page, :] - drop)
        acc = acc + weighted_values(a, [vp_ref[head_rows(h), :].astype(BF16) for h in range(N_SB_HEADS)])
        drop = drop + dropped[page, :]
    acc_ref[...] = acc
    drop_ref[...] = drop

    @pl.when(step == pl.num_programs(1) - 1)
    def _():
        for h in range(N_SB_HEADS):
            cols = slice(h * HEAD_DIM, (h + 1) * HEAD_DIM)
            o_ref[0, :, cols] = _head_rmsnorm(acc[h * l:(h + 1) * l, :], gn_ref[:, cols])


def _sb_sample(proj, cache_k, cache_v, page_table, bias, gn):
    b, l, width = proj.shape
    n_pages = page_table.shape[1]
    assert n_pages % SB_PAGES_PER_STEP == 0
    n_phys = cache_k.shape[0]
    page_rows = PAGE_SIZE * N_SB_HEADS
    cache_k = cache_k.reshape(n_phys, page_rows, HEAD_DIM)
    cache_v = cache_v.reshape(n_phys, page_rows, HEAD_DIM)
    segment = lambda j: pl.BlockSpec((1, l, SB_WIDTH), lambda i, s, pt, bs: (i, 0, j))
    last = width // SB_WIDTH - 1

    def page(n):
        return pl.BlockSpec((None, page_rows, HEAD_DIM),
                            lambda i, s, pt, bs: (pt[i, n_pages - 1 - (s * SB_PAGES_PER_STEP + n)], 0, 0))

    pages = [page(n) for n in range(SB_PAGES_PER_STEP)]
    return pl.pallas_call(
        _sb_sample_body,
        grid_spec=pltpu.PrefetchScalarGridSpec(
            num_scalar_prefetch=2,
            grid=(b, n_pages // SB_PAGES_PER_STEP),
            in_specs=[segment(last - 2), segment(last - 1), segment(last)] + pages + pages
                     + [pl.BlockSpec((1, SB_WIDTH), lambda i, s, pt, bs: (0, 0))],
            out_specs=segment(0),
            scratch_shapes=[pltpu.VMEM((N_SB_HEADS * l, HEAD_DIM), F32), pltpu.VMEM((N_SB_HEADS * l, 1), F32)],
        ),
        out_shape=jax.ShapeDtypeStruct((b, l, SB_WIDTH), F32),
        compiler_params=_params("parallel", "arbitrary"),
        name="stick_breaking_sample",
    )(page_table, bias.astype(F32), proj, proj, proj, *([cache_k] * SB_PAGES_PER_STEP),
      *([cache_v] * SB_PAGES_PER_STEP), gn.reshape(1, SB_WIDTH))


def _cross_body(q_ref, k_ref, v_ref, o_ref):
    n_mem = k_ref.shape[0] // N_MEM_HEADS
    for h in range(N_MEM_HEADS):
        cols = slice(h * HEAD_DIM, (h + 1) * HEAD_DIM)
        head_rows = pl.ds(h, n_mem, stride=N_MEM_HEADS)
        q = q_ref[0, :, cols].astype(BF16)
        k = k_ref[head_rows, :].astype(BF16)
        v = v_ref[head_rows, :].astype(BF16)
        s = lax.dot_general(q, k, NT_DIMS, preferred_element_type=F32) * QK_SCALE
        e = jnp.exp(s - jnp.max(s, axis=-1, keepdims=True))
        p = e / jnp.sum(e, axis=-1, keepdims=True)
        o_ref[0, :, cols] = jnp.dot(p.astype(BF16), v, preferred_element_type=F32)


def _cross_attention(q, mem_k, mem_v, tq):
    b, t, _ = q.shape
    m = mem_k.shape[1]
    tq = min(tq, t)
    qspec = pl.BlockSpec((1, tq, MEM_WIDTH), lambda i, j: (i, j, 0))
    mspec = pl.BlockSpec((None, m, HEAD_DIM), lambda i, j: (i, 0, 0))
    return pl.pallas_call(
        _cross_body,
        grid=(b, t // tq),
        in_specs=[qspec, mspec, mspec],
        out_specs=qspec,
        out_shape=jax.ShapeDtypeStruct(q.shape, F32),
        compiler_params=_params("parallel", "arbitrary"),
        name="memory_cross_attention",
    )(q, mem_k, mem_v)


def _swiglu(x, wg_b, wu_b, wd_b):
    g = jnp.dot(x, wg_b[...], preferred_element_type=F32)
    u = jnp.dot(x, wu_b[...], preferred_element_type=F32)
    return jnp.dot((_silu(g) * u).astype(BF16), wd_b[...], preferred_element_type=F32)


def _cast_weights(wg_ref, wu_ref, wd_ref, wg_b, wu_b, wd_b):
    wg_b[...] = wg_ref[0].astype(BF16)
    wu_b[...] = wu_ref[0].astype(BF16)
    wd_b[...] = wd_ref[0].astype(BF16)


def _shared_body(x_ref, wg_ref, wu_ref, wd_ref, o_ref, wg_b, wu_b, wd_b):
    @pl.when(pl.program_id(0) == 0)
    def _():
        _cast_weights(wg_ref, wu_ref, wd_ref, wg_b, wu_b, wd_b)

    o_ref[...] = _swiglu(x_ref[...].astype(BF16), wg_b, wu_b, wd_b)


def _shared_mlp(x, w_gate, w_up, w_down, rows):
    r, d = x.shape
    f = w_gate.shape[2]
    wspec = lambda shape: pl.BlockSpec((1,) + shape, lambda i: (0, 0, 0))
    return pl.pallas_call(
        _shared_body,
        grid=(r // rows,),
        in_specs=[pl.BlockSpec((rows, d), lambda i: (i, 0)), wspec((d, f)), wspec((d, f)), wspec((f, d))],
        out_specs=pl.BlockSpec((rows, d), lambda i: (i, 0)),
        out_shape=jax.ShapeDtypeStruct((r, d), F32),
        scratch_shapes=[pltpu.VMEM((d, f), BF16), pltpu.VMEM((d, f), BF16), pltpu.VMEM((f, d), BF16)],
        compiler_params=_params("arbitrary"),
        name="shared_mlp",
    )(x, w_gate, w_up, w_down)


def _expert_body(be_ref, used_ref, shift_ref, tok_ref, x_hbm, wg_ref, wu_ref, wd_ref, o_ref, wg_b, wu_b, wd_b,
                 xbuf0, xbuf1, sem):
    b = pl.program_id(0)
    rows = o_ref.shape[0]
    n_used = used_ref[0]
    bufs = (xbuf0, xbuf1)
    last_entry = tok_ref.shape[0] - 1

    def row_token(first_entry, r):
        return tok_ref[jnp.minimum(first_entry + r, last_entry)]

    def first_entry(blk):
        return blk * rows - shift_ref[blk]

    def wait_gather(parity):
        pltpu.make_async_copy(x_hbm.at[pl.ds(0, rows), :], bufs[parity], sem.at[parity]).wait()

    @pl.when(b == 0)
    def _():
        def row(r, carry):
            src = x_hbm.at[pl.ds(row_token(first_entry(0), r), 1), :]
            pltpu.make_async_copy(src, xbuf0.at[pl.ds(r, 1), :], sem.at[0]).start()
            return carry

        lax.fori_loop(0, rows, row, 0, unroll=8)

    @pl.when(jnp.logical_and(b < n_used, jnp.logical_or(b == 0, be_ref[b] != be_ref[jnp.maximum(b - 1, 0)])))
    def _():
        _cast_weights(wg_ref, wu_ref, wd_ref, wg_b, wu_b, wd_b)

    def run(parity):
        nxt = first_entry(jnp.minimum(b + 1, n_used - 1))
        wait_gather(parity)
        x = bufs[parity][...].astype(BF16)
        for r in range(rows):
            src = x_hbm.at[pl.ds(row_token(nxt, r), 1), :]
            pltpu.make_async_copy(src, bufs[1 - parity].at[pl.ds(r, 1), :], sem.at[1 - parity]).start()
        o_ref[...] = _swiglu(x, wg_b, wu_b, wd_b)

        @pl.when(b == n_used - 1)
        def _():
            wait_gather(1 - parity)

    for parity in (0, 1):
        pl.when(jnp.logical_and(b < n_used, b % 2 == parity))(functools.partial(run, parity))

    @pl.when(b >= n_used)
    def _():
        o_ref[...] = jnp.zeros_like(o_ref)


def _expert_mlp(x, slot_token, block_shift, block_expert, n_used, w_gate, w_up, w_down, rows):
    d = x.shape[1]
    f = w_gate.shape[2]
    n_blocks = block_expert.shape[0]
    wspec = lambda shape: pl.BlockSpec((1,) + shape, lambda i, be, nu, sh, tok: (be[i], 0, 0))
    return pl.pallas_call(
        _expert_body,
        grid_spec=pltpu.PrefetchScalarGridSpec(
            num_scalar_prefetch=4,
            grid=(n_blocks,),
            in_specs=[pl.BlockSpec(memory_space=pl.ANY), wspec((d, f)), wspec((d, f)), wspec((f, d))],
            out_specs=pl.BlockSpec((rows, d), lambda i, be, nu, sh, tok: (i, 0)),
            scratch_shapes=[pltpu.VMEM((d, f), BF16), pltpu.VMEM((d, f), BF16), pltpu.VMEM((f, d), BF16),
                            pltpu.VMEM((rows, d), F32), pltpu.VMEM((rows, d), F32), pltpu.SemaphoreType.DMA((2,))],
        ),
        out_shape=jax.ShapeDtypeStruct((n_blocks * rows, d), F32),
        compiler_params=_params("arbitrary"),
        name="expert_mlp",
    )(block_expert, n_used, block_shift, slot_token, x, w_gate, w_up, w_down)


def _combine_body(slot_ref, x_ref, sh_ref, gate_ref, g_ref, yb_hbm, o_ref, buf, sem):
    i = pl.program_id(0)
    tm = x_ref.shape[0]

    def start_gather(tile, b):
        base = tile * (tm * TOP_K)

        def token(r, carry):
            for k in range(TOP_K):
                src = yb_hbm.at[pl.ds(slot_ref[base + r * TOP_K + k], 1), :]
                pltpu.make_async_copy(src, buf.at[b, pl.ds(k * tm + r, 1), :], sem.at[b]).start()
            return carry

        lax.fori_loop(0, tm, token, 0)

    @pl.when(i == 0)
    def _():
        start_gather(0, 0)

    @pl.when(i + 1 < pl.num_programs(0))
    def _():
        start_gather(i + 1, (i + 1) % 2)

    b = i % 2
    pltpu.make_async_copy(yb_hbm.at[pl.ds(0, TOP_K * tm), :], buf.at[b], sem.at[b]).wait()
    routed = buf[b, 0:tm, :] * gate_ref[:, 0:1]
    for k in range(1, TOP_K):
        routed = routed + buf[b, k * tm:(k + 1) * tm, :] * gate_ref[:, k:k + 1]
    x = x_ref[...] + (routed + sh_ref[...])
    o_ref[...] = (x * lax.rsqrt(jnp.mean(x * x, axis=-1, keepdims=True) + RMS_EPS)) * g_ref[...]


def _combine_final_norm(x, yb, slot, gates, shared, gain):
    m, d = x.shape
    tm = COMBINE_ROWS
    assert m % tm == 0 and yb.shape[0] >= TOP_K * tm
    row = pl.BlockSpec((tm, d), lambda i, s: (i, 0))
    return pl.pallas_call(
        _combine_body,
        grid_spec=pltpu.PrefetchScalarGridSpec(
            num_scalar_prefetch=1,
            grid=(m // tm,),
            in_specs=[row, row, pl.BlockSpec((tm, TOP_K), lambda i, s: (i, 0)),
                      pl.BlockSpec((1, d), lambda i, s: (0, 0)), pl.BlockSpec(memory_space=pl.ANY)],
            out_specs=row,
            scratch_shapes=[pltpu.VMEM((2, TOP_K * tm, d), F32), pltpu.SemaphoreType.DMA((2,))],
        ),
        out_shape=jax.ShapeDtypeStruct((m, d), F32),
        compiler_params=_params("arbitrary"),
        name="combine_final_norm",
    )(slot.reshape(-1), x, shared, gates, gain.reshape(1, d), yb)


def _route_body(lg_ref, rb_ref, idx_ref, gate_ref, rank_ref, cnt_ref, seen_ref):
    @pl.when(pl.program_id(0) == 0)
    def _():
        seen_ref[...] = jnp.zeros_like(seen_ref)

    tm, n_exp = lg_ref.shape
    scores = _sigmoid(lg_ref[...])
    biased = scores + rb_ref[...]
    expert = lax.broadcasted_iota(jnp.int32, (tm, n_exp), 1).astype(F32)
    chosen = jnp.zeros((tm, n_exp), F32)
    picks = []
    for _ in range(TOP_K):
        best = jnp.max(biased, axis=-1, keepdims=True)
        first = jnp.min(jnp.where(biased == best, expert, float(n_exp)), axis=-1, keepdims=True)
        hit = expert == first
        picks.append((first, hit, jnp.sum(jnp.where(hit, scores, 0.0), axis=-1, keepdims=True)))
        biased = jnp.where(hit, -jnp.inf, biased)
        chosen = jnp.where(hit, 1.0, chosen)

    r = lax.broadcasted_iota(jnp.int32, (tm, tm), 0)
    c = lax.broadcasted_iota(jnp.int32, (tm, tm), 1)
    earlier = jnp.where(c < r, 1.0, 0.0).astype(BF16)
    rank_all = jnp.dot(earlier, chosen.astype(BF16), preferred_element_type=F32) + seen_ref[...]

    total = picks[0][2]
    for _, _, sel in picks[1:]:
        total = total + sel
    col = lax.broadcasted_iota(jnp.int32, (tm, TOP_K), 1)
    idx = jnp.zeros((tm, TOP_K), F32)
    gate = jnp.zeros((tm, TOP_K), F32)
    rank = jnp.zeros((tm, TOP_K), F32)
    for k, (first, hit, sel) in enumerate(picks):
        idx = jnp.where(col == k, first, idx)
        gate = jnp.where(col == k, sel / total * ROUTED_SCALE, gate)
        rank = jnp.where(col == k, jnp.sum(jnp.where(hit, rank_all, 0.0), axis=-1, keepdims=True), rank)
    idx_ref[...] = idx.astype(jnp.int32)
    gate_ref[...] = gate
    rank_ref[...] = rank.astype(jnp.int32)
    seen_ref[...] += jnp.sum(chosen, axis=0, keepdims=True)
    cnt_ref[...] = seen_ref[...].astype(jnp.int32)


def _route(logits, router_bias, tm=256):
    t, n_exp = logits.shape
    assert t % tm == 0
    row = lambda width: pl.BlockSpec((tm, width), lambda i: (i, 0))
    one = pl.BlockSpec((1, n_exp), lambda i: (0, 0))
    return pl.pallas_call(
        _route_body,
        grid=(t // tm,),
        in_specs=[row(n_exp), one],
        out_specs=[row(TOP_K), row(TOP_K), row(TOP_K), one],
        out_shape=[jax.ShapeDtypeStruct((t, TOP_K), jnp.int32), jax.ShapeDtypeStruct((t, TOP_K), F32),
                   jax.ShapeDtypeStruct((t, TOP_K), jnp.int32), jax.ShapeDtypeStruct((1, n_exp), jnp.int32)],
        scratch_shapes=[pltpu.VMEM((1, n_exp), F32)],
        compiler_params=_params("arbitrary"),
        name="router_topk",
    )(logits, router_bias.reshape(1, n_exp).astype(F32))


def _moe(hn, logits, router_bias, w_gate, w_up, w_down, ws_gate, ws_up, ws_down):
    t, d = hn.shape
    idx, gates, rank, counts = _route(logits, router_bias)

    padded = (counts[0] + EXPERT_ROWS - 1) // EXPERT_ROWS * EXPERT_ROWS
    pad_end = jnp.cumsum(padded)
    pad_start = pad_end - padded
    experts = jnp.arange(N_EXPERTS, dtype=jnp.int32)
    slot = jnp.sum(jnp.where(idx[:, :, None] == experts, pad_start, 0), axis=-1) + rank
    n_blocks = -(-(t * TOP_K) // EXPERT_ROWS) + N_EXPERTS
    block_start = jnp.arange(n_blocks, dtype=jnp.int32) * EXPERT_ROWS
    block_e = jnp.minimum(jnp.sum(pad_end[None, :] <= block_start[:, None], axis=1), N_EXPERTS - 1).astype(jnp.int32)
    n_used = (pad_end[-1:] // EXPERT_ROWS).astype(jnp.int32)
    _, tok_sorted = lax.sort_key_val(slot.reshape(-1), jnp.repeat(jnp.arange(t, dtype=jnp.int32), TOP_K))
    first = jnp.cumsum(counts[0]) - counts[0]
    block_shift = (pad_start - first)[block_e].astype(jnp.int32)
    yb = _expert_mlp(hn, tok_sorted, block_shift, block_e, n_used, w_gate, w_up, w_down, EXPERT_ROWS)
    shared = _shared_mlp(hn, ws_gate[None], ws_up[None], ws_down[None], EXPERT_ROWS)
    return yb, slot.astype(jnp.int32), gates, shared


def kernel(x_prompt, x_sample, mem_prompt, cache_sb_k, cache_sb_v, page_table, state_ret, cache_mem_k, cache_mem_v,
           norm_mix, w_in, ret_gn, sb_gn, sb_bias, w_out, norm_mem, w_ck, w_cv, norm_cross, w_cq, w_co, norm_ffn,
           router_w, router_bias, w_gate, w_up, w_down, ws_gate, ws_up, ws_down, norm_final):
    depth = w_in.shape[0]
    assert depth == 1 and x_prompt.shape[0] == 1
    _, t, d = x_prompt.shape
    b, l, _ = x_sample.shape
    n_pages = page_table.shape[1]
    lyr = 0
    xp = x_prompt.reshape(t, d)
    xs = x_sample.reshape(b * l, d)
    pos_p = np.arange(t)
    pos_s = n_pages * PAGE_SIZE + np.arange(l)

    w_in_b = w_in[lyr].astype(BF16)
    w_out_b = w_out[lyr].astype(BF16)
    w_cq_b = w_cq[lyr].astype(BF16)
    w_co_b = w_co[lyr].astype(BF16)

    mem = mem_prompt.reshape(-1, d)
    n_mem = mem.shape[0]
    mkv = _mm(mem, jnp.concatenate([w_ck[lyr], w_cv[lyr]], axis=1).astype(BF16), gain=norm_mem[lyr])
    mk_p, mv_p = mkv[:, :MEM_WIDTH], mkv[:, MEM_WIDTH:]

    proj = _mm(xp, w_in_b, gain=norm_mix[lyr])
    o_ret, s_ret_p = _retention_prompt(proj, pos_p, ret_gn[lyr])
    o_sb, sk_p, sv_p = _sb_prompt(proj, sb_bias[lyr], sb_gn[lyr])
    xp = _mm(jnp.concatenate([o_ret, o_sb], axis=-1), w_out_b, res=xp)

    proj = _mm(xs, w_in_b, gain=norm_mix[lyr]).reshape(b, l, -1)
    sk_s, sv_s = proj[:, :, -2 * SB_WIDTH:-SB_WIDTH], proj[:, :, -SB_WIDTH:]
    o_ret, s_ret_s = _retention_sample(proj, state_ret[lyr], pos_s, ret_gn[lyr])
    o_sb = _sb_sample(proj, cache_sb_k[lyr], cache_sb_v[lyr], page_table, sb_bias[lyr], sb_gn[lyr])
    xs = _mm(jnp.concatenate([o_ret, o_sb], axis=-1).reshape(b * l, -1), w_out_b, res=xs)

    q = _mm(xp, w_cq_b, gain=norm_cross[lyr])
    mem_rows = n_mem * N_MEM_HEADS
    o = _cross_attention(q[None], mk_p.reshape(1, mem_rows, HEAD_DIM), mv_p.reshape(1, mem_rows, HEAD_DIM), tq=512)
    xp = _mm(o[0], w_co_b, res=xp)
    q = _mm(xs, w_cq_b, gain=norm_cross[lyr]).reshape(b, l, MEM_WIDTH)
    o = _cross_attention(q, cache_mem_k[lyr].reshape(b, mem_rows, HEAD_DIM),
                         cache_mem_v[lyr].reshape(b, mem_rows, HEAD_DIM), tq=l)
    xs = _mm(o.reshape(b * l, MEM_WIDTH), w_co_b, res=xs)

    x_all = jnp.concatenate([xp, xs], axis=0)
    logits, hn = _mm(x_all, router_w[lyr].astype(BF16), gain=norm_ffn[lyr], emit_xn=True)
    yb, slot, gates, shared = _moe(hn, logits, router_bias[lyr], w_gate[lyr], w_up[lyr], w_down[lyr],
                                   ws_gate[lyr], ws_up[lyr], ws_down[lyr])
    y_all = _combine_final_norm(x_all, yb, slot, gates, shared, norm_final)

    y_prompt = y_all[:t].reshape(1, t, d)
    y_sample = y_all[t:].reshape(b, l, d)
    return (y_prompt, y_sample,
            s_ret_p[None, None],
            sk_p.reshape(1, 1, t, N_SB_HEADS, HEAD_DIM), sv_p.reshape(1, 1, t, N_SB_HEADS, HEAD_DIM),
            mk_p.reshape(1, 1, n_mem, N_MEM_HEADS, HEAD_DIM), mv_p.reshape(1, 1, n_mem, N_MEM_HEADS, HEAD_DIM),
            s_ret_s[None],
            sk_s.reshape(1, b, l, N_SB_HEADS, HEAD_DIM), sv_s.reshape(1, b, l, N_SB_HEADS, HEAD_DIM))
```

```python
import functools

import jax
import jax.numpy as jnp
import numpy as np
from jax import lax
from jax.experimental import pallas as pl
from jax.experimental.pallas import tpu as pltpu

F32 = jnp.float32
BF16 = jnp.bfloat16

HEAD_DIM = 128
N_RET_HEADS = 8
N_SB_HEADS = 8
RET_WIDTH = N_RET_HEADS * HEAD_DIM
SB_WIDTH = N_SB_HEADS * HEAD_DIM
RET_CHUNK = 128
PAGE_SIZE = 128
ROPE_BASE = 10000.0
N_MEM_HEADS = 4
MEM_WIDTH = N_MEM_HEADS * HEAD_DIM
N_EXPERTS = 64
TOP_K = 8
ROUTED_SCALE = 2.5
RMS_EPS = 1e-6
NORM_EPS = 1e-6
QK_SCALE = HEAD_DIM ** -0.5

VMEM_LIMIT_BYTES = 56 * 1024 * 1024
SB_KEYS = 256
SB_QUERIES = 512
SB_PAGES_PER_STEP = 8
EXPERT_ROWS = 256
COMBINE_ROWS = 128
SB_CHUNK = 128
RET_REQUESTS_PER_STEP = 4
LOG2E = 1.4426950408889634

NT_DIMS = (((1,), (1,)), ((), ()))
TN_DIMS = (((0,), (0,)), ((), ()))


def _params(*sem):
    return pltpu.CompilerParams(dimension_semantics=sem, vmem_limit_bytes=VMEM_LIMIT_BYTES)


def _sigmoid(x):
    return 1.0 / (1.0 + jnp.exp(-x))


def _silu(x):
    return x * _sigmoid(x)


def _token_major_store(ref, x):
    rows, d = x.shape
    pieces = d // HEAD_DIM
    for c in range(pieces):
        ref[pl.ds(c, rows, stride=pieces), :] = x[:, c * HEAD_DIM:(c + 1) * HEAD_DIM]


def _token_major_load(ref, rows, dtype):
    pieces = ref.shape[0] // rows
    return jnp.concatenate([ref[pl.ds(c, rows, stride=pieces), :].astype(dtype) for c in range(pieces)], axis=1)


def _mm_body(*refs, n_x, has_norm, has_res, emit_xn):
    it = iter(refs)
    x_refs = [next(it) for _ in range(n_x)]
    g_ref = next(it) if has_norm else None
    w_ref = next(it)
    r_ref = next(it) if has_res else None
    o_ref = next(it)
    xo_ref = next(it) if emit_xn else None
    xn_ref = next(it)

    @pl.when(pl.program_id(1) == 0)
    def _():
        x = x_refs[0][...] if n_x == 1 else jnp.concatenate([r[...] for r in x_refs], axis=1)
        if has_norm:
            x = x * lax.rsqrt(jnp.mean(x * x, axis=-1, keepdims=True) + RMS_EPS)
            x = x * g_ref[...]
        xn_ref[...] = x.astype(BF16)
        if emit_xn:
            _token_major_store(xo_ref, x)

    acc = jnp.dot(xn_ref[...], w_ref[...], preferred_element_type=F32)
    if has_res:
        acc = r_ref[...] + acc
    o_ref[...] = acc


def _mm(xs, w, gain=None, res=None, emit_xn=False, tm=1024, tn=1024):
    xs = xs if isinstance(xs, (list, tuple)) else [xs]
    m = xs[0].shape[0]
    k = sum(x.shape[1] for x in xs)
    n = w.shape[1]
    tm = min(tm, m)
    tn = min(tn, n)
    assert m % tm == 0 and n % tn == 0
    in_specs = [pl.BlockSpec((tm, x.shape[1]), lambda i, j: (i, 0)) for x in xs]
    args = list(xs)
    if gain is not None:
        in_specs.append(pl.BlockSpec((1, k), lambda i, j: (0, 0)))
        args.append(gain.reshape(1, k).astype(F32))
    in_specs.append(pl.BlockSpec((k, tn), lambda i, j: (0, j)))
    args.append(w)
    if res is not None:
        in_specs.append(pl.BlockSpec((tm, tn), lambda i, j: (i, j)))
        args.append(res)
    out_shape = [jax.ShapeDtypeStruct((m, n), F32)]
    out_specs = [pl.BlockSpec((tm, tn), lambda i, j: (i, j))]
    if emit_xn:
        pieces = k // HEAD_DIM
        out_shape.append(jax.ShapeDtypeStruct((m * pieces, HEAD_DIM), F32))
        out_specs.append(pl.BlockSpec((tm * pieces, HEAD_DIM), lambda i, j: (i, 0)))
    outs = pl.pallas_call(
        functools.partial(_mm_body, n_x=len(xs), has_norm=gain is not None, has_res=res is not None,
                          emit_xn=emit_xn),
        grid=(m // tm, n // tn),
        in_specs=in_specs,
        out_specs=out_specs,
        out_shape=out_shape,
        scratch_shapes=[pltpu.VMEM((tm, k), BF16)],
        compiler_params=_params("parallel", "arbitrary"),
        name="norm_matmul",
    )(*args)
    return outs if emit_xn else outs[0]


def _rotary_tables(pos):
    half = HEAD_DIM // 2
    inv_freq = np.float32(ROPE_BASE) ** (-np.arange(half, dtype=np.float32) / np.float32(half))
    ang = pos.astype(np.float32)[:, None] * inv_freq[None, :]
    cos, sin = np.cos(ang), np.sin(ang)
    return (jnp.asarray(np.concatenate([cos, cos], axis=-1), F32),
            jnp.asarray(np.concatenate([-sin, sin], axis=-1), F32))


def _retention_tables(length):
    f32 = np.float32
    log_g = np.log1p(-np.exp2(f32(-5.0) - np.arange(N_RET_HEADS, dtype=f32)))
    i = np.arange(length, dtype=f32)
    diff = i[:, None] - i[None, :]
    decay = np.where(diff >= 0, np.exp(np.maximum(diff, f32(0.0))[None] * log_g[:, None, None]), f32(0.0))
    dq = np.exp((i[None, :] + f32(1.0)) * log_g[:, None])
    dk = np.exp((f32(length) - f32(1.0) - i)[None, :] * log_g[:, None])
    ds = np.exp(f32(length) * log_g)
    lanes = (N_RET_HEADS, length, HEAD_DIM)
    tables = (decay, np.broadcast_to(dq[:, :, None], lanes), np.broadcast_to(dk[:, :, None], lanes),
              np.broadcast_to(ds[:, None, None], (N_RET_HEADS, 8, HEAD_DIM)))
    return tuple(jnp.asarray(np.ascontiguousarray(a), F32) for a in tables)


def _rotate(x, cos, sin_signed):
    return x * cos + pltpu.roll(x, HEAD_DIM // 2, 1) * sin_signed


def _retention_step(q, k, v, s, decay, dq, dk, ds):
    qb, kb, vb = q.astype(BF16), k.astype(BF16), v.astype(BF16)
    scores = lax.dot_general(qb, kb, NT_DIMS, preferred_element_type=F32) * decay
    o = jnp.dot(scores.astype(BF16), vb, preferred_element_type=F32)
    o = o + jnp.dot((q * dq).astype(BF16), s.astype(BF16), preferred_element_type=F32)
    kd = (k * dk).astype(BF16)
    s_new = ds * s + lax.dot_general(kd, vb, TN_DIMS, preferred_element_type=F32)
    return o, s_new


def _gated_layernorm(o, gn, gate):
    c = o - jnp.mean(o, axis=-1, keepdims=True)
    y = c * lax.rsqrt(jnp.mean(c * c, axis=-1, keepdims=True) + NORM_EPS)
    return (y * gn) * _silu(gate)


def _ret_prompt_body(q_ref, k_ref, v_ref, g_ref, cos_ref, sin_ref, decay_ref, dq_ref, dk_ref, ds_ref, gn_ref,
                     o_ref, sfin_ref, s_ref):
    c = pl.program_id(0)

    @pl.when(c == 0)
    def _():
        s_ref[...] = jnp.zeros_like(s_ref)

    cos, sin = cos_ref[...], sin_ref[...]
    for h in range(N_RET_HEADS):
        cols = slice(h * HEAD_DIM, (h + 1) * HEAD_DIM)
        q = _rotate(q_ref[:, cols], cos, sin)
        k = _rotate(k_ref[:, cols], cos, sin) * QK_SCALE
        o, s_new = _retention_step(q, k, v_ref[:, cols], s_ref[h], decay_ref[h], dq_ref[h], dk_ref[h],
                                   ds_ref[h, 0:1, :])
        s_ref[h] = s_new
        o_ref[:, cols] = _gated_layernorm(o, gn_ref[:, cols], g_ref[:, cols])

    @pl.when(c == pl.num_programs(0) - 1)
    def _():
        sfin_ref[...] = s_ref[...]


def _retention_prompt(proj, pos, gn):
    t = proj.shape[0]
    cos, sin = _rotary_tables(pos)
    decay, dq, dk, ds = _retention_tables(RET_CHUNK)
    segment = lambda j: pl.BlockSpec((RET_CHUNK, RET_WIDTH), lambda c: (c, j))
    rot = pl.BlockSpec((RET_CHUNK, HEAD_DIM), lambda c: (c, 0))
    full = lambda a: pl.BlockSpec(a.shape, lambda c: (0,) * a.ndim)
    gn2 = gn.reshape(1, RET_WIDTH)
    state = jax.ShapeDtypeStruct((N_RET_HEADS, HEAD_DIM, HEAD_DIM), F32)
    return pl.pallas_call(
        _ret_prompt_body,
        grid=(t // RET_CHUNK,),
        in_specs=[segment(0), segment(1), segment(2), segment(3), rot, rot, full(decay), full(dq), full(dk), full(ds),
                  full(gn2)],
        out_specs=[segment(0), pl.BlockSpec(state.shape, lambda c: (0, 0, 0))],
        out_shape=[jax.ShapeDtypeStruct((t, RET_WIDTH), F32), state],
        scratch_shapes=[pltpu.VMEM(state.shape, F32)],
        compiler_params=_params("arbitrary"),
        name="retention_prompt",
    )(proj, proj, proj, proj, cos, sin, decay, dq, dk, ds, gn2)


def _ret_sample_body(q_ref, k_ref, v_ref, g_ref, st_ref, cos_ref, sin_ref, decay_ref, dq_ref, dk_ref, ds_ref, gn_ref,
                     o_ref, snew_ref):
    cos, sin = cos_ref[...], sin_ref[...]
    for r in range(q_ref.shape[0]):
        for h in range(N_RET_HEADS):
            cols = slice(h * HEAD_DIM, (h + 1) * HEAD_DIM)
            q = _rotate(q_ref[r, :, cols], cos, sin)
            k = _rotate(k_ref[r, :, cols], cos, sin) * QK_SCALE
            o, s_new = _retention_step(q, k, v_ref[r, :, cols], st_ref[r, h], decay_ref[h], dq_ref[h], dk_ref[h],
                                       ds_ref[h, 0:1, :])
            snew_ref[r, h] = s_new
            o_ref[r, :, cols] = _gated_layernorm(o, gn_ref[:, cols], g_ref[r, :, cols])


def _retention_sample(proj, state, pos, gn):
    b, l, _ = proj.shape
    cos, sin = _rotary_tables(pos)
    decay, dq, dk, ds = _retention_tables(l)
    per_step = RET_REQUESTS_PER_STEP
    assert b % per_step == 0
    segment = lambda j: pl.BlockSpec((per_step, l, RET_WIDTH), lambda i: (i, 0, j))
    st = pl.BlockSpec((per_step, N_RET_HEADS, HEAD_DIM, HEAD_DIM), lambda i: (i, 0, 0, 0))
    full = lambda a: pl.BlockSpec(a.shape, lambda i: (0,) * a.ndim)
    gn2 = gn.reshape(1, RET_WIDTH)
    return pl.pallas_call(
        _ret_sample_body,
        grid=(b // per_step,),
        in_specs=[segment(0), segment(1), segment(2), segment(3), st, full(cos), full(sin), full(decay), full(dq),
                  full(dk), full(ds), full(gn2)],
        out_specs=[segment(0), st],
        out_shape=[jax.ShapeDtypeStruct((b, l, RET_WIDTH), F32), jax.ShapeDtypeStruct(state.shape, F32)],
        compiler_params=_params("parallel"),
        name="retention_sample",
    )(proj, proj, proj, proj, state, cos, sin, decay, dq, dk, ds, gn2)


def _suffix_matrix(n):
    j = lax.broadcasted_iota(jnp.int32, (n, n), 0)
    s = lax.broadcasted_iota(jnp.int32, (n, n), 1)
    return jnp.where(j > s, 1.0, 0.0).astype(BF16)


def _neg_abs(x):
    bits = lax.bitcast_convert_type(x, jnp.uint32) | jnp.uint32(0x80000000)
    return lax.bitcast_convert_type(bits, F32)


def _sb_block_log2(z2, tri, mask):
    rows, keys = z2.shape
    chunk = min(rows, SB_CHUNK)
    suffix = tri.shape[1]
    log2_w, dropped = [], []
    for c in range(0, rows, chunk):
        parts, later = [], None
        for s in reversed(range(0, keys, suffix)):
            zc = z2[c:c + chunk, s:s + suffix]
            neg_log_keep = jnp.maximum(zc, 0.0) + jnp.log2(1.0 + jnp.exp2(_neg_abs(zc)))
            log_beta = zc - neg_log_keep
            if mask is not None:
                neg_log_keep = jnp.where(mask[c:c + chunk, s:s + suffix], neg_log_keep, 0.0)
            between = jnp.dot(neg_log_keep.astype(BF16), tri, preferred_element_type=F32)
            total = jnp.sum(neg_log_keep, axis=-1, keepdims=True)
            if later is None:
                parts.append(log_beta - between)
                later = total
            else:
                parts.append(log_beta - between - later)
                later = later + total
        log2_w.append(jnp.concatenate(parts[::-1], axis=1))
        dropped.append(later)
    return jnp.concatenate(log2_w, axis=0), jnp.concatenate(dropped, axis=0)


def _head_rmsnorm(o, gn):
    return (o * lax.rsqrt(jnp.mean(o * o, axis=-1, keepdims=True) + RMS_EPS)) * gn


def _sb_prompt_body(bias_ref, q_ref, k_ref, v_ref, gn_ref, o_ref, ko_ref, vo_ref, kb_ref, vb_ref, tri_ref):
    h = pl.program_id(0)
    i = pl.program_id(1)

    @pl.when(i == 0)
    def _():
        k, v = k_ref[...], v_ref[...]
        ko_ref[...] = k
        vo_ref[...] = v
        kb_ref[...] = k.astype(BF16)
        vb_ref[...] = v.astype(BF16)
        tri_ref[...] = _suffix_matrix(SB_KEYS)

    q = (q_ref[...] * (QK_SCALE * LOG2E)).astype(BF16)
    bias = bias_ref[h] * LOG2E
    tri = tri_ref[...]

    def key_rows(j):
        start = j * SB_KEYS
        return pl.ds(start if isinstance(j, int) else pl.multiple_of(start, SB_KEYS), SB_KEYS)

    def logits(j):
        return lax.dot_general(q, kb_ref[key_rows(j), :], NT_DIMS, preferred_element_type=F32) + bias

    def weighted_values(log2_w, j, mask=None):
        a = jnp.exp2(log2_w)
        if mask is not None:
            a = jnp.where(mask, a, 0.0)
        return jnp.dot(a.astype(BF16), vb_ref[key_rows(j), :], preferred_element_type=F32)

    ratio = SB_QUERIES // SB_KEYS
    r = lax.broadcasted_iota(jnp.int32, (SB_QUERIES, SB_KEYS), 0)
    c = lax.broadcasted_iota(jnp.int32, (SB_QUERIES, SB_KEYS), 1)
    acc = jnp.zeros((SB_QUERIES, HEAD_DIM), F32)
    drop = jnp.zeros((SB_QUERIES, 1), F32)
    for d in reversed(range(ratio)):
        mask = c + d * SB_KEYS < r
        log2_w, dropped = _sb_block_log2(logits(i * ratio + d), tri, mask)
        acc = acc + weighted_values(log2_w - drop, i * ratio + d, mask)
        drop = drop + dropped

    n_old = i * ratio

    def step(n, state):
        z_prev, log2_w_prev, acc, drop = state
        z_new = logits(jnp.maximum(n_old - 1 - n, 0))
        log2_w, dropped = _sb_block_log2(z_prev, tri, None)
        out = weighted_values(log2_w_prev, jnp.clip(n_old + 1 - n, 0, n_old))
        return z_new, log2_w - drop, acc + out, drop + dropped

    idle = jnp.full((SB_QUERIES, SB_KEYS), -1e30, F32)
    z0 = logits(jnp.maximum(n_old - 1, 0))
    _, log2_w_last, acc, _ = lax.fori_loop(1, n_old + 1, step, (z0, idle, acc, drop))
    acc = acc + weighted_values(log2_w_last, 0)
    o_ref[...] = _head_rmsnorm(acc, gn_ref[...])


def _sb_prompt(proj, bias, gn):
    t, width = proj.shape
    assert t % SB_QUERIES == 0 and SB_QUERIES % SB_KEYS == 0
    q0, k0, v0 = [(width - n * SB_WIDTH) // HEAD_DIM for n in (3, 2, 1)]
    qspec = pl.BlockSpec((SB_QUERIES, HEAD_DIM), lambda h, i, b: (i, q0 + h))
    kspec = pl.BlockSpec((t, HEAD_DIM), lambda h, i, b: (0, k0 + h))
    vspec = pl.BlockSpec((t, HEAD_DIM), lambda h, i, b: (0, v0 + h))
    return pl.pallas_call(
        _sb_prompt_body,
        grid_spec=pltpu.PrefetchScalarGridSpec(
            num_scalar_prefetch=1,
            grid=(N_SB_HEADS, t // SB_QUERIES),
            in_specs=[qspec, kspec, vspec, pl.BlockSpec((1, HEAD_DIM), lambda h, i, b: (0, h))],
            out_specs=[pl.BlockSpec((SB_QUERIES, HEAD_DIM), lambda h, i, b: (i, h)),
                       pl.BlockSpec((t, HEAD_DIM), lambda h, i, b: (0, h)),
                       pl.BlockSpec((t, HEAD_DIM), lambda h, i, b: (0, h))],
            scratch_shapes=[pltpu.VMEM((t, HEAD_DIM), BF16), pltpu.VMEM((t, HEAD_DIM), BF16),
                            pltpu.VMEM((SB_KEYS, SB_KEYS), BF16)],
        ),
        out_shape=[jax.ShapeDtypeStruct((t, SB_WIDTH), F32)] * 3,
        compiler_params=_params("parallel", "arbitrary"),
        name="stick_breaking_prompt",
    )(bias.astype(F32), proj, proj, proj, gn.reshape(1, SB_WIDTH))


def _sb_sample_body(pt_ref, bias_ref, q_ref, kn_ref, vn_ref, *rest):
    kp_refs = rest[:SB_PAGES_PER_STEP]
    vp_refs = rest[SB_PAGES_PER_STEP:2 * SB_PAGES_PER_STEP]
    gn_ref, o_ref, acc_ref, drop_ref = rest[2 * SB_PAGES_PER_STEP:]
    step = pl.program_id(1)
    l = q_ref.shape[1]
    rows = N_SB_HEADS * l
    tri = _suffix_matrix(PAGE_SIZE)
    bias = jnp.concatenate([jnp.full((l, 1), bias_ref[h] * LOG2E, F32) for h in range(N_SB_HEADS)], axis=0)
    qs = [(q_ref[0, :, h * HEAD_DIM:(h + 1) * HEAD_DIM] * (QK_SCALE * LOG2E)).astype(BF16)
          for h in range(N_SB_HEADS)]

    def logits(keys):
        z = [lax.dot_general(qs[h], keys[h], NT_DIMS, preferred_element_type=F32) for h in range(N_SB_HEADS)]
        return jnp.concatenate(z, axis=0) + bias

    def weighted_values(a, values):
        out = [jnp.dot(a[h * l:(h + 1) * l, :].astype(BF16), values[h], preferred_element_type=F32)
               for h in range(N_SB_HEADS)]
        return jnp.concatenate(out, axis=0)

    @pl.when(step == 0)
    def _():
        pad = jnp.zeros((PAGE_SIZE - l, HEAD_DIM), F32)
        kn = [jnp.concatenate([kn_ref[0, :, h * HEAD_DIM:(h + 1) * HEAD_DIM], pad], axis=0).astype(BF16)
              for h in range(N_SB_HEADS)]
        vn = [jnp.concatenate([vn_ref[0, :, h * HEAD_DIM:(h + 1) * HEAD_DIM], pad], axis=0).astype(BF16)
              for h in range(N_SB_HEADS)]
        qi = lax.broadcasted_iota(jnp.int32, (rows, PAGE_SIZE), 0) % l
        kj = lax.broadcasted_iota(jnp.int32, (rows, PAGE_SIZE), 1)
        mask = kj < qi
        log2_w, dropped = _sb_block_log2(logits(kn), tri, mask)
        acc_ref[...] = weighted_values(jnp.where(mask, jnp.exp2(log2_w), 0.0), vn)
        drop_ref[...] = dropped

    head_rows = lambda h: pl.ds(h, PAGE_SIZE, stride=N_SB_HEADS)
    z = jnp.concatenate([logits([kp_ref[head_rows(h), :].astype(BF16) for h in range(N_SB_HEADS)])
                         for kp_ref in kp_refs], axis=0)
    log2_w, dropped = _sb_block_log2(z, tri, None)
    drop = drop_ref[...]
    acc = acc_ref[...]
    for n, vp_ref in enumerate(vp_refs):
        page = slice(n * rows, (n + 1) * rows)
        a = jnp.exp2(log2_w[page, :] - drop)
        acc = acc + weighted_values(a, [vp_ref[head_rows(h), :].astype(BF16) for h in range(N_SB_HEADS)])
        drop = drop + dropped[page, :]
    acc_ref[...] = acc
    drop_ref[...] = drop

    @pl.when(step == pl.num_programs(1) - 1)
    def _():
        for h in range(N_SB_HEADS):
            cols = slice(h * HEAD_DIM, (h + 1) * HEAD_DIM)
            o_ref[0, :, cols] = _head_rmsnorm(acc[h * l:(h + 1) * l, :], gn_ref[:, cols])


def _sb_sample(proj, cache_k, cache_v, page_table, bias, gn):
    b, l, width = proj.shape
    n_pages = page_table.shape[1]
    assert n_pages % SB_PAGES_PER_STEP == 0
    n_phys = cache_k.shape[0]
    page_rows = PAGE_SIZE * N_SB_HEADS
    cache_k = cache_k.reshape(n_phys, page_rows, HEAD_DIM)
    cache_v = cache_v.reshape(n_phys, page_rows, HEAD_DIM)
    segment = lambda j: pl.BlockSpec((1, l, SB_WIDTH), lambda i, s, pt, bs: (i, 0, j))
    last = width // SB_WIDTH - 1

    def page(n):
        return pl.BlockSpec((None, page_rows, HEAD_DIM),
                            lambda i, s, pt, bs: (pt[i, n_pages - 1 - (s * SB_PAGES_PER_STEP + n)], 0, 0))

    pages = [page(n) for n in range(SB_PAGES_PER_STEP)]
    return pl.pallas_call(
        _sb_sample_body,
        grid_spec=pltpu.PrefetchScalarGridSpec(
            num_scalar_prefetch=2,
            grid=(b, n_pages // SB_PAGES_PER_STEP),
            in_specs=[segment(last - 2), segment(last - 1), segment(last)] + pages + pages
                     + [pl.BlockSpec((1, SB_WIDTH), lambda i, s, pt, bs: (0, 0))],
            out_specs=segment(0),
            scratch_shapes=[pltpu.VMEM((N_SB_HEADS * l, HEAD_DIM), F32), pltpu.VMEM((N_SB_HEADS * l, 1), F32)],
        ),
        out_shape=jax.ShapeDtypeStruct((b, l, SB_WIDTH), F32),
        compiler_params=_params("parallel", "arbitrary"),
        name="stick_breaking_sample",
    )(page_table, bias.astype(F32), proj, proj, proj, *([cache_k] * SB_PAGES_PER_STEP),
      *([cache_v] * SB_PAGES_PER_STEP), gn.reshape(1, SB_WIDTH))


def _cross_body(q_ref, k_ref, v_ref, o_ref):
    n_mem = k_ref.shape[0] // N_MEM_HEADS
    for h in range(N_MEM_HEADS):
        cols = slice(h * HEAD_DIM, (h + 1) * HEAD_DIM)
        head_rows = pl.ds(h, n_mem, stride=N_MEM_HEADS)
        q = q_ref[0, :, cols].astype(BF16)
        k = k_ref[head_rows, :].astype(BF16)
        v = v_ref[head_rows, :].astype(BF16)
        s = lax.dot_general(q, k, NT_DIMS, preferred_element_type=F32) * QK_SCALE
        e = jnp.exp(s - jnp.max(s, axis=-1, keepdims=True))
        p = e / jnp.sum(e, axis=-1, keepdims=True)
        o_ref[0, :, cols] = jnp.dot(p.astype(BF16), v, preferred_element_type=F32)


def _cross_attention(q, mem_k, mem_v, tq):
    b, t, _ = q.shape
    m = mem_k.shape[1]
    tq = min(tq, t)
    qspec = pl.BlockSpec((1, tq, MEM_WIDTH), lambda i, j: (i, j, 0))
    mspec = pl.BlockSpec((None, m, HEAD_DIM), lambda i, j: (i, 0, 0))
    return pl.pallas_call(
        _cross_body,
        grid=(b, t // tq),
        in_specs=[qspec, mspec, mspec],
        out_specs=qspec,
        out_shape=jax.ShapeDtypeStruct(q.shape, F32),
        compiler_params=_params("parallel", "arbitrary"),
        name="memory_cross_attention",
    )(q, mem_k, mem_v)


def _swiglu(x, wg_b, wu_b, wd_b):
    g = jnp.dot(x, wg_b[...], preferred_element_type=F32)
    u = jnp.dot(x, wu_b[...], preferred_element_type=F32)
    return jnp.dot((_silu(g) * u).astype(BF16), wd_b[...], preferred_element_type=F32)


def _cast_weights(wg_ref, wu_ref, wd_ref, wg_b, wu_b, wd_b):
    wg_b[...] = wg_ref[0].astype(BF16)
    wu_b[...] = wu_ref[0].astype(BF16)
    wd_b[...] = wd_ref[0].astype(BF16)


def _shared_body(x_ref, wg_ref, wu_ref, wd_ref, o_ref, wg_b, wu_b, wd_b):
    @pl.when(pl.program_id(0) == 0)
    def _():
        _cast_weights(wg_ref, wu_ref, wd_ref, wg_b, wu_b, wd_b)

    o_ref[...] = _swiglu(_token_major_load(x_ref, o_ref.shape[0], BF16), wg_b, wu_b, wd_b)


def _shared_mlp(x, w_gate, w_up, w_down, rows):
    d, f = w_gate.shape[1], w_gate.shape[2]
    pieces = d // HEAD_DIM
    r = x.shape[0] // pieces
    wspec = lambda shape: pl.BlockSpec((1,) + shape, lambda i: (0, 0, 0))
    return pl.pallas_call(
        _shared_body,
        grid=(r // rows,),
        in_specs=[pl.BlockSpec((rows * pieces, HEAD_DIM), lambda i: (i, 0)), wspec((d, f)), wspec((d, f)),
                  wspec((f, d))],
        out_specs=pl.BlockSpec((rows, d), lambda i: (i, 0)),
        out_shape=jax.ShapeDtypeStruct((r, d), F32),
        scratch_shapes=[pltpu.VMEM((d, f), BF16), pltpu.VMEM((d, f), BF16), pltpu.VMEM((f, d), BF16)],
        compiler_params=_params("arbitrary"),
        name="shared_mlp",
    )(x, w_gate, w_up, w_down)


def _expert_body(be_ref, used_ref, shift_ref, tok_ref, x_hbm, wg_ref, wu_ref, wd_ref, o_ref, wg_b, wu_b, wd_b,
                 xbuf0, xbuf1, sem):
    b = pl.program_id(0)
    rows = o_ref.shape[0]
    n_used = used_ref[0]
    bufs = (xbuf0, xbuf1)
    last_entry = tok_ref.shape[0] - 1
    pieces = xbuf0.shape[0] // rows

    def token_rows(tok):
        start = tok * pieces
        return pl.ds(start if isinstance(tok, int) else pl.multiple_of(start, pieces), pieces)

    def row_token(first_entry, r):
        return tok_ref[jnp.minimum(first_entry + r, last_entry)]

    def first_entry(blk):
        return blk * rows - shift_ref[blk]

    def wait_gather(parity):
        pltpu.make_async_copy(x_hbm.at[pl.ds(0, rows * pieces), :], bufs[parity], sem.at[parity]).wait()

    @pl.when(b == 0)
    def _():
        def row(r, carry):
            src = x_hbm.at[token_rows(row_token(first_entry(0), r)), :]
            pltpu.make_async_copy(src, xbuf0.at[token_rows(r), :], sem.at[0]).start()
            return carry

        lax.fori_loop(0, rows, row, 0, unroll=8)

    @pl.when(jnp.logical_and(b < n_used, jnp.logical_or(b == 0, be_ref[b] != be_ref[jnp.maximum(b - 1, 0)])))
    def _():
        _cast_weights(wg_ref, wu_ref, wd_ref, wg_b, wu_b, wd_b)

    def run(parity):
        nxt = first_entry(jnp.minimum(b + 1, n_used - 1))
        wait_gather(parity)
        x = _token_major_load(bufs[parity], rows, BF16)
        for r in range(rows):
            src = x_hbm.at[token_rows(row_token(nxt, r)), :]
            pltpu.make_async_copy(src, bufs[1 - parity].at[token_rows(r), :], sem.at[1 - parity]).start()
        o_ref[...] = _swiglu(x, wg_b, wu_b, wd_b)

        @pl.when(b == n_used - 1)
        def _():
            wait_gather(1 - parity)

    for parity in (0, 1):
        pl.when(jnp.logical_and(b < n_used, b % 2 == parity))(functools.partial(run, parity))

    @pl.when(b >= n_used)
    def _():
        o_ref[...] = jnp.zeros_like(o_ref)


def _expert_mlp(x, slot_token, block_shift, block_expert, n_used, w_gate, w_up, w_down, rows):
    d, f = w_gate.shape[1], w_gate.shape[2]
    pieces = d // HEAD_DIM
    n_blocks = block_expert.shape[0]
    wspec = lambda shape: pl.BlockSpec((1,) + shape, lambda i, be, nu, sh, tok: (be[i], 0, 0))
    return pl.pallas_call(
        _expert_body,
        grid_spec=pltpu.PrefetchScalarGridSpec(
            num_scalar_prefetch=4,
            grid=(n_blocks,),
            in_specs=[pl.BlockSpec(memory_space=pl.ANY), wspec((d, f)), wspec((d, f)), wspec((f, d))],
            out_specs=pl.BlockSpec((rows, d), lambda i, be, nu, sh, tok: (i, 0)),
            scratch_shapes=[pltpu.VMEM((d, f), BF16), pltpu.VMEM((d, f), BF16), pltpu.VMEM((f, d), BF16),
                            pltpu.VMEM((rows * pieces, HEAD_DIM), F32), pltpu.VMEM((rows * pieces, HEAD_DIM), F32),
                            pltpu.SemaphoreType.DMA((2,))],
        ),
        out_shape=jax.ShapeDtypeStruct((n_blocks * rows, d), F32),
        compiler_params=_params("arbitrary"),
        name="expert_mlp",
    )(block_expert, n_used, block_shift, slot_token, x, w_gate, w_up, w_down)


def _combine_body(slot_ref, x_ref, sh_ref, gate_ref, g_ref, yb_hbm, o_ref, buf, sem):
    i = pl.program_id(0)
    tm = x_ref.shape[0]

    def start_gather(tile, b):
        base = tile * (tm * TOP_K)

        def token(r, carry):
            for k in range(TOP_K):
                src = yb_hbm.at[pl.ds(slot_ref[base + r * TOP_K + k], 1), :]
                pltpu.make_async_copy(src, buf.at[b, pl.ds(k * tm + r, 1), :], sem.at[b]).start()
            return carry

        lax.fori_loop(0, tm, token, 0)

    @pl.when(i == 0)
    def _():
        start_gather(0, 0)

    @pl.when(i + 1 < pl.num_programs(0))
    def _():
        start_gather(i + 1, (i + 1) % 2)

    b = i % 2
    pltpu.make_async_copy(yb_hbm.at[pl.ds(0, TOP_K * tm), :], buf.at[b], sem.at[b]).wait()
    routed = buf[b, 0:tm, :] * gate_ref[:, 0:1]
    for k in range(1, TOP_K):
        routed = routed + buf[b, k * tm:(k + 1) * tm, :] * gate_ref[:, k:k + 1]
    x = x_ref[...] + (routed + sh_ref[...])
    o_ref[...] = (x * lax.rsqrt(jnp.mean(x * x, axis=-1, keepdims=True) + RMS_EPS)) * g_ref[...]


def _combine_final_norm(x, yb, slot, gates, shared, gain):
    m, d = x.shape
    tm = COMBINE_ROWS
    assert m % tm == 0 and yb.shape[0] >= TOP_K * tm
    row = pl.BlockSpec((tm, d), lambda i, s: (i, 0))
    return pl.pallas_call(
        _combine_body,
        grid_spec=pltpu.PrefetchScalarGridSpec(
            num_scalar_prefetch=1,
            grid=(m // tm,),
            in_specs=[row, row, pl.BlockSpec((tm, TOP_K), lambda i, s: (i, 0)),
                      pl.BlockSpec((1, d), lambda i, s: (0, 0)), pl.BlockSpec(memory_space=pl.ANY)],
            out_specs=row,
            scratch_shapes=[pltpu.VMEM((2, TOP_K * tm, d), F32), pltpu.SemaphoreType.DMA((2,))],
        ),
        out_shape=jax.ShapeDtypeStruct((m, d), F32),
        compiler_params=_params("arbitrary"),
        name="combine_final_norm",
    )(slot.reshape(-1), x, shared, gates, gain.reshape(1, d), yb)


def _route_body(lg_ref, rb_ref, idx_ref, gate_ref, rank_ref, cnt_ref, seen_ref):
    @pl.when(pl.program_id(0) == 0)
    def _():
        seen_ref[...] = jnp.zeros_like(seen_ref)

    tm, n_exp = lg_ref.shape
    scores = _sigmoid(lg_ref[...])
    biased = scores + rb_ref[...]
    expert = lax.broadcasted_iota(jnp.int32, (tm, n_exp), 1).astype(F32)
    chosen = jnp.zeros((tm, n_exp), F32)
    picks = []
    for _ in range(TOP_K):
        best = jnp.max(biased, axis=-1, keepdims=True)
        first = jnp.min(jnp.where(biased == best, expert, float(n_exp)), axis=-1, keepdims=True)
        hit = expert == first
        picks.append((first, hit, jnp.sum(jnp.where(hit, scores, 0.0), axis=-1, keepdims=True)))
        biased = jnp.where(hit, -jnp.inf, biased)
        chosen = jnp.where(hit, 1.0, chosen)

    r = lax.broadcasted_iota(jnp.int32, (tm, tm), 0)
    c = lax.broadcasted_iota(jnp.int32, (tm, tm), 1)
    earlier = jnp.where(c < r, 1.0, 0.0).astype(BF16)
    rank_all = jnp.dot(earlier, chosen.astype(BF16), preferred_element_type=F32) + seen_ref[...]

    total = picks[0][2]
    for _, _, sel in picks[1:]:
        total = total + sel
    col = lax.broadcasted_iota(jnp.int32, (tm, TOP_K), 1)
    idx = jnp.zeros((tm, TOP_K), F32)
    gate = jnp.zeros((tm, TOP_K), F32)
    rank = jnp.zeros((tm, TOP_K), F32)
    for k, (first, hit, sel) in enumerate(picks):
        idx = jnp.where(col == k, first, idx)
        gate = jnp.where(col == k, sel / total * ROUTED_SCALE, gate)
        rank = jnp.where(col == k, jnp.sum(jnp.where(hit, rank_all, 0.0), axis=-1, keepdims=True), rank)
    idx_ref[...] = idx.astype(jnp.int32)
    gate_ref[...] = gate
    rank_ref[...] = rank.astype(jnp.int32)
    seen_ref[...] += jnp.sum(chosen, axis=0, keepdims=True)
    cnt_ref[...] = seen_ref[...].astype(jnp.int32)


def _route(logits, router_bias, tm=256):
    t, n_exp = logits.shape
    assert t % tm == 0
    row = lambda width: pl.BlockSpec((tm, width), lambda i: (i, 0))
    one = pl.BlockSpec((1, n_exp), lambda i: (0, 0))
    return pl.pallas_call(
        _route_body,
        grid=(t // tm,),
        in_specs=[row(n_exp), one],
        out_specs=[row(TOP_K), row(TOP_K), row(TOP_K), one],
        out_shape=[jax.ShapeDtypeStruct((t, TOP_K), jnp.int32), jax.ShapeDtypeStruct((t, TOP_K), F32),
                   jax.ShapeDtypeStruct((t, TOP_K), jnp.int32), jax.ShapeDtypeStruct((1, n_exp), jnp.int32)],
        scratch_shapes=[pltpu.VMEM((1, n_exp), F32)],
        compiler_params=_params("arbitrary"),
        name="router_topk",
    )(logits, router_bias.reshape(1, n_exp).astype(F32))


def _moe(hn, logits, router_bias, w_gate, w_up, w_down, ws_gate, ws_up, ws_down):
    t = logits.shape[0]
    idx, gates, rank, counts = _route(logits, router_bias)

    padded = (counts[0] + EXPERT_ROWS - 1) // EXPERT_ROWS * EXPERT_ROWS
    pad_end = jnp.cumsum(padded)
    pad_start = pad_end - padded
    experts = jnp.arange(N_EXPERTS, dtype=jnp.int32)
    slot = jnp.sum(jnp.where(idx[:, :, None] == experts, pad_start, 0), axis=-1) + rank
    n_blocks = -(-(t * TOP_K) // EXPERT_ROWS) + N_EXPERTS
    block_start = jnp.arange(n_blocks, dtype=jnp.int32) * EXPERT_ROWS
    block_e = jnp.minimum(jnp.sum(pad_end[None, :] <= block_start[:, None], axis=1), N_EXPERTS - 1).astype(jnp.int32)
    n_used = (pad_end[-1:] // EXPERT_ROWS).astype(jnp.int32)
    _, tok_sorted = lax.sort_key_val(slot.reshape(-1), jnp.repeat(jnp.arange(t, dtype=jnp.int32), TOP_K))
    first = jnp.cumsum(counts[0]) - counts[0]
    block_shift = (pad_start - first)[block_e].astype(jnp.int32)
    yb = _expert_mlp(hn, tok_sorted, block_shift, block_e, n_used, w_gate, w_up, w_down, EXPERT_ROWS)
    shared = _shared_mlp(hn, ws_gate[None], ws_up[None], ws_down[None], EXPERT_ROWS)
    return yb, slot.astype(jnp.int32), gates, shared


def kernel(x_prompt, x_sample, mem_prompt, cache_sb_k, cache_sb_v, page_table, state_ret, cache_mem_k, cache_mem_v,
           norm_mix, w_in, ret_gn, sb_gn, sb_bias, w_out, norm_mem, w_ck, w_cv, norm_cross, w_cq, w_co, norm_ffn,
           router_w, router_bias, w_gate, w_up, w_down, ws_gate, ws_up, ws_down, norm_final):
    depth = w_in.shape[0]
    assert depth == 1 and x_prompt.shape[0] == 1
    _, t, d = x_prompt.shape
    b, l, _ = x_sample.shape
    n_pages = page_table.shape[1]
    lyr = 0
    xp = x_prompt.reshape(t, d)
    xs = x_sample.reshape(b * l, d)
    pos_p = np.arange(t)
    pos_s = n_pages * PAGE_SIZE + np.arange(l)

    w_in_b = w_in[lyr].astype(BF16)
    w_out_b = w_out[lyr].astype(BF16)
    w_cq_b = w_cq[lyr].astype(BF16)
    w_co_b = w_co[lyr].astype(BF16)

    mem = mem_prompt.reshape(-1, d)
    n_mem = mem.shape[0]
    mkv = _mm(mem, jnp.concatenate([w_ck[lyr], w_cv[lyr]], axis=1).astype(BF16), gain=norm_mem[lyr])
    mk_p, mv_p = mkv[:, :MEM_WIDTH], mkv[:, MEM_WIDTH:]

    proj = _mm(xp, w_in_b, gain=norm_mix[lyr])
    o_ret, s_ret_p = _retention_prompt(proj, pos_p, ret_gn[lyr])
    o_sb, sk_p, sv_p = _sb_prompt(proj, sb_bias[lyr], sb_gn[lyr])
    xp = _mm([o_ret, o_sb], w_out_b, res=xp)

    proj = _mm(xs, w_in_b, gain=norm_mix[lyr]).reshape(b, l, -1)
    sk_s, sv_s = proj[:, :, -2 * SB_WIDTH:-SB_WIDTH], proj[:, :, -SB_WIDTH:]
    o_ret, s_ret_s = _retention_sample(proj, state_ret[lyr], pos_s, ret_gn[lyr])
    o_sb = _sb_sample(proj, cache_sb_k[lyr], cache_sb_v[lyr], page_table, sb_bias[lyr], sb_gn[lyr])
    xs = _mm([o_ret.reshape(b * l, -1), o_sb.reshape(b * l, -1)], w_out_b, res=xs)

    q = _mm(xp, w_cq_b, gain=norm_cross[lyr])
    mem_rows = n_mem * N_MEM_HEADS
    o = _cross_attention(q[None], mk_p.reshape(1, mem_rows, HEAD_DIM), mv_p.reshape(1, mem_rows, HEAD_DIM), tq=512)
    xp = _mm(o[0], w_co_b, res=xp)
    q = _mm(xs, w_cq_b, gain=norm_cross[lyr]).reshape(b, l, MEM_WIDTH)
    o = _cross_attention(q, cache_mem_k[lyr].reshape(b, mem_rows, HEAD_DIM),
                         cache_mem_v[lyr].reshape(b, mem_rows, HEAD_DIM), tq=l)
    xs = _mm(o.reshape(b * l, MEM_WIDTH), w_co_b, res=xs)

    x_all = jnp.concatenate([xp, xs], axis=0)
    logits, hn = _mm(x_all, router_w[lyr].astype(BF16), gain=norm_ffn[lyr], emit_xn=True)
    yb, slot, gates, shared = _moe(hn, logits, router_bias[lyr], w_gate[lyr], w_up[lyr], w_down[lyr],
                                   ws_gate[lyr], ws_up[lyr], ws_down[lyr])
    y_all = _combine_final_norm(x_all, yb, slot, gates, shared, norm_final)

    y_prompt = y_all[:t].reshape(1, t, d)
    y_sample = y_all[t:].reshape(b, l, d)
    return (y_prompt, y_sample,
            s_ret_p[None, None],
            sk_p.reshape(1, 1, t, N_SB_HEADS, HEAD_DIM), sv_p.reshape(1, 1, t, N_SB_HEADS, HEAD_DIM),
            mk_p.reshape(1, 1, n_mem, N_MEM_HEADS, HEAD_DIM), mv_p.reshape(1, 1, n_mem, N_MEM_HEADS, HEAD_DIM),
            s_ret_s[None],
            sk_s.reshape(1, b, l, N_SB_HEADS, HEAD_DIM), sv_s.reshape(1, b, l, N_SB_HEADS, HEAD_DIM))
```

```python
import functools

import jax
import jax.numpy as jnp
import numpy as np
from jax import lax
from jax.experimental import pallas as pl
from jax.experimental.pallas import tpu as pltpu

F32 = jnp.float32
BF16 = jnp.bfloat16

HEAD_DIM = 128
N_RET_HEADS = 8
N_SB_HEADS = 8
RET_WIDTH = N_RET_HEADS * HEAD_DIM
SB_WIDTH = N_SB_HEADS * HEAD_DIM
RET_CHUNK = 128
PAGE_SIZE = 128
ROPE_BASE = 10000.0
N_MEM_HEADS = 4
MEM_WIDTH = N_MEM_HEADS * HEAD_DIM
N_EXPERTS = 64
TOP_K = 8
ROUTED_SCALE = 2.5
RMS_EPS = 1e-6
NORM_EPS = 1e-6
QK_SCALE = HEAD_DIM ** -0.5

VMEM_LIMIT_BYTES = 56 * 1024 * 1024
SB_KEYS = 256
SB_QUERIES = 512
SB_PAGES_PER_STEP = 8
EXPERT_ROWS = 256
COMBINE_ROWS = 128
SB_CHUNK = 128
RET_REQUESTS_PER_STEP = 4
CROSS_REQUESTS_PER_STEP = 8
LOG2E = 1.4426950408889634

NT_DIMS = (((1,), (1,)), ((), ()))
TN_DIMS = (((0,), (0,)), ((), ()))


def _params(*sem):
    return pltpu.CompilerParams(dimension_semantics=sem, vmem_limit_bytes=VMEM_LIMIT_BYTES)


def _sigmoid(x):
    return 1.0 / (1.0 + jnp.exp(-x))


def _silu(x):
    return x * _sigmoid(x)


def _mm_body(*refs, n_x, has_norm, has_res, emit_xn):
    it = iter(refs)
    x_refs = [next(it) for _ in range(n_x)]
    g_ref = next(it) if has_norm else None
    w_ref = next(it)
    r_ref = next(it) if has_res else None
    o_ref = next(it)
    xo_ref = next(it) if emit_xn else None
    xn_ref = next(it)

    @pl.when(pl.program_id(1) == 0)
    def _():
        x = x_refs[0][...] if n_x == 1 else jnp.concatenate([r[...] for r in x_refs], axis=1)
        if has_norm:
            x = x * lax.rsqrt(jnp.mean(x * x, axis=-1, keepdims=True) + RMS_EPS)
            x = x * g_ref[...]
        xn_ref[...] = x.astype(BF16)
        if emit_xn:
            xo_ref[...] = x

    acc = jnp.dot(xn_ref[...], w_ref[...], preferred_element_type=F32)
    if has_res:
        acc = r_ref[...] + acc
    o_ref[...] = acc


def _mm(xs, w, gain=None, res=None, emit_xn=False, tm=1024, tn=1024):
    xs = xs if isinstance(xs, (list, tuple)) else [xs]
    m = xs[0].shape[0]
    k = sum(x.shape[1] for x in xs)
    n = w.shape[1]
    tm = min(tm, m)
    tn = min(tn, n)
    assert m % tm == 0 and n % tn == 0
    in_specs = [pl.BlockSpec((tm, x.shape[1]), lambda i, j: (i, 0)) for x in xs]
    args = list(xs)
    if gain is not None:
        in_specs.append(pl.BlockSpec((1, k), lambda i, j: (0, 0)))
        args.append(gain.reshape(1, k).astype(F32))
    in_specs.append(pl.BlockSpec((k, tn), lambda i, j: (0, j)))
    args.append(w)
    if res is not None:
        in_specs.append(pl.BlockSpec((tm, tn), lambda i, j: (i, j)))
        args.append(res)
    out_shape = [jax.ShapeDtypeStruct((m, n), F32)]
    out_specs = [pl.BlockSpec((tm, tn), lambda i, j: (i, j))]
    if emit_xn:
        out_shape.append(jax.ShapeDtypeStruct((m, k), F32))
        out_specs.append(pl.BlockSpec((tm, k), lambda i, j: (i, 0)))
    outs = pl.pallas_call(
        functools.partial(_mm_body, n_x=len(xs), has_norm=gain is not None, has_res=res is not None,
                          emit_xn=emit_xn),
        grid=(m // tm, n // tn),
        in_specs=in_specs,
        out_specs=out_specs,
        out_shape=out_shape,
        scratch_shapes=[pltpu.VMEM((tm, k), BF16)],
        compiler_params=_params("parallel", "arbitrary"),
        name="norm_matmul",
    )(*args)
    return outs if emit_xn else outs[0]


def _rotary_tables(pos):
    half = HEAD_DIM // 2
    inv_freq = np.float32(ROPE_BASE) ** (-np.arange(half, dtype=np.float32) / np.float32(half))
    ang = pos.astype(np.float32)[:, None] * inv_freq[None, :]
    cos, sin = np.cos(ang), np.sin(ang)
    return (jnp.asarray(np.concatenate([cos, cos], axis=-1), F32),
            jnp.asarray(np.concatenate([-sin, sin], axis=-1), F32))


def _retention_tables(length):
    f32 = np.float32
    log_g = np.log1p(-np.exp2(f32(-5.0) - np.arange(N_RET_HEADS, dtype=f32)))
    i = np.arange(length, dtype=f32)
    diff = i[:, None] - i[None, :]
    decay = np.where(diff >= 0, np.exp(np.maximum(diff, f32(0.0))[None] * log_g[:, None, None]), f32(0.0))
    dq = np.exp((i[None, :] + f32(1.0)) * log_g[:, None])
    dk = np.exp((f32(length) - f32(1.0) - i)[None, :] * log_g[:, None])
    ds = np.exp(f32(length) * log_g)
    lanes = (N_RET_HEADS, length, HEAD_DIM)
    tables = (decay, np.broadcast_to(dq[:, :, None], lanes), np.broadcast_to(dk[:, :, None], lanes),
              np.broadcast_to(ds[:, None, None], (N_RET_HEADS, 8, HEAD_DIM)))
    return tuple(jnp.asarray(np.ascontiguousarray(a), F32) for a in tables)


def _rotate(x, cos, sin_signed):
    return x * cos + pltpu.roll(x, HEAD_DIM // 2, 1) * sin_signed


def _retention_step(q, k, v, s, decay, dq, dk, ds):
    qb, kb, vb = q.astype(BF16), k.astype(BF16), v.astype(BF16)
    scores = lax.dot_general(qb, kb, NT_DIMS, preferred_element_type=F32) * decay
    o = jnp.dot(scores.astype(BF16), vb, preferred_element_type=F32)
    o = o + jnp.dot((q * dq).astype(BF16), s.astype(BF16), preferred_element_type=F32)
    kd = (k * dk).astype(BF16)
    s_new = ds * s + lax.dot_general(kd, vb, TN_DIMS, preferred_element_type=F32)
    return o, s_new


def _gated_layernorm(o, gn, gate):
    c = o - jnp.mean(o, axis=-1, keepdims=True)
    y = c * lax.rsqrt(jnp.mean(c * c, axis=-1, keepdims=True) + NORM_EPS)
    return (y * gn) * _silu(gate)


def _ret_prompt_body(q_ref, k_ref, v_ref, g_ref, cos_ref, sin_ref, decay_ref, dq_ref, dk_ref, ds_ref, gn_ref,
                     o_ref, sfin_ref, s_ref):
    c = pl.program_id(0)

    @pl.when(c == 0)
    def _():
        s_ref[...] = jnp.zeros_like(s_ref)

    cos, sin = cos_ref[...], sin_ref[...]
    for h in range(N_RET_HEADS):
        cols = slice(h * HEAD_DIM, (h + 1) * HEAD_DIM)
        q = _rotate(q_ref[:, cols], cos, sin)
        k = _rotate(k_ref[:, cols], cos, sin) * QK_SCALE
        o, s_new = _retention_step(q, k, v_ref[:, cols], s_ref[h], decay_ref[h], dq_ref[h], dk_ref[h],
                                   ds_ref[h, 0:1, :])
        s_ref[h] = s_new
        o_ref[:, cols] = _gated_layernorm(o, gn_ref[:, cols], g_ref[:, cols])

    @pl.when(c == pl.num_programs(0) - 1)
    def _():
        sfin_ref[...] = s_ref[...]


def _retention_prompt(proj, pos, gn):
    t = proj.shape[0]
    cos, sin = _rotary_tables(pos)
    decay, dq, dk, ds = _retention_tables(RET_CHUNK)
    segment = lambda j: pl.BlockSpec((RET_CHUNK, RET_WIDTH), lambda c: (c, j))
    rot = pl.BlockSpec((RET_CHUNK, HEAD_DIM), lambda c: (c, 0))
    full = lambda a: pl.BlockSpec(a.shape, lambda c: (0,) * a.ndim)
    gn2 = gn.reshape(1, RET_WIDTH)
    state = jax.ShapeDtypeStruct((N_RET_HEADS, HEAD_DIM, HEAD_DIM), F32)
    return pl.pallas_call(
        _ret_prompt_body,
        grid=(t // RET_CHUNK,),
        in_specs=[segment(0), segment(1), segment(2), segment(3), rot, rot, full(decay), full(dq), full(dk), full(ds),
                  full(gn2)],
        out_specs=[segment(0), pl.BlockSpec(state.shape, lambda c: (0, 0, 0))],
        out_shape=[jax.ShapeDtypeStruct((t, RET_WIDTH), F32), state],
        scratch_shapes=[pltpu.VMEM(state.shape, F32)],
        compiler_params=_params("arbitrary"),
        name="retention_prompt",
    )(proj, proj, proj, proj, cos, sin, decay, dq, dk, ds, gn2)


def _ret_sample_body(q_ref, k_ref, v_ref, g_ref, st_ref, cos_ref, sin_ref, decay_ref, dq_ref, dk_ref, ds_ref, gn_ref,
                     o_ref, snew_ref):
    cos, sin = cos_ref[...], sin_ref[...]
    for r in range(q_ref.shape[0]):
        for h in range(N_RET_HEADS):
            cols = slice(h * HEAD_DIM, (h + 1) * HEAD_DIM)
            q = _rotate(q_ref[r, :, cols], cos, sin)
            k = _rotate(k_ref[r, :, cols], cos, sin) * QK_SCALE
            o, s_new = _retention_step(q, k, v_ref[r, :, cols], st_ref[r, h], decay_ref[h], dq_ref[h], dk_ref[h],
                                       ds_ref[h, 0:1, :])
            snew_ref[r, h] = s_new
            o_ref[r, :, cols] = _gated_layernorm(o, gn_ref[:, cols], g_ref[r, :, cols])


def _retention_sample(proj, state, pos, gn):
    b, l, _ = proj.shape
    cos, sin = _rotary_tables(pos)
    decay, dq, dk, ds = _retention_tables(l)
    per_step = RET_REQUESTS_PER_STEP
    assert b % per_step == 0
    segment = lambda j: pl.BlockSpec((per_step, l, RET_WIDTH), lambda i: (i, 0, j))
    st = pl.BlockSpec((per_step, N_RET_HEADS, HEAD_DIM, HEAD_DIM), lambda i: (i, 0, 0, 0))
    full = lambda a: pl.BlockSpec(a.shape, lambda i: (0,) * a.ndim)
    gn2 = gn.reshape(1, RET_WIDTH)
    return pl.pallas_call(
        _ret_sample_body,
        grid=(b // per_step,),
        in_specs=[segment(0), segment(1), segment(2), segment(3), st, full(cos), full(sin), full(decay), full(dq),
                  full(dk), full(ds), full(gn2)],
        out_specs=[segment(0), st],
        out_shape=[jax.ShapeDtypeStruct((b, l, RET_WIDTH), F32), jax.ShapeDtypeStruct(state.shape, F32)],
        compiler_params=_params("parallel"),
        name="retention_sample",
    )(proj, proj, proj, proj, state, cos, sin, decay, dq, dk, ds, gn2)


def _suffix_matrix(n):
    j = lax.broadcasted_iota(jnp.int32, (n, n), 0)
    s = lax.broadcasted_iota(jnp.int32, (n, n), 1)
    return jnp.where(j > s, 1.0, 0.0).astype(BF16)


def _neg_abs(x):
    bits = lax.bitcast_convert_type(x, jnp.uint32) | jnp.uint32(0x80000000)
    return lax.bitcast_convert_type(bits, F32)


def _sb_block_log2(z2, tri, mask):
    rows, keys = z2.shape
    chunk = min(rows, SB_CHUNK)
    suffix = tri.shape[1]
    log2_w, dropped = [], []
    for c in range(0, rows, chunk):
        parts, later = [], None
        for s in reversed(range(0, keys, suffix)):
            zc = z2[c:c + chunk, s:s + suffix]
            neg_log_keep = jnp.maximum(zc, 0.0) + jnp.log2(1.0 + jnp.exp2(_neg_abs(zc)))
            log_beta = zc - neg_log_keep
            if mask is not None:
                neg_log_keep = jnp.where(mask[c:c + chunk, s:s + suffix], neg_log_keep, 0.0)
            between = jnp.dot(neg_log_keep.astype(BF16), tri, preferred_element_type=F32)
            total = jnp.sum(neg_log_keep, axis=-1, keepdims=True)
            if later is None:
                parts.append(log_beta - between)
                later = total
            else:
                parts.append(log_beta - between - later)
                later = later + total
        log2_w.append(jnp.concatenate(parts[::-1], axis=1))
        dropped.append(later)
    return jnp.concatenate(log2_w, axis=0), jnp.concatenate(dropped, axis=0)


def _head_rmsnorm(o, gn):
    return (o * lax.rsqrt(jnp.mean(o * o, axis=-1, keepdims=True) + RMS_EPS)) * gn


def _sb_prompt_body(bias_ref, q_ref, k_ref, v_ref, gn_ref, o_ref, ko_ref, vo_ref, kb_ref, vb_ref, tri_ref):
    h = pl.program_id(0)
    i = pl.program_id(1)

    @pl.when(i == 0)
    def _():
        k, v = k_ref[...], v_ref[...]
        ko_ref[...] = k
        vo_ref[...] = v
        kb_ref[...] = k.astype(BF16)
        vb_ref[...] = v.astype(BF16)
        tri_ref[...] = _suffix_matrix(SB_KEYS)

    q = (q_ref[...] * (QK_SCALE * LOG2E)).astype(BF16)
    bias = bias_ref[h] * LOG2E
    tri = tri_ref[...]

    def key_rows(j):
        start = j * SB_KEYS
        return pl.ds(start if isinstance(j, int) else pl.multiple_of(start, SB_KEYS), SB_KEYS)

    def logits(j):
        return lax.dot_general(q, kb_ref[key_rows(j), :], NT_DIMS, preferred_element_type=F32) + bias

    def weighted_values(log2_w, j, mask=None):
        a = jnp.exp2(log2_w)
        if mask is not None:
            a = jnp.where(mask, a, 0.0)
        return jnp.dot(a.astype(BF16), vb_ref[key_rows(j), :], preferred_element_type=F32)

    ratio = SB_QUERIES // SB_KEYS
    r = lax.broadcasted_iota(jnp.int32, (SB_QUERIES, SB_KEYS), 0)
    c = lax.broadcasted_iota(jnp.int32, (SB_QUERIES, SB_KEYS), 1)
    acc = jnp.zeros((SB_QUERIES, HEAD_DIM), F32)
    drop = jnp.zeros((SB_QUERIES, 1), F32)
    for d in reversed(range(ratio)):
        mask = c + d * SB_KEYS < r
        log2_w, dropped = _sb_block_log2(logits(i * ratio + d), tri, mask)
        acc = acc + weighted_values(log2_w - drop, i * ratio + d, mask)
        drop = drop + dropped

    n_old = i * ratio

    def step(n, state):
        z_prev, log2_w_prev, acc, drop = state
        z_new = logits(jnp.maximum(n_old - 1 - n, 0))
        log2_w, dropped = _sb_block_log2(z_prev, tri, None)
        out = weighted_values(log2_w_prev, jnp.clip(n_old + 1 - n, 0, n_old))
        return z_new, log2_w - drop, acc + out, drop + dropped

    idle = jnp.full((SB_QUERIES, SB_KEYS), -1e30, F32)
    z0 = logits(jnp.maximum(n_old - 1, 0))
    _, log2_w_last, acc, _ = lax.fori_loop(1, n_old + 1, step, (z0, idle, acc, drop))
    acc = acc + weighted_values(log2_w_last, 0)
    o_ref[...] = _head_rmsnorm(acc, gn_ref[...])


def _sb_prompt(proj, bias, gn):
    t, width = proj.shape
    assert t % SB_QUERIES == 0 and SB_QUERIES % SB_KEYS == 0
    q0, k0, v0 = [(width - n * SB_WIDTH) // HEAD_DIM for n in (3, 2, 1)]
    qspec = pl.BlockSpec((SB_QUERIES, HEAD_DIM), lambda h, i, b: (i, q0 + h))
    kspec = pl.BlockSpec((t, HEAD_DIM), lambda h, i, b: (0, k0 + h))
    vspec = pl.BlockSpec((t, HEAD_DIM), lambda h, i, b: (0, v0 + h))
    return pl.pallas_call(
        _sb_prompt_body,
        grid_spec=pltpu.PrefetchScalarGridSpec(
            num_scalar_prefetch=1,
            grid=(N_SB_HEADS, t // SB_QUERIES),
            in_specs=[qspec, kspec, vspec, pl.BlockSpec((1, HEAD_DIM), lambda h, i, b: (0, h))],
            out_specs=[pl.BlockSpec((SB_QUERIES, HEAD_DIM), lambda h, i, b: (i, h)),
                       pl.BlockSpec((t, HEAD_DIM), lambda h, i, b: (0, h)),
                       pl.BlockSpec((t, HEAD_DIM), lambda h, i, b: (0, h))],
            scratch_shapes=[pltpu.VMEM((t, HEAD_DIM), BF16), pltpu.VMEM((t, HEAD_DIM), BF16),
                            pltpu.VMEM((SB_KEYS, SB_KEYS), BF16)],
        ),
        out_shape=[jax.ShapeDtypeStruct((t, SB_WIDTH), F32)] * 3,
        compiler_params=_params("parallel", "arbitrary"),
        name="stick_breaking_prompt",
    )(bias.astype(F32), proj, proj, proj, gn.reshape(1, SB_WIDTH))


def _sb_sample_body(pt_ref, bias_ref, q_ref, kn_ref, vn_ref, *rest):
    kp_refs = rest[:SB_PAGES_PER_STEP]
    vp_refs = rest[SB_PAGES_PER_STEP:2 * SB_PAGES_PER_STEP]
    gn_ref, o_ref, acc_ref, drop_ref = rest[2 * SB_PAGES_PER_STEP:]
    step = pl.program_id(1)
    l = q_ref.shape[1]
    rows = N_SB_HEADS * l
    tri = _suffix_matrix(PAGE_SIZE)
    bias = jnp.concatenate([jnp.full((l, 1), bias_ref[h] * LOG2E, F32) for h in range(N_SB_HEADS)], axis=0)
    qs = [(q_ref[0, :, h * HEAD_DIM:(h + 1) * HEAD_DIM] * (QK_SCALE * LOG2E)).astype(BF16)
          for h in range(N_SB_HEADS)]

    def logits(keys):
        z = [lax.dot_general(qs[h], keys[h], NT_DIMS, preferred_element_type=F32) for h in range(N_SB_HEADS)]
        return jnp.concatenate(z, axis=0) + bias

    def weighted_values(a, values):
        out = [jnp.dot(a[h * l:(h + 1) * l, :].astype(BF16), values[h], preferred_element_type=F32)
               for h in range(N_SB_HEADS)]
        return jnp.concatenate(out, axis=0)

    @pl.when(step == 0)
    def _():
        pad = jnp.zeros((PAGE_SIZE - l, HEAD_DIM), F32)
        kn = [jnp.concatenate([kn_ref[0, :, h * HEAD_DIM:(h + 1) * HEAD_DIM], pad], axis=0).astype(BF16)
              for h in range(N_SB_HEADS)]
        vn = [jnp.concatenate([vn_ref[0, :, h * HEAD_DIM:(h + 1) * HEAD_DIM], pad], axis=0).astype(BF16)
              for h in range(N_SB_HEADS)]
        qi = lax.broadcasted_iota(jnp.int32, (rows, PAGE_SIZE), 0) % l
        kj = lax.broadcasted_iota(jnp.int32, (rows, PAGE_SIZE), 1)
        mask = kj < qi
        log2_w, dropped = _sb_block_log2(logits(kn), tri, mask)
        acc_ref[...] = weighted_values(jnp.where(mask, jnp.exp2(log2_w), 0.0), vn)
        drop_ref[...] = dropped

    head_rows = lambda h: pl.ds(h, PAGE_SIZE, stride=N_SB_HEADS)
    z = jnp.concatenate([logits([kp_ref[head_rows(h), :].astype(BF16) for h in range(N_SB_HEADS)])
                         for kp_ref in kp_refs], axis=0)
    log2_w, dropped = _sb_block_log2(z, tri, None)
    drop = drop_ref[...]
    acc = acc_ref[...]
    for n, vp_ref in enumerate(vp_refs):
        page = slice(n * rows, (n + 1) * rows)
        a = jnp.exp2(log2_w[page, :] - drop)
        acc = acc + weighted_values(a, [vp_ref[head_rows(h), :].astype(BF16) for h in range(N_SB_HEADS)])
        drop = drop + dropped[page, :]
    acc_ref[...] = acc
    drop_ref[...] = drop

    @pl.when(step == pl.num_programs(1) - 1)
    def _():
        for h in range(N_SB_HEADS):
            cols = slice(h * HEAD_DIM, (h + 1) * HEAD_DIM)
            o_ref[0, :, cols] = _head_rmsnorm(acc[h * l:(h + 1) * l, :], gn_ref[:, cols])


def _sb_sample(proj, cache_k, cache_v, page_table, bias, gn):
    b, l, width = proj.shape
    n_pages = page_table.shape[1]
    assert n_pages % SB_PAGES_PER_STEP == 0
    n_phys = cache_k.shape[0]
    page_rows = PAGE_SIZE * N_SB_HEADS
    cache_k = cache_k.reshape(n_phys, page_rows, HEAD_DIM)
    cache_v = cache_v.reshape(n_phys, page_rows, HEAD_DIM)
    segment = lambda j: pl.BlockSpec((1, l, SB_WIDTH), lambda i, s, pt, bs: (i, 0, j))
    last = width // SB_WIDTH - 1

    def page(n):
        return pl.BlockSpec((None, page_rows, HEAD_DIM),
                            lambda i, s, pt, bs: (pt[i, n_pages - 1 - (s * SB_PAGES_PER_STEP + n)], 0, 0))

    pages = [page(n) for n in range(SB_PAGES_PER_STEP)]
    return pl.pallas_call(
        _sb_sample_body,
        grid_spec=pltpu.PrefetchScalarGridSpec(
            num_scalar_prefetch=2,
            grid=(b, n_pages // SB_PAGES_PER_STEP),
            in_specs=[segment(last - 2), segment(last - 1), segment(last)] + pages + pages
                     + [pl.BlockSpec((1, SB_WIDTH), lambda i, s, pt, bs: (0, 0))],
            out_specs=segment(0),
            scratch_shapes=[pltpu.VMEM((N_SB_HEADS * l, HEAD_DIM), F32), pltpu.VMEM((N_SB_HEADS * l, 1), F32)],
        ),
        out_shape=jax.ShapeDtypeStruct((b, l, SB_WIDTH), F32),
        compiler_params=_params("parallel", "arbitrary"),
        name="stick_breaking_sample",
    )(page_table, bias.astype(F32), proj, proj, proj, *([cache_k] * SB_PAGES_PER_STEP),
      *([cache_v] * SB_PAGES_PER_STEP), gn.reshape(1, SB_WIDTH))


def _cross_body(q_ref, *refs):
    group = q_ref.shape[0]
    k_refs, v_refs, o_ref = refs[:group], refs[group:2 * group], refs[2 * group]
    tq = q_ref.shape[1]
    n_mem = k_refs[0].shape[0] // N_MEM_HEADS
    pairs = [(g, h) for g in range(group) for h in range(N_MEM_HEADS)]
    cols = lambda h: slice(h * HEAD_DIM, (h + 1) * HEAD_DIM)
    head_rows = lambda h: pl.ds(h, n_mem, stride=N_MEM_HEADS)
    scores = [lax.dot_general(q_ref[g, :, cols(h)].astype(BF16), k_refs[g][head_rows(h), :].astype(BF16), NT_DIMS,
                              preferred_element_type=F32) for g, h in pairs]
    s = jnp.concatenate(scores, axis=0) * QK_SCALE
    e = jnp.exp(s - jnp.max(s, axis=-1, keepdims=True))
    p = e / jnp.sum(e, axis=-1, keepdims=True)
    for n, (g, h) in enumerate(pairs):
        v = v_refs[g][head_rows(h), :].astype(BF16)
        o_ref[g, :, cols(h)] = jnp.dot(p[n * tq:(n + 1) * tq, :].astype(BF16), v, preferred_element_type=F32)


def _cross_attention(q, mem_k, mem_v, tq, group):
    b, t, _ = q.shape
    m = mem_k.shape[1]
    tq = min(tq, t)
    assert b % group == 0 and t % tq == 0
    qspec = pl.BlockSpec((group, tq, MEM_WIDTH), lambda i, j: (i, j, 0))
    mspec = lambda g: pl.BlockSpec((None, m, HEAD_DIM), lambda i, j: (i * group + g, 0, 0))
    mspecs = [mspec(g) for g in range(group)]
    return pl.pallas_call(
        _cross_body,
        grid=(b // group, t // tq),
        in_specs=[qspec] + mspecs + mspecs,
        out_specs=qspec,
        out_shape=jax.ShapeDtypeStruct(q.shape, F32),
        compiler_params=_params("parallel", "arbitrary"),
        name="memory_cross_attention",
    )(q, *([mem_k] * group), *([mem_v] * group))


def _swiglu(x, wg_b, wu_b, wd_b):
    g = jnp.dot(x, wg_b[...], preferred_element_type=F32)
    u = jnp.dot(x, wu_b[...], preferred_element_type=F32)
    return jnp.dot((_silu(g) * u).astype(BF16), wd_b[...], preferred_element_type=F32)


def _cast_weights(wg_ref, wu_ref, wd_ref, wg_b, wu_b, wd_b):
    wg_b[...] = wg_ref[0].astype(BF16)
    wu_b[...] = wu_ref[0].astype(BF16)
    wd_b[...] = wd_ref[0].astype(BF16)


def _shared_body(x_ref, wg_ref, wu_ref, wd_ref, o_ref, wg_b, wu_b, wd_b):
    @pl.when(pl.program_id(0) == 0)
    def _():
        _cast_weights(wg_ref, wu_ref, wd_ref, wg_b, wu_b, wd_b)

    o_ref[...] = _swiglu(x_ref[...].astype(BF16), wg_b, wu_b, wd_b)


def _shared_mlp(x, w_gate, w_up, w_down, rows):
    r, d = x.shape
    f = w_gate.shape[2]
    wspec = lambda shape: pl.BlockSpec((1,) + shape, lambda i: (0, 0, 0))
    return pl.pallas_call(
        _shared_body,
        grid=(r // rows,),
        in_specs=[pl.BlockSpec((rows, d), lambda i: (i, 0)), wspec((d, f)), wspec((d, f)), wspec((f, d))],
        out_specs=pl.BlockSpec((rows, d), lambda i: (i, 0)),
        out_shape=jax.ShapeDtypeStruct((r, d), F32),
        scratch_shapes=[pltpu.VMEM((d, f), BF16), pltpu.VMEM((d, f), BF16), pltpu.VMEM((f, d), BF16)],
        compiler_params=_params("arbitrary"),
        name="shared_mlp",
    )(x, w_gate, w_up, w_down)


def _expert_body(be_ref, used_ref, shift_ref, tok_ref, x_hbm, wg_ref, wu_ref, wd_ref, o_ref, wg_b, wu_b, wd_b,
                 xbuf0, xbuf1, sem):
    b = pl.program_id(0)
    rows = o_ref.shape[0]
    n_used = used_ref[0]
    bufs = (xbuf0, xbuf1)
    last_entry = tok_ref.shape[0] - 1

    def row_token(first_entry, r):
        return tok_ref[jnp.minimum(first_entry + r, last_entry)]

    def first_entry(blk):
        return blk * rows - shift_ref[blk]

    def wait_gather(parity):
        pltpu.make_async_copy(x_hbm.at[pl.ds(0, rows), :], bufs[parity], sem.at[parity]).wait()

    @pl.when(b == 0)
    def _():
        def row(r, carry):
            src = x_hbm.at[pl.ds(row_token(first_entry(0), r), 1), :]
            pltpu.make_async_copy(src, xbuf0.at[pl.ds(r, 1), :], sem.at[0]).start()
            return carry

        lax.fori_loop(0, rows, row, 0, unroll=8)

    @pl.when(jnp.logical_and(b < n_used, jnp.logical_or(b == 0, be_ref[b] != be_ref[jnp.maximum(b - 1, 0)])))
    def _():
        _cast_weights(wg_ref, wu_ref, wd_ref, wg_b, wu_b, wd_b)

    def run(parity):
        nxt = first_entry(jnp.minimum(b + 1, n_used - 1))
        wait_gather(parity)
        x = bufs[parity][...].astype(BF16)
        for r in range(rows):
            src = x_hbm.at[pl.ds(row_token(nxt, r), 1), :]
            pltpu.make_async_copy(src, bufs[1 - parity].at[pl.ds(r, 1), :], sem.at[1 - parity]).start()
        o_ref[...] = _swiglu(x, wg_b, wu_b, wd_b)

        @pl.when(b == n_used - 1)
        def _():
            wait_gather(1 - parity)

    for parity in (0, 1):
        pl.when(jnp.logical_and(b < n_used, b % 2 == parity))(functools.partial(run, parity))

    @pl.when(b >= n_used)
    def _():
        o_ref[...] = jnp.zeros_like(o_ref)


def _expert_mlp(x, slot_token, block_shift, block_expert, n_used, w_gate, w_up, w_down, rows):
    d = x.shape[1]
    f = w_gate.shape[2]
    n_blocks = block_expert.shape[0]
    wspec = lambda shape: pl.BlockSpec((1,) + shape, lambda i, be, nu, sh, tok: (be[i], 0, 0))
    return pl.pallas_call(
        _expert_body,
        grid_spec=pltpu.PrefetchScalarGridSpec(
            num_scalar_prefetch=4,
            grid=(n_blocks,),
            in_specs=[pl.BlockSpec(memory_space=pl.ANY), wspec((d, f)), wspec((d, f)), wspec((f, d))],
            out_specs=pl.BlockSpec((rows, d), lambda i, be, nu, sh, tok: (i, 0)),
            scratch_shapes=[pltpu.VMEM((d, f), BF16), pltpu.VMEM((d, f), BF16), pltpu.VMEM((f, d), BF16),
                            pltpu.VMEM((rows, d), F32), pltpu.VMEM((rows, d), F32), pltpu.SemaphoreType.DMA((2,))],
        ),
        out_shape=jax.ShapeDtypeStruct((n_blocks * rows, d), F32),
        compiler_params=_params("arbitrary"),
        name="expert_mlp",
    )(block_expert, n_used, block_shift, slot_token, x, w_gate, w_up, w_down)


def _combine_body(slot_ref, x_ref, sh_ref, gate_ref, g_ref, yb_hbm, o0_ref, o1_ref, buf, sem, *, first_tiles):
    i = pl.program_id(0)
    tm = x_ref.shape[0]

    def start_gather(tile, b):
        base = tile * (tm * TOP_K)

        def token(r, carry):
            for k in range(TOP_K):
                src = yb_hbm.at[pl.ds(slot_ref[base + r * TOP_K + k], 1), :]
                pltpu.make_async_copy(src, buf.at[b, pl.ds(k * tm + r, 1), :], sem.at[b]).start()
            return carry

        lax.fori_loop(0, tm, token, 0)

    @pl.when(i == 0)
    def _():
        start_gather(0, 0)

    @pl.when(i + 1 < pl.num_programs(0))
    def _():
        start_gather(i + 1, (i + 1) % 2)

    b = i % 2
    pltpu.make_async_copy(yb_hbm.at[pl.ds(0, TOP_K * tm), :], buf.at[b], sem.at[b]).wait()
    routed = buf[b, 0:tm, :] * gate_ref[:, 0:1]
    for k in range(1, TOP_K):
        routed = routed + buf[b, k * tm:(k + 1) * tm, :] * gate_ref[:, k:k + 1]
    x = x_ref[...] + (routed + sh_ref[...])
    y = (x * lax.rsqrt(jnp.mean(x * x, axis=-1, keepdims=True) + RMS_EPS)) * g_ref[...]

    @pl.when(i < first_tiles)
    def _():
        o0_ref[...] = y

    @pl.when(i >= first_tiles)
    def _():
        o1_ref[...] = y


def _combine_final_norm(x, yb, slot, gates, shared, gain, split):
    m, d = x.shape
    tm = COMBINE_ROWS
    assert m % tm == 0 and split % tm == 0 and 0 < split < m and yb.shape[0] >= TOP_K * tm
    first_tiles = split // tm
    row = pl.BlockSpec((tm, d), lambda i, s: (i, 0))
    return pl.pallas_call(
        functools.partial(_combine_body, first_tiles=first_tiles),
        grid_spec=pltpu.PrefetchScalarGridSpec(
            num_scalar_prefetch=1,
            grid=(m // tm,),
            in_specs=[row, row, pl.BlockSpec((tm, TOP_K), lambda i, s: (i, 0)),
                      pl.BlockSpec((1, d), lambda i, s: (0, 0)), pl.BlockSpec(memory_space=pl.ANY)],
            out_specs=[pl.BlockSpec((tm, d), lambda i, s: (jnp.minimum(i, first_tiles - 1), 0)),
                       pl.BlockSpec((tm, d), lambda i, s: (jnp.maximum(i - first_tiles, 0), 0))],
            scratch_shapes=[pltpu.VMEM((2, TOP_K * tm, d), F32), pltpu.SemaphoreType.DMA((2,))],
        ),
        out_shape=[jax.ShapeDtypeStruct((split, d), F32), jax.ShapeDtypeStruct((m - split, d), F32)],
        compiler_params=_params("arbitrary"),
        name="combine_final_norm",
    )(slot.reshape(-1), x, shared, gates, gain.reshape(1, d), yb)


def _route_body(lg_ref, rb_ref, idx_ref, gate_ref, rank_ref, cnt_ref, seen_ref):
    @pl.when(pl.program_id(0) == 0)
    def _():
        seen_ref[...] = jnp.zeros_like(seen_ref)

    tm, n_exp = lg_ref.shape
    scores = _sigmoid(lg_ref[...])
    biased = scores + rb_ref[...]
    expert = lax.broadcasted_iota(jnp.int32, (tm, n_exp), 1).astype(F32)
    chosen = jnp.zeros((tm, n_exp), F32)
    picks = []
    for _ in range(TOP_K):
        best = jnp.max(biased, axis=-1, keepdims=True)
        first = jnp.min(jnp.where(biased == best, expert, float(n_exp)), axis=-1, keepdims=True)
        hit = expert == first
        picks.append((first, hit, jnp.sum(jnp.where(hit, scores, 0.0), axis=-1, keepdims=True)))
        biased = jnp.where(hit, -jnp.inf, biased)
        chosen = jnp.where(hit, 1.0, chosen)

    r = lax.broadcasted_iota(jnp.int32, (tm, tm), 0)
    c = lax.broadcasted_iota(jnp.int32, (tm, tm), 1)
    earlier = jnp.where(c < r, 1.0, 0.0).astype(BF16)
    rank_all = jnp.dot(earlier, chosen.astype(BF16), preferred_element_type=F32) + seen_ref[...]

    total = picks[0][2]
    for _, _, sel in picks[1:]:
        total = total + sel
    col = lax.broadcasted_iota(jnp.int32, (tm, TOP_K), 1)
    idx = jnp.zeros((tm, TOP_K), F32)
    gate = jnp.zeros((tm, TOP_K), F32)
    rank = jnp.zeros((tm, TOP_K), F32)
    for k, (first, hit, sel) in enumerate(picks):
        idx = jnp.where(col == k, first, idx)
        gate = jnp.where(col == k, sel / total * ROUTED_SCALE, gate)
        rank = jnp.where(col == k, jnp.sum(jnp.where(hit, rank_all, 0.0), axis=-1, keepdims=True), rank)
    idx_ref[...] = idx.astype(jnp.int32)
    gate_ref[...] = gate
    rank_ref[...] = rank.astype(jnp.int32)
    seen_ref[...] += jnp.sum(chosen, axis=0, keepdims=True)
    cnt_ref[...] = seen_ref[...].astype(jnp.int32)


def _route(logits, router_bias, tm=256):
    t, n_exp = logits.shape
    assert t % tm == 0
    row = lambda width: pl.BlockSpec((tm, width), lambda i: (i, 0))
    one = pl.BlockSpec((1, n_exp), lambda i: (0, 0))
    return pl.pallas_call(
        _route_body,
        grid=(t // tm,),
        in_specs=[row(n_exp), one],
        out_specs=[row(TOP_K), row(TOP_K), row(TOP_K), one],
        out_shape=[jax.ShapeDtypeStruct((t, TOP_K), jnp.int32), jax.ShapeDtypeStruct((t, TOP_K), F32),
                   jax.ShapeDtypeStruct((t, TOP_K), jnp.int32), jax.ShapeDtypeStruct((1, n_exp), jnp.int32)],
        scratch_shapes=[pltpu.VMEM((1, n_exp), F32)],
        compiler_params=_params("arbitrary"),
        name="router_topk",
    )(logits, router_bias.reshape(1, n_exp).astype(F32))


def _moe(hn, logits, router_bias, w_gate, w_up, w_down, ws_gate, ws_up, ws_down):
    t = logits.shape[0]
    idx, gates, rank, counts = _route(logits, router_bias)

    padded = (counts[0] + EXPERT_ROWS - 1) // EXPERT_ROWS * EXPERT_ROWS
    pad_end = jnp.cumsum(padded)
    pad_start = pad_end - padded
    experts = jnp.arange(N_EXPERTS, dtype=jnp.int32)
    slot = jnp.sum(jnp.where(idx[:, :, None] == experts, pad_start, 0), axis=-1) + rank
    n_blocks = -(-(t * TOP_K) // EXPERT_ROWS) + N_EXPERTS
    block_start = jnp.arange(n_blocks, dtype=jnp.int32) * EXPERT_ROWS
    block_e = jnp.minimum(jnp.sum(pad_end[None, :] <= block_start[:, None], axis=1), N_EXPERTS - 1).astype(jnp.int32)
    n_used = (pad_end[-1:] // EXPERT_ROWS).astype(jnp.int32)
    _, tok_sorted = lax.sort_key_val(slot.reshape(-1), jnp.repeat(jnp.arange(t, dtype=jnp.int32), TOP_K))
    first = jnp.cumsum(counts[0]) - counts[0]
    block_shift = (pad_start - first)[block_e].astype(jnp.int32)
    yb = _expert_mlp(hn, tok_sorted, block_shift, block_e, n_used, w_gate, w_up, w_down, EXPERT_ROWS)
    shared = _shared_mlp(hn, ws_gate[None], ws_up[None], ws_down[None], EXPERT_ROWS)
    return yb, slot.astype(jnp.int32), gates, shared


def kernel(x_prompt, x_sample, mem_prompt, cache_sb_k, cache_sb_v, page_table, state_ret, cache_mem_k, cache_mem_v,
           norm_mix, w_in, ret_gn, sb_gn, sb_bias, w_out, norm_mem, w_ck, w_cv, norm_cross, w_cq, w_co, norm_ffn,
           router_w, router_bias, w_gate, w_up, w_down, ws_gate, ws_up, ws_down, norm_final):
    depth = w_in.shape[0]
    assert depth == 1 and x_prompt.shape[0] == 1
    _, t, d = x_prompt.shape
    b, l, _ = x_sample.shape
    n_pages = page_table.shape[1]
    lyr = 0
    xp = x_prompt.reshape(t, d)
    xs = x_sample.reshape(b * l, d)
    pos_p = np.arange(t)
    pos_s = n_pages * PAGE_SIZE + np.arange(l)

    w_in_b = w_in[lyr].astype(BF16)
    w_out_b = w_out[lyr].astype(BF16)
    w_cq_b = w_cq[lyr].astype(BF16)
    w_co_b = w_co[lyr].astype(BF16)

    mem = mem_prompt.reshape(-1, d)
    n_mem = mem.shape[0]
    mkv = _mm(mem, jnp.concatenate([w_ck[lyr], w_cv[lyr]], axis=1).astype(BF16), gain=norm_mem[lyr])
    mk_p, mv_p = mkv[:, :MEM_WIDTH], mkv[:, MEM_WIDTH:]

    proj = _mm(xp, w_in_b, gain=norm_mix[lyr])
    o_ret, s_ret_p = _retention_prompt(proj, pos_p, ret_gn[lyr])
    o_sb, sk_p, sv_p = _sb_prompt(proj, sb_bias[lyr], sb_gn[lyr])
    xp = _mm([o_ret, o_sb], w_out_b, res=xp)

    proj = _mm(xs, w_in_b, gain=norm_mix[lyr]).reshape(b, l, -1)
    sk_s, sv_s = proj[:, :, -2 * SB_WIDTH:-SB_WIDTH], proj[:, :, -SB_WIDTH:]
    o_ret, s_ret_s = _retention_sample(proj, state_ret[lyr], pos_s, ret_gn[lyr])
    o_sb = _sb_sample(proj, cache_sb_k[lyr], cache_sb_v[lyr], page_table, sb_bias[lyr], sb_gn[lyr])
    xs = _mm([o_ret.reshape(b * l, -1), o_sb.reshape(b * l, -1)], w_out_b, res=xs)

    q = _mm(xp, w_cq_b, gain=norm_cross[lyr])
    mem_rows = n_mem * N_MEM_HEADS
    o = _cross_attention(q[None], mk_p.reshape(1, mem_rows, HEAD_DIM), mv_p.reshape(1, mem_rows, HEAD_DIM), tq=512,
                         group=1)
    xp = _mm(o[0], w_co_b, res=xp)
    q = _mm(xs, w_cq_b, gain=norm_cross[lyr]).reshape(b, l, MEM_WIDTH)
    o = _cross_attention(q, cache_mem_k[lyr].reshape(b, mem_rows, HEAD_DIM),
                         cache_mem_v[lyr].reshape(b, mem_rows, HEAD_DIM), tq=l, group=CROSS_REQUESTS_PER_STEP)
    xs = _mm(o.reshape(b * l, MEM_WIDTH), w_co_b, res=xs)

    x_all = jnp.concatenate([xp, xs], axis=0)
    logits, hn = _mm(x_all, router_w[lyr].astype(BF16), gain=norm_ffn[lyr], emit_xn=True)
    yb, slot, gates, shared = _moe(hn, logits, router_bias[lyr], w_gate[lyr], w_up[lyr], w_down[lyr],
                                   ws_gate[lyr], ws_up[lyr], ws_down[lyr])
    y_prompt, y_sample = _combine_final_norm(x_all, yb, slot, gates, shared, norm_final, split=t)
    y_prompt = y_prompt.reshape(1, t, d)
    y_sample = y_sample.reshape(b, l, d)
    return (y_prompt, y_sample,
            s_ret_p[None, None],
            sk_p.reshape(1, 1, t, N_SB_HEADS, HEAD_DIM), sv_p.reshape(1, 1, t, N_SB_HEADS, HEAD_DIM),
            mk_p.reshape(1, 1, n_mem, N_MEM_HEADS, HEAD_DIM), mv_p.reshape(1, 1, n_mem, N_MEM_HEADS, HEAD_DIM),
            s_ret_s[None],
            sk_s.reshape(1, b, l, N_SB_HEADS, HEAD_DIM), sv_s.reshape(1, b, l, N_SB_HEADS, HEAD_DIM))
```

```python
import functools

import jax
import jax.numpy as jnp
import numpy as np
from jax import lax
from jax.experimental import pallas as pl
from jax.experimental.pallas import tpu as pltpu

F32 = jnp.float32
BF16 = jnp.bfloat16

HEAD_DIM = 128
N_RET_HEADS = 8
N_SB_HEADS = 8
RET_WIDTH = N_RET_HEADS * HEAD_DIM
SB_WIDTH = N_SB_HEADS * HEAD_DIM
RET_CHUNK = 128
PAGE_SIZE = 128
ROPE_BASE = 10000.0
N_MEM_HEADS = 4
MEM_WIDTH = N_MEM_HEADS * HEAD_DIM
N_EXPERTS = 64
TOP_K = 8
ROUTED_SCALE = 2.5
RMS_EPS = 1e-6
NORM_EPS = 1e-6
QK_SCALE = HEAD_DIM ** -0.5

VMEM_LIMIT_BYTES = 56 * 1024 * 1024
SB_KEYS = 256
SB_QUERIES = 512
SB_PAGES_PER_STEP = 8
EXPERT_ROWS = 256
COMBINE_ROWS = 128
SB_CHUNK = 128
RET_REQUESTS_PER_STEP = 4
CROSS_REQUESTS_PER_STEP = 8
LOG2E = 1.4426950408889634

NT_DIMS = (((1,), (1,)), ((), ()))
TN_DIMS = (((0,), (0,)), ((), ()))


def _params(*sem):
    return pltpu.CompilerParams(dimension_semantics=sem, vmem_limit_bytes=VMEM_LIMIT_BYTES)


def _sigmoid(x):
    return 1.0 / (1.0 + jnp.exp(-x))


def _silu(x):
    return x * _sigmoid(x)


def _mm_body(*refs, n_x, has_norm, has_res, has_into, emit_xn, own_tiles):
    it = iter(refs)
    x_refs = [next(it) for _ in range(n_x)]
    g_ref = next(it) if has_norm else None
    w_ref = next(it)
    r_ref = next(it) if has_res else None
    if has_into:
        next(it)
    o_ref = next(it)
    xo_ref = next(it) if emit_xn else None
    xn_ref = next(it)

    i, j = pl.program_id(0), pl.program_id(1)

    @pl.when(i < own_tiles)
    def _():
        @pl.when(j == 0)
        def _():
            x = x_refs[0][...] if n_x == 1 else jnp.concatenate([r[...] for r in x_refs], axis=1)
            if has_norm:
                x = x * lax.rsqrt(jnp.mean(x * x, axis=-1, keepdims=True) + RMS_EPS)
                x = x * g_ref[...]
            xn_ref[...] = x.astype(BF16)
            if emit_xn:
                xo_ref[...] = x

        acc = jnp.dot(xn_ref[...], w_ref[...], preferred_element_type=F32)
        if has_res:
            acc = r_ref[...] + acc
        o_ref[...] = acc

    @pl.when(i >= own_tiles)
    def _():
        o_ref[...] = jnp.zeros_like(o_ref)


def _mm(xs, w, gain=None, res=None, emit_xn=False, total_rows=None, into=None, row_offset=0, tm=1024, tn=1024):
    xs = xs if isinstance(xs, (list, tuple)) else [xs]
    m = xs[0].shape[0]
    k = sum(x.shape[1] for x in xs)
    n = w.shape[1]
    tm = min(tm, m)
    tn = min(tn, n)
    out_rows = into.shape[0] if into is not None else (total_rows or m)
    assert m % tm == 0 and n % tn == 0 and row_offset % tm == 0 and out_rows % tm == 0
    own_tiles = m // tm
    row_tiles = own_tiles if into is not None else out_rows // tm
    first = row_offset // tm
    own = lambda i: jnp.minimum(i, own_tiles - 1)
    in_specs = [pl.BlockSpec((tm, x.shape[1]), lambda i, j: (own(i), 0)) for x in xs]
    args = list(xs)
    if gain is not None:
        in_specs.append(pl.BlockSpec((1, k), lambda i, j: (0, 0)))
        args.append(gain.reshape(1, k).astype(F32))
    in_specs.append(pl.BlockSpec((k, tn), lambda i, j: (0, j)))
    args.append(w)
    if res is not None:
        in_specs.append(pl.BlockSpec((tm, tn), lambda i, j: (own(i), j)))
        args.append(res)
    aliases = {}
    if into is not None:
        aliases = {len(args): 0}
        in_specs.append(pl.BlockSpec(memory_space=pl.ANY))
        args.append(into)
    out_shape = [jax.ShapeDtypeStruct((out_rows, n), F32)]
    out_specs = [pl.BlockSpec((tm, tn), lambda i, j: (first + i, j))]
    if emit_xn:
        assert out_rows == m
        out_shape.append(jax.ShapeDtypeStruct((m, k), F32))
        out_specs.append(pl.BlockSpec((tm, k), lambda i, j: (i, 0)))
    outs = pl.pallas_call(
        functools.partial(_mm_body, n_x=len(xs), has_norm=gain is not None, has_res=res is not None,
                          has_into=into is not None, emit_xn=emit_xn, own_tiles=own_tiles),
        grid=(row_tiles, n // tn),
        in_specs=in_specs,
        out_specs=out_specs,
        out_shape=out_shape,
        scratch_shapes=[pltpu.VMEM((tm, k), BF16)],
        input_output_aliases=aliases,
        compiler_params=_params("parallel", "arbitrary"),
        name="norm_matmul",
    )(*args)
    return outs if emit_xn else outs[0]


def _rotary_tables(pos):
    half = HEAD_DIM // 2
    inv_freq = np.float32(ROPE_BASE) ** (-np.arange(half, dtype=np.float32) / np.float32(half))
    ang = pos.astype(np.float32)[:, None] * inv_freq[None, :]
    cos, sin = np.cos(ang), np.sin(ang)
    return (jnp.asarray(np.concatenate([cos, cos], axis=-1), F32),
            jnp.asarray(np.concatenate([-sin, sin], axis=-1), F32))


def _retention_tables(length):
    f32 = np.float32
    log_g = np.log1p(-np.exp2(f32(-5.0) - np.arange(N_RET_HEADS, dtype=f32)))
    i = np.arange(length, dtype=f32)
    diff = i[:, None] - i[None, :]
    decay = np.where(diff >= 0, np.exp(np.maximum(diff, f32(0.0))[None] * log_g[:, None, None]), f32(0.0))
    dq = np.exp((i[None, :] + f32(1.0)) * log_g[:, None])
    dk = np.exp((f32(length) - f32(1.0) - i)[None, :] * log_g[:, None])
    ds = np.exp(f32(length) * log_g)
    lanes = (N_RET_HEADS, length, HEAD_DIM)
    tables = (decay, np.broadcast_to(dq[:, :, None], lanes), np.broadcast_to(dk[:, :, None], lanes),
              np.broadcast_to(ds[:, None, None], (N_RET_HEADS, 8, HEAD_DIM)))
    return tuple(jnp.asarray(np.ascontiguousarray(a), F32) for a in tables)


def _rotate(x, cos, sin_signed):
    return x * cos + pltpu.roll(x, HEAD_DIM // 2, 1) * sin_signed


def _retention_step(q, k, v, s, decay, dq, dk, ds):
    qb, kb, vb = q.astype(BF16), k.astype(BF16), v.astype(BF16)
    scores = lax.dot_general(qb, kb, NT_DIMS, preferred_element_type=F32) * decay
    o = jnp.dot(scores.astype(BF16), vb, preferred_element_type=F32)
    o = o + jnp.dot((q * dq).astype(BF16), s.astype(BF16), preferred_element_type=F32)
    kd = (k * dk).astype(BF16)
    s_new = ds * s + lax.dot_general(kd, vb, TN_DIMS, preferred_element_type=F32)
    return o, s_new


def _gated_layernorm(o, gn, gate):
    c = o - jnp.mean(o, axis=-1, keepdims=True)
    y = c * lax.rsqrt(jnp.mean(c * c, axis=-1, keepdims=True) + NORM_EPS)
    return (y * gn) * _silu(gate)


def _ret_prompt_body(q_ref, k_ref, v_ref, g_ref, cos_ref, sin_ref, decay_ref, dq_ref, dk_ref, ds_ref, gn_ref,
                     o_ref, sfin_ref, s_ref):
    c = pl.program_id(0)

    @pl.when(c == 0)
    def _():
        s_ref[...] = jnp.zeros_like(s_ref)

    cos, sin = cos_ref[...], sin_ref[...]
    for h in range(N_RET_HEADS):
        cols = slice(h * HEAD_DIM, (h + 1) * HEAD_DIM)
        q = _rotate(q_ref[:, cols], cos, sin)
        k = _rotate(k_ref[:, cols], cos, sin) * QK_SCALE
        o, s_new = _retention_step(q, k, v_ref[:, cols], s_ref[h], decay_ref[h], dq_ref[h], dk_ref[h],
                                   ds_ref[h, 0:1, :])
        s_ref[h] = s_new
        o_ref[:, cols] = _gated_layernorm(o, gn_ref[:, cols], g_ref[:, cols])

    @pl.when(c == pl.num_programs(0) - 1)
    def _():
        sfin_ref[...] = s_ref[...]


def _retention_prompt(proj, pos, gn):
    t = proj.shape[0]
    cos, sin = _rotary_tables(pos)
    decay, dq, dk, ds = _retention_tables(RET_CHUNK)
    segment = lambda j: pl.BlockSpec((RET_CHUNK, RET_WIDTH), lambda c: (c, j))
    rot = pl.BlockSpec((RET_CHUNK, HEAD_DIM), lambda c: (c, 0))
    full = lambda a: pl.BlockSpec(a.shape, lambda c: (0,) * a.ndim)
    gn2 = gn.reshape(1, RET_WIDTH)
    state = jax.ShapeDtypeStruct((N_RET_HEADS, HEAD_DIM, HEAD_DIM), F32)
    return pl.pallas_call(
        _ret_prompt_body,
        grid=(t // RET_CHUNK,),
        in_specs=[segment(0), segment(1), segment(2), segment(3), rot, rot, full(decay), full(dq), full(dk), full(ds),
                  full(gn2)],
        out_specs=[segment(0), pl.BlockSpec(state.shape, lambda c: (0, 0, 0))],
        out_shape=[jax.ShapeDtypeStruct((t, RET_WIDTH), F32), state],
        scratch_shapes=[pltpu.VMEM(state.shape, F32)],
        compiler_params=_params("arbitrary"),
        name="retention_prompt",
    )(proj, proj, proj, proj, cos, sin, decay, dq, dk, ds, gn2)


def _ret_sample_body(q_ref, k_ref, v_ref, g_ref, st_ref, cos_ref, sin_ref, decay_ref, dq_ref, dk_ref, ds_ref, gn_ref,
                     o_ref, snew_ref):
    cos, sin = cos_ref[...], sin_ref[...]
    for r in range(q_ref.shape[0]):
        for h in range(N_RET_HEADS):
            cols = slice(h * HEAD_DIM, (h + 1) * HEAD_DIM)
            q = _rotate(q_ref[r, :, cols], cos, sin)
            k = _rotate(k_ref[r, :, cols], cos, sin) * QK_SCALE
            o, s_new = _retention_step(q, k, v_ref[r, :, cols], st_ref[r, h], decay_ref[h], dq_ref[h], dk_ref[h],
                                       ds_ref[h, 0:1, :])
            snew_ref[r, h] = s_new
            o_ref[r, :, cols] = _gated_layernorm(o, gn_ref[:, cols], g_ref[r, :, cols])


def _retention_sample(proj, state, pos, gn):
    b, l, _ = proj.shape
    cos, sin = _rotary_tables(pos)
    decay, dq, dk, ds = _retention_tables(l)
    per_step = RET_REQUESTS_PER_STEP
    assert b % per_step == 0
    segment = lambda j: pl.BlockSpec((per_step, l, RET_WIDTH), lambda i: (i, 0, j))
    st = pl.BlockSpec((per_step, N_RET_HEADS, HEAD_DIM, HEAD_DIM), lambda i: (i, 0, 0, 0))
    full = lambda a: pl.BlockSpec(a.shape, lambda i: (0,) * a.ndim)
    gn2 = gn.reshape(1, RET_WIDTH)
    return pl.pallas_call(
        _ret_sample_body,
        grid=(b // per_step,),
        in_specs=[segment(0), segment(1), segment(2), segment(3), st, full(cos), full(sin), full(decay), full(dq),
                  full(dk), full(ds), full(gn2)],
        out_specs=[segment(0), st],
        out_shape=[jax.ShapeDtypeStruct((b, l, RET_WIDTH), F32), jax.ShapeDtypeStruct(state.shape, F32)],
        compiler_params=_params("parallel"),
        name="retention_sample",
    )(proj, proj, proj, proj, state, cos, sin, decay, dq, dk, ds, gn2)


def _suffix_matrix(n):
    j = lax.broadcasted_iota(jnp.int32, (n, n), 0)
    s = lax.broadcasted_iota(jnp.int32, (n, n), 1)
    return jnp.where(j > s, 1.0, 0.0).astype(BF16)


def _neg_abs(x):
    bits = lax.bitcast_convert_type(x, jnp.uint32) | jnp.uint32(0x80000000)
    return lax.bitcast_convert_type(bits, F32)


def _sb_block_log2(z2, tri, mask):
    rows, keys = z2.shape
    chunk = min(rows, SB_CHUNK)
    suffix = tri.shape[1]
    log2_w, dropped = [], []
    for c in range(0, rows, chunk):
        parts, later = [], None
        for s in reversed(range(0, keys, suffix)):
            zc = z2[c:c + chunk, s:s + suffix]
            neg_log_keep = jnp.maximum(zc, 0.0) + jnp.log2(1.0 + jnp.exp2(_neg_abs(zc)))
            log_beta = zc - neg_log_keep
            if mask is not None:
                neg_log_keep = jnp.where(mask[c:c + chunk, s:s + suffix], neg_log_keep, 0.0)
            between = jnp.dot(neg_log_keep.astype(BF16), tri, preferred_element_type=F32)
            total = jnp.sum(neg_log_keep, axis=-1, keepdims=True)
            if later is None:
                parts.append(log_beta - between)
                later = total
            else:
                parts.append(log_beta - between - later)
                later = later + total
        log2_w.append(jnp.concatenate(parts[::-1], axis=1))
        dropped.append(later)
    return jnp.concatenate(log2_w, axis=0), jnp.concatenate(dropped, axis=0)


def _head_rmsnorm(o, gn):
    return (o * lax.rsqrt(jnp.mean(o * o, axis=-1, keepdims=True) + RMS_EPS)) * gn


def _sb_prompt_body(bias_ref, q_ref, k_ref, v_ref, gn_ref, o_ref, ko_ref, vo_ref, kb_ref, vb_ref, tri_ref):
    h = pl.program_id(0)
    i = pl.program_id(1)

    @pl.when(i == 0)
    def _():
        k, v = k_ref[...], v_ref[...]
        ko_ref[...] = k
        vo_ref[...] = v
        kb_ref[...] = k.astype(BF16)
        vb_ref[...] = v.astype(BF16)
        tri_ref[...] = _suffix_matrix(SB_KEYS)

    q = (q_ref[...] * (QK_SCALE * LOG2E)).astype(BF16)
    bias = bias_ref[h] * LOG2E
    tri = tri_ref[...]

    def key_rows(j):
        start = j * SB_KEYS
        return pl.ds(start if isinstance(j, int) else pl.multiple_of(start, SB_KEYS), SB_KEYS)

    def logits(j):
        return lax.dot_general(q, kb_ref[key_rows(j), :], NT_DIMS, preferred_element_type=F32) + bias

    def weighted_values(log2_w, j, mask=None):
        a = jnp.exp2(log2_w)
        if mask is not None:
            a = jnp.where(mask, a, 0.0)
        return jnp.dot(a.astype(BF16), vb_ref[key_rows(j), :], preferred_element_type=F32)

    ratio = SB_QUERIES // SB_KEYS
    r = lax.broadcasted_iota(jnp.int32, (SB_QUERIES, SB_KEYS), 0)
    c = lax.broadcasted_iota(jnp.int32, (SB_QUERIES, SB_KEYS), 1)
    acc = jnp.zeros((SB_QUERIES, HEAD_DIM), F32)
    drop = jnp.zeros((SB_QUERIES, 1), F32)
    for d in reversed(range(ratio)):
        mask = c + d * SB_KEYS < r
        log2_w, dropped = _sb_block_log2(logits(i * ratio + d), tri, mask)
        acc = acc + weighted_values(log2_w - drop, i * ratio + d, mask)
        drop = drop + dropped

    n_old = i * ratio

    def step(n, state):
        z_prev, log2_w_prev, acc, drop = state
        z_new = logits(jnp.maximum(n_old - 1 - n, 0))
        log2_w, dropped = _sb_block_log2(z_prev, tri, None)
        out = weighted_values(log2_w_prev, jnp.clip(n_old + 1 - n, 0, n_old))
        return z_new, log2_w - drop, acc + out, drop + dropped

    idle = jnp.full((SB_QUERIES, SB_KEYS), -1e30, F32)
    z0 = logits(jnp.maximum(n_old - 1, 0))
    _, log2_w_last, acc, _ = lax.fori_loop(1, n_old + 1, step, (z0, idle, acc, drop))
    acc = acc + weighted_values(log2_w_last, 0)
    o_ref[...] = _head_rmsnorm(acc, gn_ref[...])


def _sb_prompt(proj, bias, gn):
    t, width = proj.shape
    assert t % SB_QUERIES == 0 and SB_QUERIES % SB_KEYS == 0
    q0, k0, v0 = [(width - n * SB_WIDTH) // HEAD_DIM for n in (3, 2, 1)]
    qspec = pl.BlockSpec((SB_QUERIES, HEAD_DIM), lambda h, i, b: (i, q0 + h))
    kspec = pl.BlockSpec((t, HEAD_DIM), lambda h, i, b: (0, k0 + h))
    vspec = pl.BlockSpec((t, HEAD_DIM), lambda h, i, b: (0, v0 + h))
    return pl.pallas_call(
        _sb_prompt_body,
        grid_spec=pltpu.PrefetchScalarGridSpec(
            num_scalar_prefetch=1,
            grid=(N_SB_HEADS, t // SB_QUERIES),
            in_specs=[qspec, kspec, vspec, pl.BlockSpec((1, HEAD_DIM), lambda h, i, b: (0, h))],
            out_specs=[pl.BlockSpec((SB_QUERIES, HEAD_DIM), lambda h, i, b: (i, h)),
                       pl.BlockSpec((t, HEAD_DIM), lambda h, i, b: (0, h)),
                       pl.BlockSpec((t, HEAD_DIM), lambda h, i, b: (0, h))],
            scratch_shapes=[pltpu.VMEM((t, HEAD_DIM), BF16), pltpu.VMEM((t, HEAD_DIM), BF16),
                            pltpu.VMEM((SB_KEYS, SB_KEYS), BF16)],
        ),
        out_shape=[jax.ShapeDtypeStruct((t, SB_WIDTH), F32)] * 3,
        compiler_params=_params("parallel", "arbitrary"),
        name="stick_breaking_prompt",
    )(bias.astype(F32), proj, proj, proj, gn.reshape(1, SB_WIDTH))


def _sb_sample_body(pt_ref, bias_ref, q_ref, kn_ref, vn_ref, *rest):
    kp_refs = rest[:SB_PAGES_PER_STEP]
    vp_refs = rest[SB_PAGES_PER_STEP:2 * SB_PAGES_PER_STEP]
    gn_ref, o_ref, acc_ref, drop_ref = rest[2 * SB_PAGES_PER_STEP:]
    step = pl.program_id(1)
    l = q_ref.shape[1]
    rows = N_SB_HEADS * l
    tri = _suffix_matrix(PAGE_SIZE)
    bias = jnp.concatenate([jnp.full((l, 1), bias_ref[h] * LOG2E, F32) for h in range(N_SB_HEADS)], axis=0)
    qs = [(q_ref[0, :, h * HEAD_DIM:(h + 1) * HEAD_DIM] * (QK_SCALE * LOG2E)).astype(BF16)
          for h in range(N_SB_HEADS)]

    def logits(keys):
        z = [lax.dot_general(qs[h], keys[h], NT_DIMS, preferred_element_type=F32) for h in range(N_SB_HEADS)]
        return jnp.concatenate(z, axis=0) + bias

    def weighted_values(a, values):
        out = [jnp.dot(a[h * l:(h + 1) * l, :].astype(BF16), values[h], preferred_element_type=F32)
               for h in range(N_SB_HEADS)]
        return jnp.concatenate(out, axis=0)

    @pl.when(step == 0)
    def _():
        pad = jnp.zeros((PAGE_SIZE - l, HEAD_DIM), F32)
        kn = [jnp.concatenate([kn_ref[0, :, h * HEAD_DIM:(h + 1) * HEAD_DIM], pad], axis=0).astype(BF16)
              for h in range(N_SB_HEADS)]
        vn = [jnp.concatenate([vn_ref[0, :, h * HEAD_DIM:(h + 1) * HEAD_DIM], pad], axis=0).astype(BF16)
              for h in range(N_SB_HEADS)]
        qi = lax.broadcasted_iota(jnp.int32, (rows, PAGE_SIZE), 0) % l
        kj = lax.broadcasted_iota(jnp.int32, (rows, PAGE_SIZE), 1)
        mask = kj < qi
        log2_w, dropped = _sb_block_log2(logits(kn), tri, mask)
        acc_ref[...] = weighted_values(jnp.where(mask, jnp.exp2(log2_w), 0.0), vn)
        drop_ref[...] = dropped

    head_rows = lambda h: pl.ds(h, PAGE_SIZE, stride=N_SB_HEADS)
    z = jnp.concatenate([logits([kp_ref[head_rows(h), :].astype(BF16) for h in range(N_SB_HEADS)])
                         for kp_ref in kp_refs], axis=0)
    log2_w, dropped = _sb_block_log2(z, tri, None)
    drop = drop_ref[...]
    acc = acc_ref[...]
    for n, vp_ref in enumerate(vp_refs):
        page = slice(n * rows, (n + 1) * rows)
        a = jnp.exp2(log2_w[page, :] - drop)
        acc = acc + weighted_values(a, [vp_ref[head_rows(h), :].astype(BF16) for h in range(N_SB_HEADS)])
        drop = drop + dropped[page, :]
    acc_ref[...] = acc
    drop_ref[...] = drop

    @pl.when(step == pl.num_programs(1) - 1)
    def _():
        for h in range(N_SB_HEADS):
            cols = slice(h * HEAD_DIM, (h + 1) * HEAD_DIM)
            o_ref[0, :, cols] = _head_rmsnorm(acc[h * l:(h + 1) * l, :], gn_ref[:, cols])


def _sb_sample(proj, cache_k, cache_v, page_table, bias, gn):
    b, l, width = proj.shape
    n_pages = page_table.shape[1]
    assert n_pages % SB_PAGES_PER_STEP == 0
    n_phys = cache_k.shape[0]
    page_rows = PAGE_SIZE * N_SB_HEADS
    cache_k = cache_k.reshape(n_phys, page_rows, HEAD_DIM)
    cache_v = cache_v.reshape(n_phys, page_rows, HEAD_DIM)
    segment = lambda j: pl.BlockSpec((1, l, SB_WIDTH), lambda i, s, pt, bs: (i, 0, j))
    last = width // SB_WIDTH - 1

    def page(n):
        return pl.BlockSpec((None, page_rows, HEAD_DIM),
                            lambda i, s, pt, bs: (pt[i, n_pages - 1 - (s * SB_PAGES_PER_STEP + n)], 0, 0))

    pages = [page(n) for n in range(SB_PAGES_PER_STEP)]
    return pl.pallas_call(
        _sb_sample_body,
        grid_spec=pltpu.PrefetchScalarGridSpec(
            num_scalar_prefetch=2,
            grid=(b, n_pages // SB_PAGES_PER_STEP),
            in_specs=[segment(last - 2), segment(last - 1), segment(last)] + pages + pages
                     + [pl.BlockSpec((1, SB_WIDTH), lambda i, s, pt, bs: (0, 0))],
            out_specs=segment(0),
            scratch_shapes=[pltpu.VMEM((N_SB_HEADS * l, HEAD_DIM), F32), pltpu.VMEM((N_SB_HEADS * l, 1), F32)],
        ),
        out_shape=jax.ShapeDtypeStruct((b, l, SB_WIDTH), F32),
        compiler_params=_params("parallel", "arbitrary"),
        name="stick_breaking_sample",
    )(page_table, bias.astype(F32), proj, proj, proj, *([cache_k] * SB_PAGES_PER_STEP),
      *([cache_v] * SB_PAGES_PER_STEP), gn.reshape(1, SB_WIDTH))


def _cross_body(q_ref, *refs):
    group = q_ref.shape[0]
    k_refs, v_refs, o_ref = refs[:group], refs[group:2 * group], refs[2 * group]
    tq = q_ref.shape[1]
    n_mem = k_refs[0].shape[0] // N_MEM_HEADS
    pairs = [(g, h) for g in range(group) for h in range(N_MEM_HEADS)]
    cols = lambda h: slice(h * HEAD_DIM, (h + 1) * HEAD_DIM)
    head_rows = lambda h: pl.ds(h, n_mem, stride=N_MEM_HEADS)
    scores = [lax.dot_general(q_ref[g, :, cols(h)].astype(BF16), k_refs[g][head_rows(h), :].astype(BF16), NT_DIMS,
                              preferred_element_type=F32) for g, h in pairs]
    s = jnp.concatenate(scores, axis=0) * QK_SCALE
    e = jnp.exp(s - jnp.max(s, axis=-1, keepdims=True))
    p = e / jnp.sum(e, axis=-1, keepdims=True)
    for n, (g, h) in enumerate(pairs):
        v = v_refs[g][head_rows(h), :].astype(BF16)
        o_ref[g, :, cols(h)] = jnp.dot(p[n * tq:(n + 1) * tq, :].astype(BF16), v, preferred_element_type=F32)


def _cross_attention(q, mem_k, mem_v, tq, group):
    b, t, _ = q.shape
    m = mem_k.shape[1]
    tq = min(tq, t)
    assert b % group == 0 and t % tq == 0
    qspec = pl.BlockSpec((group, tq, MEM_WIDTH), lambda i, j: (i, j, 0))
    mspec = lambda g: pl.BlockSpec((None, m, HEAD_DIM), lambda i, j: (i * group + g, 0, 0))
    mspecs = [mspec(g) for g in range(group)]
    return pl.pallas_call(
        _cross_body,
        grid=(b // group, t // tq),
        in_specs=[qspec] + mspecs + mspecs,
        out_specs=qspec,
        out_shape=jax.ShapeDtypeStruct(q.shape, F32),
        compiler_params=_params("parallel", "arbitrary"),
        name="memory_cross_attention",
    )(q, *([mem_k] * group), *([mem_v] * group))


def _swiglu(x, wg_b, wu_b, wd_b):
    g = jnp.dot(x, wg_b[...], preferred_element_type=F32)
    u = jnp.dot(x, wu_b[...], preferred_element_type=F32)
    return jnp.dot((_silu(g) * u).astype(BF16), wd_b[...], preferred_element_type=F32)


def _cast_weights(wg_ref, wu_ref, wd_ref, wg_b, wu_b, wd_b):
    wg_b[...] = wg_ref[0].astype(BF16)
    wu_b[...] = wu_ref[0].astype(BF16)
    wd_b[...] = wd_ref[0].astype(BF16)


def _shared_body(x_ref, wg_ref, wu_ref, wd_ref, o_ref, wg_b, wu_b, wd_b):
    @pl.when(pl.program_id(0) == 0)
    def _():
        _cast_weights(wg_ref, wu_ref, wd_ref, wg_b, wu_b, wd_b)

    o_ref[...] = _swiglu(x_ref[...].astype(BF16), wg_b, wu_b, wd_b)


def _shared_mlp(x, w_gate, w_up, w_down, rows):
    r, d = x.shape
    f = w_gate.shape[2]
    wspec = lambda shape: pl.BlockSpec((1,) + shape, lambda i: (0, 0, 0))
    return pl.pallas_call(
        _shared_body,
        grid=(r // rows,),
        in_specs=[pl.BlockSpec((rows, d), lambda i: (i, 0)), wspec((d, f)), wspec((d, f)), wspec((f, d))],
        out_specs=pl.BlockSpec((rows, d), lambda i: (i, 0)),
        out_shape=jax.ShapeDtypeStruct((r, d), F32),
        scratch_shapes=[pltpu.VMEM((d, f), BF16), pltpu.VMEM((d, f), BF16), pltpu.VMEM((f, d), BF16)],
        compiler_params=_params("arbitrary"),
        name="shared_mlp",
    )(x, w_gate, w_up, w_down)


def _expert_body(be_ref, used_ref, shift_ref, tok_ref, x_hbm, wg_ref, wu_ref, wd_ref, o_ref, wg_b, wu_b, wd_b,
                 xbuf0, xbuf1, sem):
    b = pl.program_id(0)
    rows = o_ref.shape[0]
    n_used = used_ref[0]
    bufs = (xbuf0, xbuf1)
    last_entry = tok_ref.shape[0] - 1

    def row_token(first_entry, r):
        return tok_ref[jnp.minimum(first_entry + r, last_entry)]

    def first_entry(blk):
        return blk * rows - shift_ref[blk]

    def wait_gather(parity):
        pltpu.make_async_copy(x_hbm.at[pl.ds(0, rows), :], bufs[parity], sem.at[parity]).wait()

    @pl.when(b == 0)
    def _():
        def row(r, carry):
            src = x_hbm.at[pl.ds(row_token(first_entry(0), r), 1), :]
            pltpu.make_async_copy(src, xbuf0.at[pl.ds(r, 1), :], sem.at[0]).start()
            return carry

        lax.fori_loop(0, rows, row, 0, unroll=8)

    @pl.when(jnp.logical_and(b < n_used, jnp.logical_or(b == 0, be_ref[b] != be_ref[jnp.maximum(b - 1, 0)])))
    def _():
        _cast_weights(wg_ref, wu_ref, wd_ref, wg_b, wu_b, wd_b)

    def run(parity):
        nxt = first_entry(jnp.minimum(b + 1, n_used - 1))
        wait_gather(parity)
        x = bufs[parity][...].astype(BF16)
        for r in range(rows):
            src = x_hbm.at[pl.ds(row_token(nxt, r), 1), :]
            pltpu.make_async_copy(src, bufs[1 - parity].at[pl.ds(r, 1), :], sem.at[1 - parity]).start()
        o_ref[...] = _swiglu(x, wg_b, wu_b, wd_b)

        @pl.when(b == n_used - 1)
        def _():
            wait_gather(1 - parity)

    for parity in (0, 1):
        pl.when(jnp.logical_and(b < n_used, b % 2 == parity))(functools.partial(run, parity))

    @pl.when(b >= n_used)
    def _():
        o_ref[...] = jnp.zeros_like(o_ref)


def _expert_mlp(x, slot_token, block_shift, block_expert, n_used, w_gate, w_up, w_down, rows):
    d = x.shape[1]
    f = w_gate.shape[2]
    n_blocks = block_expert.shape[0]
    wspec = lambda shape: pl.BlockSpec((1,) + shape, lambda i, be, nu, sh, tok: (be[i], 0, 0))
    return pl.pallas_call(
        _expert_body,
        grid_spec=pltpu.PrefetchScalarGridSpec(
            num_scalar_prefetch=4,
            grid=(n_blocks,),
            in_specs=[pl.BlockSpec(memory_space=pl.ANY), wspec((d, f)), wspec((d, f)), wspec((f, d))],
            out_specs=pl.BlockSpec((rows, d), lambda i, be, nu, sh, tok: (i, 0)),
            scratch_shapes=[pltpu.VMEM((d, f), BF16), pltpu.VMEM((d, f), BF16), pltpu.VMEM((f, d), BF16),
                            pltpu.VMEM((rows, d), F32), pltpu.VMEM((rows, d), F32), pltpu.SemaphoreType.DMA((2,))],
        ),
        out_shape=jax.ShapeDtypeStruct((n_blocks * rows, d), F32),
        compiler_params=_params("arbitrary"),
        name="expert_mlp",
    )(block_expert, n_used, block_shift, slot_token, x, w_gate, w_up, w_down)


def _combine_body(slot_ref, x_ref, sh_ref, gate_ref, g_ref, yb_hbm, o0_ref, o1_ref, buf, sem, *, first_tiles):
    i = pl.program_id(0)
    tm = x_ref.shape[0]

    def start_gather(tile, b):
        base = tile * (tm * TOP_K)

        def token(r, carry):
            for k in range(TOP_K):
                src = yb_hbm.at[pl.ds(slot_ref[base + r * TOP_K + k], 1), :]
                pltpu.make_async_copy(src, buf.at[b, pl.ds(k * tm + r, 1), :], sem.at[b]).start()
            return carry

        lax.fori_loop(0, tm, token, 0)

    @pl.when(i == 0)
    def _():
        start_gather(0, 0)

    @pl.when(i + 1 < pl.num_programs(0))
    def _():
        start_gather(i + 1, (i + 1) % 2)

    b = i % 2
    pltpu.make_async_copy(yb_hbm.at[pl.ds(0, TOP_K * tm), :], buf.at[b], sem.at[b]).wait()
    routed = buf[b, 0:tm, :] * gate_ref[:, 0:1]
    for k in range(1, TOP_K):
        routed = routed + buf[b, k * tm:(k + 1) * tm, :] * gate_ref[:, k:k + 1]
    x = x_ref[...] + (routed + sh_ref[...])
    y = (x * lax.rsqrt(jnp.mean(x * x, axis=-1, keepdims=True) + RMS_EPS)) * g_ref[...]

    @pl.when(i < first_tiles)
    def _():
        o0_ref[...] = y

    @pl.when(i >= first_tiles)
    def _():
        o1_ref[...] = y


def _combine_final_norm(x, yb, slot, gates, shared, gain, split):
    m, d = x.shape
    tm = COMBINE_ROWS
    assert m % tm == 0 and split % tm == 0 and 0 < split < m and yb.shape[0] >= TOP_K * tm
    first_tiles = split // tm
    row = pl.BlockSpec((tm, d), lambda i, s: (i, 0))
    return pl.pallas_call(
        functools.partial(_combine_body, first_tiles=first_tiles),
        grid_spec=pltpu.PrefetchScalarGridSpec(
            num_scalar_prefetch=1,
            grid=(m // tm,),
            in_specs=[row, row, pl.BlockSpec((tm, TOP_K), lambda i, s: (i, 0)),
                      pl.BlockSpec((1, d), lambda i, s: (0, 0)), pl.BlockSpec(memory_space=pl.ANY)],
            out_specs=[pl.BlockSpec((tm, d), lambda i, s: (jnp.minimum(i, first_tiles - 1), 0)),
                       pl.BlockSpec((tm, d), lambda i, s: (jnp.maximum(i - first_tiles, 0), 0))],
            scratch_shapes=[pltpu.VMEM((2, TOP_K * tm, d), F32), pltpu.SemaphoreType.DMA((2,))],
        ),
        out_shape=[jax.ShapeDtypeStruct((split, d), F32), jax.ShapeDtypeStruct((m - split, d), F32)],
        compiler_params=_params("arbitrary"),
        name="combine_final_norm",
    )(slot.reshape(-1), x, shared, gates, gain.reshape(1, d), yb)


def _route_body(lg_ref, rb_ref, idx_ref, gate_ref, rank_ref, cnt_ref, seen_ref):
    @pl.when(pl.program_id(0) == 0)
    def _():
        seen_ref[...] = jnp.zeros_like(seen_ref)

    tm, n_exp = lg_ref.shape
    scores = _sigmoid(lg_ref[...])
    biased = scores + rb_ref[...]
    expert = lax.broadcasted_iota(jnp.int32, (tm, n_exp), 1).astype(F32)
    chosen = jnp.zeros((tm, n_exp), F32)
    picks = []
    for _ in range(TOP_K):
        best = jnp.max(biased, axis=-1, keepdims=True)
        first = jnp.min(jnp.where(biased == best, expert, float(n_exp)), axis=-1, keepdims=True)
        hit = expert == first
        picks.append((first, hit, jnp.sum(jnp.where(hit, scores, 0.0), axis=-1, keepdims=True)))
        biased = jnp.where(hit, -jnp.inf, biased)
        chosen = jnp.where(hit, 1.0, chosen)

    r = lax.broadcasted_iota(jnp.int32, (tm, tm), 0)
    c = lax.broadcasted_iota(jnp.int32, (tm, tm), 1)
    earlier = jnp.where(c < r, 1.0, 0.0).astype(BF16)
    rank_all = jnp.dot(earlier, chosen.astype(BF16), preferred_element_type=F32) + seen_ref[...]

    total = picks[0][2]
    for _, _, sel in picks[1:]:
        total = total + sel
    col = lax.broadcasted_iota(jnp.int32, (tm, TOP_K), 1)
    idx = jnp.zeros((tm, TOP_K), F32)
    gate = jnp.zeros((tm, TOP_K), F32)
    rank = jnp.zeros((tm, TOP_K), F32)
    for k, (first, hit, sel) in enumerate(picks):
        idx = jnp.where(col == k, first, idx)
        gate = jnp.where(col == k, sel / total * ROUTED_SCALE, gate)
        rank = jnp.where(col == k, jnp.sum(jnp.where(hit, rank_all, 0.0), axis=-1, keepdims=True), rank)
    idx_ref[...] = idx.astype(jnp.int32)
    gate_ref[...] = gate
    rank_ref[...] = rank.astype(jnp.int32)
    seen_ref[...] += jnp.sum(chosen, axis=0, keepdims=True)
    cnt_ref[...] = seen_ref[...].astype(jnp.int32)


def _route(logits, router_bias, tm=256):
    t, n_exp = logits.shape
    assert t % tm == 0
    row = lambda width: pl.BlockSpec((tm, width), lambda i: (i, 0))
    one = pl.BlockSpec((1, n_exp), lambda i: (0, 0))
    return pl.pallas_call(
        _route_body,
        grid=(t // tm,),
        in_specs=[row(n_exp), one],
        out_specs=[row(TOP_K), row(TOP_K), row(TOP_K), one],
        out_shape=[jax.ShapeDtypeStruct((t, TOP_K), jnp.int32), jax.ShapeDtypeStruct((t, TOP_K), F32),
                   jax.ShapeDtypeStruct((t, TOP_K), jnp.int32), jax.ShapeDtypeStruct((1, n_exp), jnp.int32)],
        scratch_shapes=[pltpu.VMEM((1, n_exp), F32)],
        compiler_params=_params("arbitrary"),
        name="router_topk",
    )(logits, router_bias.reshape(1, n_exp).astype(F32))


def _moe(hn, logits, router_bias, w_gate, w_up, w_down, ws_gate, ws_up, ws_down):
    t = logits.shape[0]
    idx, gates, rank, counts = _route(logits, router_bias)

    padded = (counts[0] + EXPERT_ROWS - 1) // EXPERT_ROWS * EXPERT_ROWS
    pad_end = jnp.cumsum(padded)
    pad_start = pad_end - padded
    experts = jnp.arange(N_EXPERTS, dtype=jnp.int32)
    slot = jnp.sum(jnp.where(idx[:, :, None] == experts, pad_start, 0), axis=-1) + rank
    n_blocks = -(-(t * TOP_K) // EXPERT_ROWS) + N_EXPERTS
    block_start = jnp.arange(n_blocks, dtype=jnp.int32) * EXPERT_ROWS
    block_e = jnp.minimum(jnp.sum(pad_end[None, :] <= block_start[:, None], axis=1), N_EXPERTS - 1).astype(jnp.int32)
    n_used = (pad_end[-1:] // EXPERT_ROWS).astype(jnp.int32)
    _, tok_sorted = lax.sort_key_val(slot.reshape(-1), jnp.repeat(jnp.arange(t, dtype=jnp.int32), TOP_K))
    first = jnp.cumsum(counts[0]) - counts[0]
    block_shift = (pad_start - first)[block_e].astype(jnp.int32)
    yb = _expert_mlp(hn, tok_sorted, block_shift, block_e, n_used, w_gate, w_up, w_down, EXPERT_ROWS)
    shared = _shared_mlp(hn, ws_gate[None], ws_up[None], ws_down[None], EXPERT_ROWS)
    return yb, slot.astype(jnp.int32), gates, shared


def kernel(x_prompt, x_sample, mem_prompt, cache_sb_k, cache_sb_v, page_table, state_ret, cache_mem_k, cache_mem_v,
           norm_mix, w_in, ret_gn, sb_gn, sb_bias, w_out, norm_mem, w_ck, w_cv, norm_cross, w_cq, w_co, norm_ffn,
           router_w, router_bias, w_gate, w_up, w_down, ws_gate, ws_up, ws_down, norm_final):
    depth = w_in.shape[0]
    assert depth == 1 and x_prompt.shape[0] == 1
    _, t, d = x_prompt.shape
    b, l, _ = x_sample.shape
    n_pages = page_table.shape[1]
    lyr = 0
    xp = x_prompt.reshape(t, d)
    xs = x_sample.reshape(b * l, d)
    pos_p = np.arange(t)
    pos_s = n_pages * PAGE_SIZE + np.arange(l)

    w_in_b = w_in[lyr].astype(BF16)
    w_out_b = w_out[lyr].astype(BF16)
    w_cq_b = w_cq[lyr].astype(BF16)
    w_co_b = w_co[lyr].astype(BF16)

    mem = mem_prompt.reshape(-1, d)
    n_mem = mem.shape[0]
    mkv = _mm(mem, jnp.concatenate([w_ck[lyr], w_cv[lyr]], axis=1).astype(BF16), gain=norm_mem[lyr])
    mk_p, mv_p = mkv[:, :MEM_WIDTH], mkv[:, MEM_WIDTH:]

    proj = _mm(xp, w_in_b, gain=norm_mix[lyr])
    o_ret, s_ret_p = _retention_prompt(proj, pos_p, ret_gn[lyr])
    o_sb, sk_p, sv_p = _sb_prompt(proj, sb_bias[lyr], sb_gn[lyr])
    xp = _mm([o_ret, o_sb], w_out_b, res=xp)

    proj = _mm(xs, w_in_b, gain=norm_mix[lyr]).reshape(b, l, -1)
    sk_s, sv_s = proj[:, :, -2 * SB_WIDTH:-SB_WIDTH], proj[:, :, -SB_WIDTH:]
    o_ret, s_ret_s = _retention_sample(proj, state_ret[lyr], pos_s, ret_gn[lyr])
    o_sb = _sb_sample(proj, cache_sb_k[lyr], cache_sb_v[lyr], page_table, sb_bias[lyr], sb_gn[lyr])
    xs = _mm([o_ret.reshape(b * l, -1), o_sb.reshape(b * l, -1)], w_out_b, res=xs)

    q = _mm(xp, w_cq_b, gain=norm_cross[lyr])
    mem_rows = n_mem * N_MEM_HEADS
    o = _cross_attention(q[None], mk_p.reshape(1, mem_rows, HEAD_DIM), mv_p.reshape(1, mem_rows, HEAD_DIM), tq=512,
                         group=1)
    x_all = _mm(o[0], w_co_b, res=xp, total_rows=t + b * l)
    q = _mm(xs, w_cq_b, gain=norm_cross[lyr]).reshape(b, l, MEM_WIDTH)
    o = _cross_attention(q, cache_mem_k[lyr].reshape(b, mem_rows, HEAD_DIM),
                         cache_mem_v[lyr].reshape(b, mem_rows, HEAD_DIM), tq=l, group=CROSS_REQUESTS_PER_STEP)
    x_all = _mm(o.reshape(b * l, MEM_WIDTH), w_co_b, res=xs, into=x_all, row_offset=t)

    logits, hn = _mm(x_all, router_w[lyr].astype(BF16), gain=norm_ffn[lyr], emit_xn=True)
    yb, slot, gates, shared = _moe(hn, logits, router_bias[lyr], w_gate[lyr], w_up[lyr], w_down[lyr],
                                   ws_gate[lyr], ws_up[lyr], ws_down[lyr])
    y_prompt, y_sample = _combine_final_norm(x_all, yb, slot, gates, shared, norm_final, split=t)
    y_prompt = y_prompt.reshape(1, t, d)
    y_sample = y_sample.reshape(b, l, d)
    return (y_prompt, y_sample,
            s_ret_p[None, None],
            sk_p.reshape(1, 1, t, N_SB_HEADS, HEAD_DIM), sv_p.reshape(1, 1, t, N_SB_HEADS, HEAD_DIM),
            mk_p.reshape(1, 1, n_mem, N_MEM_HEADS, HEAD_DIM), mv_p.reshape(1, 1, n_mem, N_MEM_HEADS, HEAD_DIM),
            s_ret_s[None],
            sk_s.reshape(1, b, l, N_SB_HEADS, HEAD_DIM), sv_s.reshape(1, b, l, N_SB_HEADS, HEAD_DIM))
```

```python
import functools

import jax
import jax.numpy as jnp
import numpy as np
from jax import lax
from jax.experimental import pallas as pl
from jax.experimental.pallas import tpu as pltpu

F32 = jnp.float32
BF16 = jnp.bfloat16

HEAD_DIM = 128
N_RET_HEADS = 8
N_SB_HEADS = 8
RET_WIDTH = N_RET_HEADS * HEAD_DIM
SB_WIDTH = N_SB_HEADS * HEAD_DIM
RET_CHUNK = 128
PAGE_SIZE = 128
ROPE_BASE = 10000.0
N_MEM_HEADS = 4
MEM_WIDTH = N_MEM_HEADS * HEAD_DIM
N_EXPERTS = 64
TOP_K = 8
ROUTED_SCALE = 2.5
RMS_EPS = 1e-6
NORM_EPS = 1e-6
QK_SCALE = HEAD_DIM ** -0.5

VMEM_LIMIT_BYTES = 56 * 1024 * 1024
SB_KEYS = 256
SB_QUERIES = 512
SB_PAGES_PER_STEP = 8
EXPERT_ROWS = 256
COMBINE_ROWS = 128
SB_CHUNK = 128
RET_REQUESTS_PER_STEP = 4
CROSS_REQUESTS_PER_STEP = 8
LOG2E = 1.4426950408889634

NT_DIMS = (((1,), (1,)), ((), ()))
TN_DIMS = (((0,), (0,)), ((), ()))


def _params(*sem):
    return pltpu.CompilerParams(dimension_semantics=sem, vmem_limit_bytes=VMEM_LIMIT_BYTES)


def _sigmoid(x):
    return 1.0 / (1.0 + jnp.exp(-x))


def _silu(x):
    return x * _sigmoid(x)


def _mm_body(*refs, n_x, has_norm, has_res, has_into, emit_xn, own_tiles):
    it = iter(refs)
    x_refs = [next(it) for _ in range(n_x)]
    g_ref = next(it) if has_norm else None
    w_ref = next(it)
    r_ref = next(it) if has_res else None
    if has_into:
        next(it)
    o_ref = next(it)
    xo_ref = next(it) if emit_xn else None
    xn_ref = next(it)

    i, j = pl.program_id(0), pl.program_id(1)

    @pl.when(i < own_tiles)
    def _():
        @pl.when(j == 0)
        def _():
            x = x_refs[0][...] if n_x == 1 else jnp.concatenate([r[...] for r in x_refs], axis=1)
            if has_norm:
                x = x * lax.rsqrt(jnp.mean(x * x, axis=-1, keepdims=True) + RMS_EPS)
                x = x * g_ref[...]
            xn_ref[...] = x.astype(BF16)
            if emit_xn:
                xo_ref[...] = x

        acc = jnp.dot(xn_ref[...], w_ref[...], preferred_element_type=F32)
        if has_res:
            acc = r_ref[...] + acc
        o_ref[...] = acc

    @pl.when(i >= own_tiles)
    def _():
        o_ref[...] = jnp.zeros_like(o_ref)


def _mm(xs, w, gain=None, res=None, emit_xn=False, total_rows=None, into=None, row_offset=0, tm=1024, tn=1024):
    xs = xs if isinstance(xs, (list, tuple)) else [xs]
    m = xs[0].shape[0]
    k = sum(x.shape[1] for x in xs)
    n = w.shape[1]
    tm = min(tm, m)
    tn = min(tn, n)
    out_rows = into.shape[0] if into is not None else (total_rows or m)
    assert m % tm == 0 and n % tn == 0 and row_offset % tm == 0 and out_rows % tm == 0
    own_tiles = m // tm
    row_tiles = own_tiles if into is not None else out_rows // tm
    first = row_offset // tm
    own = lambda i: jnp.minimum(i, own_tiles - 1)
    in_specs = [pl.BlockSpec((tm, x.shape[1]), lambda i, j: (own(i), 0)) for x in xs]
    args = list(xs)
    if gain is not None:
        in_specs.append(pl.BlockSpec((1, k), lambda i, j: (0, 0)))
        args.append(gain.reshape(1, k).astype(F32))
    in_specs.append(pl.BlockSpec((k, tn), lambda i, j: (0, j)))
    args.append(w)
    if res is not None:
        in_specs.append(pl.BlockSpec((tm, tn), lambda i, j: (own(i), j)))
        args.append(res)
    aliases = {}
    if into is not None:
        aliases = {len(args): 0}
        in_specs.append(pl.BlockSpec(memory_space=pl.ANY))
        args.append(into)
    out_shape = [jax.ShapeDtypeStruct((out_rows, n), F32)]
    out_specs = [pl.BlockSpec((tm, tn), lambda i, j: (first + i, j))]
    if emit_xn:
        assert out_rows == m
        out_shape.append(jax.ShapeDtypeStruct((m, k), F32))
        out_specs.append(pl.BlockSpec((tm, k), lambda i, j: (i, 0)))
    outs = pl.pallas_call(
        functools.partial(_mm_body, n_x=len(xs), has_norm=gain is not None, has_res=res is not None,
                          has_into=into is not None, emit_xn=emit_xn, own_tiles=own_tiles),
        grid=(row_tiles, n // tn),
        in_specs=in_specs,
        out_specs=out_specs,
        out_shape=out_shape,
        scratch_shapes=[pltpu.VMEM((tm, k), BF16)],
        input_output_aliases=aliases,
        compiler_params=_params("parallel", "arbitrary"),
        name="norm_matmul",
    )(*args)
    return outs if emit_xn else outs[0]


def _rotary_tables(pos):
    half = HEAD_DIM // 2
    inv_freq = np.float32(ROPE_BASE) ** (-np.arange(half, dtype=np.float32) / np.float32(half))
    ang = pos.astype(np.float32)[:, None] * inv_freq[None, :]
    cos, sin = np.cos(ang), np.sin(ang)
    return (jnp.asarray(np.concatenate([cos, cos], axis=-1), F32),
            jnp.asarray(np.concatenate([-sin, sin], axis=-1), F32))


def _retention_tables(length):
    f32 = np.float32
    log_g = np.log1p(-np.exp2(f32(-5.0) - np.arange(N_RET_HEADS, dtype=f32)))
    i = np.arange(length, dtype=f32)
    diff = i[:, None] - i[None, :]
    decay = np.where(diff >= 0, np.exp(np.maximum(diff, f32(0.0))[None] * log_g[:, None, None]), f32(0.0))
    dq = np.exp((i[None, :] + f32(1.0)) * log_g[:, None])
    dk = np.exp((f32(length) - f32(1.0) - i)[None, :] * log_g[:, None])
    ds = np.exp(f32(length) * log_g)
    lanes = (N_RET_HEADS, length, HEAD_DIM)
    tables = (decay, np.broadcast_to(dq[:, :, None], lanes), np.broadcast_to(dk[:, :, None], lanes),
              np.broadcast_to(ds[:, None, None], (N_RET_HEADS, 8, HEAD_DIM)))
    return tuple(jnp.asarray(np.ascontiguousarray(a), F32) for a in tables)


def _rotate(x, cos, sin_signed):
    return x * cos + pltpu.roll(x, HEAD_DIM // 2, 1) * sin_signed


def _retention_step(q, k, v, s, decay, dq, dk, ds):
    qb, kb, vb = q.astype(BF16), k.astype(BF16), v.astype(BF16)
    scores = lax.dot_general(qb, kb, NT_DIMS, preferred_element_type=F32) * decay
    o = jnp.dot(scores.astype(BF16), vb, preferred_element_type=F32)
    o = o + jnp.dot((q * dq).astype(BF16), s.astype(BF16), preferred_element_type=F32)
    kd = (k * dk).astype(BF16)
    s_new = ds * s + lax.dot_general(kd, vb, TN_DIMS, preferred_element_type=F32)
    return o, s_new


def _gated_layernorm(o, gn, gate):
    c = o - jnp.mean(o, axis=-1, keepdims=True)
    y = c * lax.rsqrt(jnp.mean(c * c, axis=-1, keepdims=True) + NORM_EPS)
    return (y * gn) * _silu(gate)


def _ret_prompt_body(q_ref, k_ref, v_ref, g_ref, cos_ref, sin_ref, decay_ref, dq_ref, dk_ref, ds_ref, gn_ref,
                     o_ref, sfin_ref, s_ref):
    c = pl.program_id(0)

    @pl.when(c == 0)
    def _():
        s_ref[...] = jnp.zeros_like(s_ref)

    cos, sin = cos_ref[...], sin_ref[...]
    for h in range(N_RET_HEADS):
        cols = slice(h * HEAD_DIM, (h + 1) * HEAD_DIM)
        q = _rotate(q_ref[:, cols], cos, sin)
        k = _rotate(k_ref[:, cols], cos, sin) * QK_SCALE
        o, s_new = _retention_step(q, k, v_ref[:, cols], s_ref[h], decay_ref[h], dq_ref[h], dk_ref[h],
                                   ds_ref[h, 0:1, :])
        s_ref[h] = s_new
        o_ref[:, cols] = _gated_layernorm(o, gn_ref[:, cols], g_ref[:, cols])

    @pl.when(c == pl.num_programs(0) - 1)
    def _():
        sfin_ref[...] = s_ref[...]


def _retention_prompt(proj, pos, gn):
    t = proj.shape[0]
    cos, sin = _rotary_tables(pos)
    decay, dq, dk, ds = _retention_tables(RET_CHUNK)
    segment = lambda j: pl.BlockSpec((RET_CHUNK, RET_WIDTH), lambda c: (c, j))
    rot = pl.BlockSpec((RET_CHUNK, HEAD_DIM), lambda c: (c, 0))
    full = lambda a: pl.BlockSpec(a.shape, lambda c: (0,) * a.ndim)
    gn2 = gn.reshape(1, RET_WIDTH)
    state = jax.ShapeDtypeStruct((N_RET_HEADS, HEAD_DIM, HEAD_DIM), F32)
    return pl.pallas_call(
        _ret_prompt_body,
        grid=(t // RET_CHUNK,),
        in_specs=[segment(0), segment(1), segment(2), segment(3), rot, rot, full(decay), full(dq), full(dk), full(ds),
                  full(gn2)],
        out_specs=[segment(0), pl.BlockSpec(state.shape, lambda c: (0, 0, 0))],
        out_shape=[jax.ShapeDtypeStruct((t, RET_WIDTH), F32), state],
        scratch_shapes=[pltpu.VMEM(state.shape, F32)],
        compiler_params=_params("arbitrary"),
        name="retention_prompt",
    )(proj, proj, proj, proj, cos, sin, decay, dq, dk, ds, gn2)


def _ret_sample_body(q_ref, k_ref, v_ref, g_ref, st_ref, cos_ref, sin_ref, decay_ref, dq_ref, dk_ref, ds_ref, gn_ref,
                     o_ref, snew_ref):
    cos, sin = cos_ref[...], sin_ref[...]
    for r in range(q_ref.shape[0]):
        for h in range(N_RET_HEADS):
            cols = slice(h * HEAD_DIM, (h + 1) * HEAD_DIM)
            q = _rotate(q_ref[r, :, cols], cos, sin)
            k = _rotate(k_ref[r, :, cols], cos, sin) * QK_SCALE
            o, s_new = _retention_step(q, k, v_ref[r, :, cols], st_ref[r, h], decay_ref[h], dq_ref[h], dk_ref[h],
                                       ds_ref[h, 0:1, :])
            snew_ref[r, h] = s_new
            o_ref[r, :, cols] = _gated_layernorm(o, gn_ref[:, cols], g_ref[r, :, cols])


def _retention_sample(proj, state, pos, gn):
    b, l, _ = proj.shape
    cos, sin = _rotary_tables(pos)
    decay, dq, dk, ds = _retention_tables(l)
    per_step = RET_REQUESTS_PER_STEP
    assert b % per_step == 0
    segment = lambda j: pl.BlockSpec((per_step, l, RET_WIDTH), lambda i: (i, 0, j))
    st = pl.BlockSpec((per_step, N_RET_HEADS, HEAD_DIM, HEAD_DIM), lambda i: (i, 0, 0, 0))
    full = lambda a: pl.BlockSpec(a.shape, lambda i: (0,) * a.ndim)
    gn2 = gn.reshape(1, RET_WIDTH)
    return pl.pallas_call(
        _ret_sample_body,
        grid=(b // per_step,),
        in_specs=[segment(0), segment(1), segment(2), segment(3), st, full(cos), full(sin), full(decay), full(dq),
                  full(dk), full(ds), full(gn2)],
        out_specs=[segment(0), st],
        out_shape=[jax.ShapeDtypeStruct((b, l, RET_WIDTH), F32), jax.ShapeDtypeStruct(state.shape, F32)],
        compiler_params=_params("parallel"),
        name="retention_sample",
    )(proj, proj, proj, proj, state, cos, sin, decay, dq, dk, ds, gn2)


def _suffix_matrix(n):
    j = lax.broadcasted_iota(jnp.int32, (n, n), 0)
    s = lax.broadcasted_iota(jnp.int32, (n, n), 1)
    return jnp.where(j > s, 1.0, 0.0).astype(BF16)


def _neg_abs(x):
    bits = lax.bitcast_convert_type(x, jnp.uint32) | jnp.uint32(0x80000000)
    return lax.bitcast_convert_type(bits, F32)


def _sb_block_log2(z2, tri, mask):
    rows, keys = z2.shape
    chunk = min(rows, SB_CHUNK)
    suffix = tri.shape[1]
    log2_w, dropped = [], []
    for c in range(0, rows, chunk):
        parts, later = [], None
        for s in reversed(range(0, keys, suffix)):
            zc = z2[c:c + chunk, s:s + suffix]
            neg_log_keep = jnp.maximum(zc, 0.0) + jnp.log2(1.0 + jnp.exp2(_neg_abs(zc)))
            log_beta = zc - neg_log_keep
            if mask is not None:
                neg_log_keep = jnp.where(mask[c:c + chunk, s:s + suffix], neg_log_keep, 0.0)
            between = jnp.dot(neg_log_keep.astype(BF16), tri, preferred_element_type=F32)
            total = jnp.sum(neg_log_keep, axis=-1, keepdims=True)
            if later is None:
                parts.append(log_beta - between)
                later = total
            else:
                parts.append(log_beta - between - later)
                later = later + total
        log2_w.append(jnp.concatenate(parts[::-1], axis=1))
        dropped.append(later)
    return jnp.concatenate(log2_w, axis=0), jnp.concatenate(dropped, axis=0)


def _head_rmsnorm(o, gn):
    return (o * lax.rsqrt(jnp.mean(o * o, axis=-1, keepdims=True) + RMS_EPS)) * gn


def _sb_prompt_body(bias_ref, q_ref, k_ref, v_ref, gn_ref, o_ref, ko_ref, vo_ref, kb_ref, vb_ref, tri_ref):
    h = pl.program_id(0)
    i = pl.program_id(1)

    @pl.when(i == 0)
    def _():
        k, v = k_ref[...], v_ref[...]
        ko_ref[...] = k
        vo_ref[...] = v
        kb_ref[...] = k.astype(BF16)
        vb_ref[...] = v.astype(BF16)
        tri_ref[...] = _suffix_matrix(SB_KEYS)

    q = (q_ref[...] * (QK_SCALE * LOG2E)).astype(BF16)
    bias = bias_ref[h] * LOG2E
    tri = tri_ref[...]

    def key_rows(j):
        start = j * SB_KEYS
        return pl.ds(start if isinstance(j, int) else pl.multiple_of(start, SB_KEYS), SB_KEYS)

    def logits(j):
        return lax.dot_general(q, kb_ref[key_rows(j), :], NT_DIMS, preferred_element_type=F32) + bias

    def weighted_values(log2_w, j, mask=None):
        a = jnp.exp2(log2_w)
        if mask is not None:
            a = jnp.where(mask, a, 0.0)
        return jnp.dot(a.astype(BF16), vb_ref[key_rows(j), :], preferred_element_type=F32)

    ratio = SB_QUERIES // SB_KEYS
    r = lax.broadcasted_iota(jnp.int32, (SB_QUERIES, SB_KEYS), 0)
    c = lax.broadcasted_iota(jnp.int32, (SB_QUERIES, SB_KEYS), 1)
    acc = jnp.zeros((SB_QUERIES, HEAD_DIM), F32)
    drop = jnp.zeros((SB_QUERIES, 1), F32)
    for d in reversed(range(ratio)):
        mask = c + d * SB_KEYS < r
        log2_w, dropped = _sb_block_log2(logits(i * ratio + d), tri, mask)
        acc = acc + weighted_values(log2_w - drop, i * ratio + d, mask)
        drop = drop + dropped

    n_old = i * ratio

    def step(n, state):
        z_prev, log2_w_prev, acc, drop = state
        z_new = logits(jnp.maximum(n_old - 1 - n, 0))
        log2_w, dropped = _sb_block_log2(z_prev, tri, None)
        out = weighted_values(log2_w_prev, jnp.clip(n_old + 1 - n, 0, n_old))
        return z_new, log2_w - drop, acc + out, drop + dropped

    idle = jnp.full((SB_QUERIES, SB_KEYS), -1e30, F32)
    z0 = logits(jnp.maximum(n_old - 1, 0))
    _, log2_w_last, acc, _ = lax.fori_loop(1, n_old + 1, step, (z0, idle, acc, drop))
    acc = acc + weighted_values(log2_w_last, 0)
    o_ref[...] = _head_rmsnorm(acc, gn_ref[...])


def _sb_prompt(proj, bias, gn):
    t, width = proj.shape
    assert t % SB_QUERIES == 0 and SB_QUERIES % SB_KEYS == 0
    q0, k0, v0 = [(width - n * SB_WIDTH) // HEAD_DIM for n in (3, 2, 1)]
    qspec = pl.BlockSpec((SB_QUERIES, HEAD_DIM), lambda h, i, b: (i, q0 + h))
    kspec = pl.BlockSpec((t, HEAD_DIM), lambda h, i, b: (0, k0 + h))
    vspec = pl.BlockSpec((t, HEAD_DIM), lambda h, i, b: (0, v0 + h))
    return pl.pallas_call(
        _sb_prompt_body,
        grid_spec=pltpu.PrefetchScalarGridSpec(
            num_scalar_prefetch=1,
            grid=(N_SB_HEADS, t // SB_QUERIES),
            in_specs=[qspec, kspec, vspec, pl.BlockSpec((1, HEAD_DIM), lambda h, i, b: (0, h))],
            out_specs=[pl.BlockSpec((SB_QUERIES, HEAD_DIM), lambda h, i, b: (i, h)),
                       pl.BlockSpec((t, HEAD_DIM), lambda h, i, b: (0, h)),
                       pl.BlockSpec((t, HEAD_DIM), lambda h, i, b: (0, h))],
            scratch_shapes=[pltpu.VMEM((t, HEAD_DIM), BF16), pltpu.VMEM((t, HEAD_DIM), BF16),
                            pltpu.VMEM((SB_KEYS, SB_KEYS), BF16)],
        ),
        out_shape=[jax.ShapeDtypeStruct((t, SB_WIDTH), F32)] * 3,
        compiler_params=_params("parallel", "arbitrary"),
        name="stick_breaking_prompt",
    )(bias.astype(F32), proj, proj, proj, gn.reshape(1, SB_WIDTH))


def _sb_sample_body(pt_ref, bias_ref, q_ref, kn_ref, vn_ref, *rest):
    kp_refs = rest[:SB_PAGES_PER_STEP]
    vp_refs = rest[SB_PAGES_PER_STEP:2 * SB_PAGES_PER_STEP]
    gn_ref, o_ref, acc_ref, drop_ref = rest[2 * SB_PAGES_PER_STEP:]
    step = pl.program_id(1)
    l = q_ref.shape[1]
    rows = N_SB_HEADS * l
    tri = _suffix_matrix(PAGE_SIZE)
    bias = jnp.concatenate([jnp.full((l, 1), bias_ref[h] * LOG2E, F32) for h in range(N_SB_HEADS)], axis=0)
    qs = [(q_ref[0, :, h * HEAD_DIM:(h + 1) * HEAD_DIM] * (QK_SCALE * LOG2E)).astype(BF16)
          for h in range(N_SB_HEADS)]

    def logits(keys):
        z = [lax.dot_general(qs[h], keys[h], NT_DIMS, preferred_element_type=F32) for h in range(N_SB_HEADS)]
        return jnp.concatenate(z, axis=0) + bias

    def weighted_values(a, values):
        out = [jnp.dot(a[h * l:(h + 1) * l, :].astype(BF16), values[h], preferred_element_type=F32)
               for h in range(N_SB_HEADS)]
        return jnp.concatenate(out, axis=0)

    @pl.when(step == 0)
    def _():
        pad = jnp.zeros((PAGE_SIZE - l, HEAD_DIM), F32)
        kn = [jnp.concatenate([kn_ref[0, :, h * HEAD_DIM:(h + 1) * HEAD_DIM], pad], axis=0).astype(BF16)
              for h in range(N_SB_HEADS)]
        vn = [jnp.concatenate([vn_ref[0, :, h * HEAD_DIM:(h + 1) * HEAD_DIM], pad], axis=0).astype(BF16)
              for h in range(N_SB_HEADS)]
        qi = lax.broadcasted_iota(jnp.int32, (rows, PAGE_SIZE), 0) % l
        kj = lax.broadcasted_iota(jnp.int32, (rows, PAGE_SIZE), 1)
        mask = kj < qi
        log2_w, dropped = _sb_block_log2(logits(kn), tri, mask)
        acc_ref[...] = weighted_values(jnp.where(mask, jnp.exp2(log2_w), 0.0), vn)
        drop_ref[...] = dropped

    head_rows = lambda h: pl.ds(h, PAGE_SIZE, stride=N_SB_HEADS)
    z = jnp.concatenate([logits([kp_ref[head_rows(h), :].astype(BF16) for h in range(N_SB_HEADS)])
                         for kp_ref in kp_refs], axis=0)
    log2_w, dropped = _sb_block_log2(z, tri, None)
    drop = drop_ref[...]
    acc = acc_ref[...]
    for n, vp_ref in enumerate(vp_refs):
        page = slice(n * rows, (n + 1) * rows)
        a = jnp.exp2(log2_w[page, :] - drop)
        acc = acc + weighted_values(a, [vp_ref[head_rows(h), :].astype(BF16) for h in range(N_SB_HEADS)])
        drop = drop + dropped[page, :]
    acc_ref[...] = acc
    drop_ref[...] = drop

    @pl.when(step == pl.num_programs(1) - 1)
    def _():
        for h in range(N_SB_HEADS):
            cols = slice(h * HEAD_DIM, (h + 1) * HEAD_DIM)
            o_ref[0, :, cols] = _head_rmsnorm(acc[h * l:(h + 1) * l, :], gn_ref[:, cols])


def _sb_sample(proj, cache_k, cache_v, page_table, bias, gn):
    b, l, width = proj.shape
    n_pages = page_table.shape[1]
    assert n_pages % SB_PAGES_PER_STEP == 0
    n_phys = cache_k.shape[0]
    page_rows = PAGE_SIZE * N_SB_HEADS
    cache_k = cache_k.reshape(n_phys, page_rows, HEAD_DIM)
    cache_v = cache_v.reshape(n_phys, page_rows, HEAD_DIM)
    segment = lambda j: pl.BlockSpec((1, l, SB_WIDTH), lambda i, s, pt, bs: (i, 0, j))
    last = width // SB_WIDTH - 1

    def page(n):
        return pl.BlockSpec((None, page_rows, HEAD_DIM),
                            lambda i, s, pt, bs: (pt[i, n_pages - 1 - (s * SB_PAGES_PER_STEP + n)], 0, 0))

    pages = [page(n) for n in range(SB_PAGES_PER_STEP)]
    return pl.pallas_call(
        _sb_sample_body,
        grid_spec=pltpu.PrefetchScalarGridSpec(
            num_scalar_prefetch=2,
            grid=(b, n_pages // SB_PAGES_PER_STEP),
            in_specs=[segment(last - 2), segment(last - 1), segment(last)] + pages + pages
                     + [pl.BlockSpec((1, SB_WIDTH), lambda i, s, pt, bs: (0, 0))],
            out_specs=segment(0),
            scratch_shapes=[pltpu.VMEM((N_SB_HEADS * l, HEAD_DIM), F32), pltpu.VMEM((N_SB_HEADS * l, 1), F32)],
        ),
        out_shape=jax.ShapeDtypeStruct((b, l, SB_WIDTH), F32),
        compiler_params=_params("parallel", "arbitrary"),
        name="stick_breaking_sample",
    )(page_table, bias.astype(F32), proj, proj, proj, *([cache_k] * SB_PAGES_PER_STEP),
      *([cache_v] * SB_PAGES_PER_STEP), gn.reshape(1, SB_WIDTH))


def _cross_body(q_ref, *refs):
    group = q_ref.shape[0]
    k_refs, v_refs, o_ref = refs[:group], refs[group:2 * group], refs[2 * group]
    tq = q_ref.shape[1]
    n_mem = k_refs[0].shape[0] // N_MEM_HEADS
    pairs = [(g, h) for g in range(group) for h in range(N_MEM_HEADS)]
    cols = lambda h: slice(h * HEAD_DIM, (h + 1) * HEAD_DIM)
    head_rows = lambda h: pl.ds(h, n_mem, stride=N_MEM_HEADS)
    scores = [lax.dot_general(q_ref[g, :, cols(h)].astype(BF16), k_refs[g][head_rows(h), :].astype(BF16), NT_DIMS,
                              preferred_element_type=F32) for g, h in pairs]
    s = jnp.concatenate(scores, axis=0) * QK_SCALE
    e = jnp.exp(s - jnp.max(s, axis=-1, keepdims=True))
    p = e / jnp.sum(e, axis=-1, keepdims=True)
    for n, (g, h) in enumerate(pairs):
        v = v_refs[g][head_rows(h), :].astype(BF16)
        o_ref[g, :, cols(h)] = jnp.dot(p[n * tq:(n + 1) * tq, :].astype(BF16), v, preferred_element_type=F32)


def _cross_attention(q, mem_k, mem_v, tq, group):
    b, t, _ = q.shape
    m = mem_k.shape[1]
    tq = min(tq, t)
    assert b % group == 0 and t % tq == 0
    qspec = pl.BlockSpec((group, tq, MEM_WIDTH), lambda i, j: (i, j, 0))
    mspec = lambda g: pl.BlockSpec((None, m, HEAD_DIM), lambda i, j: (i * group + g, 0, 0))
    mspecs = [mspec(g) for g in range(group)]
    return pl.pallas_call(
        _cross_body,
        grid=(b // group, t // tq),
        in_specs=[qspec] + mspecs + mspecs,
        out_specs=qspec,
        out_shape=jax.ShapeDtypeStruct(q.shape, F32),
        compiler_params=_params("parallel", "arbitrary"),
        name="memory_cross_attention",
    )(q, *([mem_k] * group), *([mem_v] * group))


def _swiglu(x, wg_b, wu_b, wd_b):
    g = jnp.dot(x, wg_b[...], preferred_element_type=F32)
    u = jnp.dot(x, wu_b[...], preferred_element_type=F32)
    return jnp.dot((_silu(g) * u).astype(BF16), wd_b[...], preferred_element_type=F32)


def _cast_weights(wg_ref, wu_ref, wd_ref, wg_b, wu_b, wd_b):
    wg_b[...] = wg_ref[0].astype(BF16)
    wu_b[...] = wu_ref[0].astype(BF16)
    wd_b[...] = wd_ref[0].astype(BF16)


def _shared_body(x_ref, wg_ref, wu_ref, wd_ref, o_ref, wg_b, wu_b, wd_b):
    @pl.when(pl.program_id(0) == 0)
    def _():
        _cast_weights(wg_ref, wu_ref, wd_ref, wg_b, wu_b, wd_b)

    o_ref[...] = _swiglu(x_ref[...].astype(BF16), wg_b, wu_b, wd_b)


def _shared_mlp(x, w_gate, w_up, w_down, rows):
    r, d = x.shape
    f = w_gate.shape[2]
    wspec = lambda shape: pl.BlockSpec((1,) + shape, lambda i: (0, 0, 0))
    return pl.pallas_call(
        _shared_body,
        grid=(r // rows,),
        in_specs=[pl.BlockSpec((rows, d), lambda i: (i, 0)), wspec((d, f)), wspec((d, f)), wspec((f, d))],
        out_specs=pl.BlockSpec((rows, d), lambda i: (i, 0)),
        out_shape=jax.ShapeDtypeStruct((r, d), F32),
        scratch_shapes=[pltpu.VMEM((d, f), BF16), pltpu.VMEM((d, f), BF16), pltpu.VMEM((f, d), BF16)],
        compiler_params=_params("arbitrary"),
        name="shared_mlp",
    )(x, w_gate, w_up, w_down)


def _expert_body(be_ref, used_ref, shift_ref, tok_ref, x_hbm, wg_ref, wu_ref, wd_ref, o_ref, wg_b, wu_b, wd_b,
                 xbuf0, xbuf1, sem):
    b = pl.program_id(0)
    rows = o_ref.shape[0]
    n_used = used_ref[0]
    bufs = (xbuf0, xbuf1)
    last_entry = tok_ref.shape[0] - 1

    def row_token(first_entry, r):
        return tok_ref[jnp.minimum(first_entry + r, last_entry)]

    def first_entry(blk):
        return blk * rows - shift_ref[blk]

    def wait_gather(parity):
        pltpu.make_async_copy(x_hbm.at[pl.ds(0, rows), :], bufs[parity], sem.at[parity]).wait()

    @pl.when(b == 0)
    def _():
        def row(r, carry):
            src = x_hbm.at[pl.ds(row_token(first_entry(0), r), 1), :]
            pltpu.make_async_copy(src, xbuf0.at[pl.ds(r, 1), :], sem.at[0]).start()
            return carry

        lax.fori_loop(0, rows, row, 0, unroll=8)

    @pl.when(jnp.logical_and(b < n_used, jnp.logical_or(b == 0, be_ref[b] != be_ref[jnp.maximum(b - 1, 0)])))
    def _():
        _cast_weights(wg_ref, wu_ref, wd_ref, wg_b, wu_b, wd_b)

    def run(parity):
        nxt = first_entry(jnp.minimum(b + 1, n_used - 1))
        wait_gather(parity)
        x = bufs[parity][...].astype(BF16)
        for r in range(rows):
            src = x_hbm.at[pl.ds(row_token(nxt, r), 1), :]
            pltpu.make_async_copy(src, bufs[1 - parity].at[pl.ds(r, 1), :], sem.at[1 - parity]).start()
        o_ref[...] = _swiglu(x, wg_b, wu_b, wd_b)

        @pl.when(b == n_used - 1)
        def _():
            wait_gather(1 - parity)

    for parity in (0, 1):
        pl.when(jnp.logical_and(b < n_used, b % 2 == parity))(functools.partial(run, parity))

    @pl.when(b >= n_used)
    def _():
        o_ref[...] = jnp.zeros_like(o_ref)


def _expert_mlp(x, slot_token, block_shift, block_expert, n_used, w_gate, w_up, w_down, rows):
    d = x.shape[1]
    f = w_gate.shape[2]
    n_blocks = block_expert.shape[0]
    wspec = lambda shape: pl.BlockSpec((1,) + shape, lambda i, be, nu, sh, tok: (be[i], 0, 0))
    return pl.pallas_call(
        _expert_body,
        grid_spec=pltpu.PrefetchScalarGridSpec(
            num_scalar_prefetch=4,
            grid=(n_blocks,),
            in_specs=[pl.BlockSpec(memory_space=pl.ANY), wspec((d, f)), wspec((d, f)), wspec((f, d))],
            out_specs=pl.BlockSpec((rows, d), lambda i, be, nu, sh, tok: (i, 0)),
            scratch_shapes=[pltpu.VMEM((d, f), BF16), pltpu.VMEM((d, f), BF16), pltpu.VMEM((f, d), BF16),
                            pltpu.VMEM((rows, d), F32), pltpu.VMEM((rows, d), F32), pltpu.SemaphoreType.DMA((2,))],
        ),
        out_shape=jax.ShapeDtypeStruct((n_blocks * rows, d), F32),
        compiler_params=_params("arbitrary"),
        name="expert_mlp",
    )(block_expert, n_used, block_shift, slot_token, x, w_gate, w_up, w_down)


def _combine_body(slot_ref, x_ref, sh_ref, gate_ref, g_ref, yb_hbm, o0_ref, o1_ref, buf, sem, *, first_tiles):
    i = pl.program_id(0)
    tm = x_ref.shape[0]

    def start_gather(tile, b):
        base = tile * (tm * TOP_K)

        def token(r, carry):
            for k in range(TOP_K):
                src = yb_hbm.at[pl.ds(slot_ref[base + r * TOP_K + k], 1), :]
                pltpu.make_async_copy(src, buf.at[b, pl.ds(k * tm + r, 1), :], sem.at[b]).start(priority=k % 2)
            return carry

        lax.fori_loop(0, tm, token, 0)

    @pl.when(i == 0)
    def _():
        start_gather(0, 0)

    @pl.when(i + 1 < pl.num_programs(0))
    def _():
        start_gather(i + 1, (i + 1) % 2)

    b = i % 2
    pltpu.make_async_copy(yb_hbm.at[pl.ds(0, TOP_K * tm), :], buf.at[b], sem.at[b]).wait()
    routed = buf[b, 0:tm, :] * gate_ref[:, 0:1]
    for k in range(1, TOP_K):
        routed = routed + buf[b, k * tm:(k + 1) * tm, :] * gate_ref[:, k:k + 1]
    x = x_ref[...] + (routed + sh_ref[...])
    y = (x * lax.rsqrt(jnp.mean(x * x, axis=-1, keepdims=True) + RMS_EPS)) * g_ref[...]

    @pl.when(i < first_tiles)
    def _():
        o0_ref[...] = y

    @pl.when(i >= first_tiles)
    def _():
        o1_ref[...] = y


def _combine_final_norm(x, yb, slot, gates, shared, gain, split):
    m, d = x.shape
    tm = COMBINE_ROWS
    assert m % tm == 0 and split % tm == 0 and 0 < split < m and yb.shape[0] >= TOP_K * tm
    first_tiles = split // tm
    row = pl.BlockSpec((tm, d), lambda i, s: (i, 0))
    return pl.pallas_call(
        functools.partial(_combine_body, first_tiles=first_tiles),
        grid_spec=pltpu.PrefetchScalarGridSpec(
            num_scalar_prefetch=1,
            grid=(m // tm,),
            in_specs=[row, row, pl.BlockSpec((tm, TOP_K), lambda i, s: (i, 0)),
                      pl.BlockSpec((1, d), lambda i, s: (0, 0)), pl.BlockSpec(memory_space=pl.ANY)],
            out_specs=[pl.BlockSpec((tm, d), lambda i, s: (jnp.minimum(i, first_tiles - 1), 0)),
                       pl.BlockSpec((tm, d), lambda i, s: (jnp.maximum(i - first_tiles, 0), 0))],
            scratch_shapes=[pltpu.VMEM((2, TOP_K * tm, d), F32), pltpu.SemaphoreType.DMA((2,))],
        ),
        out_shape=[jax.ShapeDtypeStruct((split, d), F32), jax.ShapeDtypeStruct((m - split, d), F32)],
        compiler_params=_params("arbitrary"),
        name="combine_final_norm",
    )(slot.reshape(-1), x, shared, gates, gain.reshape(1, d), yb)


def _route_body(lg_ref, rb_ref, idx_ref, gate_ref, rank_ref, cnt_ref, seen_ref):
    @pl.when(pl.program_id(0) == 0)
    def _():
        seen_ref[...] = jnp.zeros_like(seen_ref)

    tm, n_exp = lg_ref.shape
    scores = _sigmoid(lg_ref[...])
    biased = scores + rb_ref[...]
    expert = lax.broadcasted_iota(jnp.int32, (tm, n_exp), 1).astype(F32)
    chosen = jnp.zeros((tm, n_exp), F32)
    picks = []
    for _ in range(TOP_K):
        best = jnp.max(biased, axis=-1, keepdims=True)
        first = jnp.min(jnp.where(biased == best, expert, float(n_exp)), axis=-1, keepdims=True)
        hit = expert == first
        picks.append((first, hit, jnp.sum(jnp.where(hit, scores, 0.0), axis=-1, keepdims=True)))
        biased = jnp.where(hit, -jnp.inf, biased)
        chosen = jnp.where(hit, 1.0, chosen)

    r = lax.broadcasted_iota(jnp.int32, (tm, tm), 0)
    c = lax.broadcasted_iota(jnp.int32, (tm, tm), 1)
    earlier = jnp.where(c < r, 1.0, 0.0).astype(BF16)
    rank_all = jnp.dot(earlier, chosen.astype(BF16), preferred_element_type=F32) + seen_ref[...]

    total = picks[0][2]
    for _, _, sel in picks[1:]:
        total = total + sel
    col = lax.broadcasted_iota(jnp.int32, (tm, TOP_K), 1)
    idx = jnp.zeros((tm, TOP_K), F32)
    gate = jnp.zeros((tm, TOP_K), F32)
    rank = jnp.zeros((tm, TOP_K), F32)
    for k, (first, hit, sel) in enumerate(picks):
        idx = jnp.where(col == k, first, idx)
        gate = jnp.where(col == k, sel / total * ROUTED_SCALE, gate)
        rank = jnp.where(col == k, jnp.sum(jnp.where(hit, rank_all, 0.0), axis=-1, keepdims=True), rank)
    idx_ref[...] = idx.astype(jnp.int32)
    gate_ref[...] = gate
    rank_ref[...] = rank.astype(jnp.int32)
    seen_ref[...] += jnp.sum(chosen, axis=0, keepdims=True)
    cnt_ref[...] = seen_ref[...].astype(jnp.int32)


def _route(logits, router_bias, tm=256):
    t, n_exp = logits.shape
    assert t % tm == 0
    row = lambda width: pl.BlockSpec((tm, width), lambda i: (i, 0))
    one = pl.BlockSpec((1, n_exp), lambda i: (0, 0))
    return pl.pallas_call(
        _route_body,
        grid=(t // tm,),
        in_specs=[row(n_exp), one],
        out_specs=[row(TOP_K), row(TOP_K), row(TOP_K), one],
        out_shape=[jax.ShapeDtypeStruct((t, TOP_K), jnp.int32), jax.ShapeDtypeStruct((t, TOP_K), F32),
                   jax.ShapeDtypeStruct((t, TOP_K), jnp.int32), jax.ShapeDtypeStruct((1, n_exp), jnp.int32)],
        scratch_shapes=[pltpu.VMEM((1, n_exp), F32)],
        compiler_params=_params("arbitrary"),
        name="router_topk",
    )(logits, router_bias.reshape(1, n_exp).astype(F32))


def _moe(hn, logits, router_bias, w_gate, w_up, w_down, ws_gate, ws_up, ws_down):
    t = logits.shape[0]
    idx, gates, rank, counts = _route(logits, router_bias)

    padded = (counts[0] + EXPERT_ROWS - 1) // EXPERT_ROWS * EXPERT_ROWS
    pad_end = jnp.cumsum(padded)
    pad_start = pad_end - padded
    experts = jnp.arange(N_EXPERTS, dtype=jnp.int32)
    slot = jnp.sum(jnp.where(idx[:, :, None] == experts, pad_start, 0), axis=-1) + rank
    n_blocks = -(-(t * TOP_K) // EXPERT_ROWS) + N_EXPERTS
    block_start = jnp.arange(n_blocks, dtype=jnp.int32) * EXPERT_ROWS
    block_e = jnp.minimum(jnp.sum(pad_end[None, :] <= block_start[:, None], axis=1), N_EXPERTS - 1).astype(jnp.int32)
    n_used = (pad_end[-1:] // EXPERT_ROWS).astype(jnp.int32)
    _, tok_sorted = lax.sort_key_val(slot.reshape(-1), jnp.repeat(jnp.arange(t, dtype=jnp.int32), TOP_K))
    first = jnp.cumsum(counts[0]) - counts[0]
    block_shift = (pad_start - first)[block_e].astype(jnp.int32)
    yb = _expert_mlp(hn, tok_sorted, block_shift, block_e, n_used, w_gate, w_up, w_down, EXPERT_ROWS)
    shared = _shared_mlp(hn, ws_gate[None], ws_up[None], ws_down[None], EXPERT_ROWS)
    return yb, slot.astype(jnp.int32), gates, shared


def kernel(x_prompt, x_sample, mem_prompt, cache_sb_k, cache_sb_v, page_table, state_ret, cache_mem_k, cache_mem_v,
           norm_mix, w_in, ret_gn, sb_gn, sb_bias, w_out, norm_mem, w_ck, w_cv, norm_cross, w_cq, w_co, norm_ffn,
           router_w, router_bias, w_gate, w_up, w_down, ws_gate, ws_up, ws_down, norm_final):
    depth = w_in.shape[0]
    assert depth == 1 and x_prompt.shape[0] == 1
    _, t, d = x_prompt.shape
    b, l, _ = x_sample.shape
    n_pages = page_table.shape[1]
    lyr = 0
    xp = x_prompt.reshape(t, d)
    xs = x_sample.reshape(b * l, d)
    pos_p = np.arange(t)
    pos_s = n_pages * PAGE_SIZE + np.arange(l)

    w_in_b = w_in[lyr].astype(BF16)
    w_out_b = w_out[lyr].astype(BF16)
    w_cq_b = w_cq[lyr].astype(BF16)
    w_co_b = w_co[lyr].astype(BF16)

    mem = mem_prompt.reshape(-1, d)
    n_mem = mem.shape[0]
    mkv = _mm(mem, jnp.concatenate([w_ck[lyr], w_cv[lyr]], axis=1).astype(BF16), gain=norm_mem[lyr])
    mk_p, mv_p = mkv[:, :MEM_WIDTH], mkv[:, MEM_WIDTH:]

    proj = _mm(xp, w_in_b, gain=norm_mix[lyr])
    o_ret, s_ret_p = _retention_prompt(proj, pos_p, ret_gn[lyr])
    o_sb, sk_p, sv_p = _sb_prompt(proj, sb_bias[lyr], sb_gn[lyr])
    xp = _mm([o_ret, o_sb], w_out_b, res=xp)

    proj = _mm(xs, w_in_b, gain=norm_mix[lyr]).reshape(b, l, -1)
    sk_s, sv_s = proj[:, :, -2 * SB_WIDTH:-SB_WIDTH], proj[:, :, -SB_WIDTH:]
    o_ret, s_ret_s = _retention_sample(proj, state_ret[lyr], pos_s, ret_gn[lyr])
    o_sb = _sb_sample(proj, cache_sb_k[lyr], cache_sb_v[lyr], page_table, sb_bias[lyr], sb_gn[lyr])
    xs = _mm([o_ret.reshape(b * l, -1), o_sb.reshape(b * l, -1)], w_out_b, res=xs)

    q = _mm(xp, w_cq_b, gain=norm_cross[lyr])
    mem_rows = n_mem * N_MEM_HEADS
    o = _cross_attention(q[None], mk_p.reshape(1, mem_rows, HEAD_DIM), mv_p.reshape(1, mem_rows, HEAD_DIM), tq=512,
                         group=1)
    x_all = _mm(o[0], w_co_b, res=xp, total_rows=t + b * l)
    q = _mm(xs, w_cq_b, gain=norm_cross[lyr]).reshape(b, l, MEM_WIDTH)
    o = _cross_attention(q, cache_mem_k[lyr].reshape(b, mem_rows, HEAD_DIM),
                         cache_mem_v[lyr].reshape(b, mem_rows, HEAD_DIM), tq=l, group=CROSS_REQUESTS_PER_STEP)
    x_all = _mm(o.reshape(b * l, MEM_WIDTH), w_co_b, res=xs, into=x_all, row_offset=t)

    logits, hn = _mm(x_all, router_w[lyr].astype(BF16), gain=norm_ffn[lyr], emit_xn=True)
    yb, slot, gates, shared = _moe(hn, logits, router_bias[lyr], w_gate[lyr], w_up[lyr], w_down[lyr],
                                   ws_gate[lyr], ws_up[lyr], ws_down[lyr])
    y_prompt, y_sample = _combine_final_norm(x_all, yb, slot, gates, shared, norm_final, split=t)
    y_prompt = y_prompt.reshape(1, t, d)
    y_sample = y_sample.reshape(b, l, d)
    return (y_prompt, y_sample,
            s_ret_p[None, None],
            sk_p.reshape(1, 1, t, N_SB_HEADS, HEAD_DIM), sv_p.reshape(1, 1, t, N_SB_HEADS, HEAD_DIM),
            mk_p.reshape(1, 1, n_mem, N_MEM_HEADS, HEAD_DIM), mv_p.reshape(1, 1, n_mem, N_MEM_HEADS, HEAD_DIM),
            s_ret_s[None],
            sk_s.reshape(1, b, l, N_SB_HEADS, HEAD_DIM), sv_s.reshape(1, b, l, N_SB_HEADS, HEAD_DIM))
```
